```python
import math
import jax, jax.numpy as jnp
from jax import lax
import numpy as np

D_MODEL = 1024
BATCH = 8
SEQ = 2048
DEPTH = 2
DEC_BATCH = 128
DEC_SEQ = 4
PAST_LEN = 16384
PAGE_SIZE = 128

N_MIXERS = 2
N_A_LAYERS = (DEPTH + 1) // 2
N_B_LAYERS = DEPTH // 2
S5_GROUP = 16
S5_GROUPS = D_MODEL // S5_GROUP
S5_STATE = 64
S5_DT_MIN = 1e-3
S5_DT_MAX = 1e-1
HG_EXPAND = 128
HG_HEADS = D_MODEL // HG_EXPAND
HG_DK = HG_EXPAND
HG_FDIM = HG_HEADS * HG_DK
HG_DV = D_MODEL // HG_HEADS
HG_CHUNK = 64
PEER_HEADS = 8
PEER_NKEYS = 128
PEER_EXPERTS = PEER_NKEYS * PEER_NKEYS
PEER_TOPK = 16
PEER_DKEY = 256
PEER_HALF = PEER_DKEY // 2
PEER_BLOCK = 256
RMS_EPS = 1e-6

kernel_name = "hybrid_s5_hgrn2_peer_step"

F32 = jnp.float32


def _rmsnorm(x, w):
    xf = x.astype(F32)
    y = xf * lax.rsqrt(jnp.mean(xf * xf, axis=-1, keepdims=True) + RMS_EPS)
    return (y * w.astype(F32)).astype(x.dtype)


def _complex_affine_combine(e1, e2):
    a1r, a1i, b1r, b1i = e1
    a2r, a2i, b2r, b2i = e2
    return (a1r * a2r - a1i * a2i,
            a1r * a2i + a1i * a2r,
            a2r * b1r - a2i * b1i + b2r,
            a2r * b1i + a2i * b1r + b2i)


def _s5_mixer(h, s0_re, s0_im, lam_re, lam_im, log_dt, b_re, b_im, c_re, c_im, d, w_glu):
    nb, L, _ = h.shape
    u = h.astype(F32).reshape(nb, L, S5_GROUPS, S5_GROUP)
    lr = jnp.minimum(lam_re.astype(F32), -1e-4)
    li = lam_im.astype(F32)
    dt = jnp.exp(log_dt.astype(F32))[:, None]
    mag = jnp.exp(lr * dt)
    ang = li * dt
    ab_re = mag * jnp.cos(ang)
    ab_im = mag * jnp.sin(ang)
    den = lr * lr + li * li
    nr = ab_re - 1.0
    zr = (nr * lr + ab_im * li) / den
    zi = (ab_im * lr - nr * li) / den
    br = b_re.astype(F32)
    bi = b_im.astype(F32)
    bb_re = zr[..., None] * br - zi[..., None] * bi
    bb_im = zr[..., None] * bi + zi[..., None] * br
    bu_re = jnp.einsum('blgh,gph->lbgp', u, bb_re)
    bu_im = jnp.einsum('blgh,gph->lbgp', u, bb_im)
    a_re = jnp.broadcast_to(ab_re[None, None], (L, 1, S5_GROUPS, S5_STATE))
    a_im = jnp.broadcast_to(ab_im[None, None], (L, 1, S5_GROUPS, S5_STATE))
    acr, aci, sr, si = lax.associative_scan(_complex_affine_combine, (a_re, a_im, bu_re, bu_im), axis=0)
    s0r = s0_re.astype(F32)[None]
    s0i = s0_im.astype(F32)[None]
    st_re = sr + acr * s0r - aci * s0i
    st_im = si + acr * s0i + aci * s0r
    y = (jnp.einsum('lbgp,ghp->blgh', st_re, c_re.astype(F32))
         - jnp.einsum('lbgp,ghp->blgh', st_im, c_im.astype(F32))
         + d.astype(F32).reshape(S5_GROUPS, S5_GROUP) * u)
    y = jax.nn.gelu(y.reshape(nb, L, D_MODEL), approximate=False).astype(h.dtype)
    z = y @ w_glu
    out = z[..., :D_MODEL] * jax.nn.sigmoid(z[..., D_MODEL:])
    return out, st_re[-1].astype(s0_re.dtype), st_im[-1].astype(s0_im.dtype)


def _gla_chunked(q, k, v, logf, S0):
    nb, L, H, DK = q.shape
    C = math.gcd(L, HG_CHUNK)
    NC = L // C

    def to_chunks(t):
        return t.reshape(nb, NC, C, H, t.shape[-1]).transpose(1, 0, 3, 2, 4)

    qc, kc, vc, lc = to_chunks(q), to_chunks(k), to_chunks(v), to_chunks(logf)
    b = jnp.cumsum(lc, axis=3)
    b_last = b[:, :, :, C - 1:C, :]
    b_mid = b[:, :, :, C // 2:C // 2 + 1, :]
    scores = jnp.einsum('nbhtk,nbhsk->nbhts', qc * jnp.exp(b - b_mid), kc * jnp.exp(b_mid - b))
    causal = jnp.tril(jnp.ones((C, C), dtype=bool))
    scores = jnp.where(causal, scores, 0.0)
    o_intra = jnp.einsum('nbhts,nbhsv->nbhtv', scores, vc)
    q_inter = qc * jnp.exp(b)
    kv = jnp.einsum('nbhsk,nbhsv->nbhkv', kc * jnp.exp(b_last - b), vc)
    decay = jnp.exp(b_last[:, :, :, 0, :])

    def step(S, xs):
        qi, dec, kvi = xs
        o = jnp.einsum('bhtk,bhkv->bhtv', qi, S)
        return dec[..., None] * S + kvi, o

    S_last, o_inter = lax.scan(step, S0, (q_inter, decay, kv))
    o = (o_intra + o_inter).transpose(1, 0, 3, 2, 4).reshape(nb, L, H, v.shape[-1])
    return o, S_last


def _hgrn2_mixer(h, S0, w_in, lb, norm_w, w_out):
    nb, L, _ = h.shape
    proj = h @ w_in
    q, fz, iv, g = jnp.split(proj, [HG_FDIM, 2 * HG_FDIM, 2 * HG_FDIM + D_MODEL], axis=-1)
    q = jax.nn.silu(q.astype(F32)).reshape(nb, L, HG_HEADS, HG_DK)
    fz = fz.astype(F32).reshape(nb, L, HG_HEADS, HG_DK)
    lbh = lb.astype(F32).reshape(HG_HEADS, HG_DK)
    logf = jnp.logaddexp(jnp.log(lbh), jnp.log1p(-lbh) + jax.nn.log_sigmoid(fz))
    k = (1.0 - lbh) * jax.nn.sigmoid(-fz)
    v = iv.astype(F32).reshape(nb, L, HG_HEADS, HG_DV)
    o, S_last = _gla_chunked(q, k, v, logf, S0.astype(F32))
    o = o * lax.rsqrt(jnp.mean(o * o, axis=-1, keepdims=True) + RMS_EPS)
    o = o * norm_w.astype(F32) * jax.nn.silu(g.astype(F32).reshape(nb, L, HG_HEADS, HG_DV))
    out = o.reshape(nb, L, D_MODEL).astype(h.dtype) @ w_out
    return out, S_last.astype(S0.dtype)


def _peer_ffn(h, w_q, keys1, keys2, u_tab, v_tab):
    nb, L, D = h.shape
    T = nb * L
    x = h.reshape(T, D)
    q = (x @ w_q).astype(F32).reshape(T, PEER_HEADS, PEER_DKEY)
    s1 = jnp.einsum('thd,nd->thn', q[..., :PEER_HALF], keys1.astype(F32))
    s2 = jnp.einsum('thd,nd->thn', q[..., PEER_HALF:], keys2.astype(F32))
    v1, i1 = lax.top_k(s1, PEER_TOPK)
    v2, i2 = lax.top_k(s2, PEER_TOPK)
    cand = (v1[..., :, None] + v2[..., None, :]).reshape(T, PEER_HEADS, PEER_TOPK * PEER_TOPK)
    cidx = (i1[..., :, None] * PEER_NKEYS + i2[..., None, :]).reshape(T, PEER_HEADS, PEER_TOPK * PEER_TOPK)
    sc, pos = lax.top_k(cand, PEER_TOPK)
    idx = jnp.take_along_axis(cidx, pos, axis=-1).reshape(T, PEER_HEADS * PEER_TOPK)
    gate = jax.nn.softmax(sc, axis=-1).reshape(T, PEER_HEADS * PEER_TOPK).astype(x.dtype)
    blk = min(PEER_BLOCK, T)
    nblk = -(-T // blk)
    pad = nblk * blk - T
    xb = jnp.pad(x, ((0, pad), (0, 0))).reshape(nblk, blk, D)
    ib = jnp.pad(idx, ((0, pad), (0, 0))).reshape(nblk, blk, PEER_HEADS * PEER_TOPK)
    gb = jnp.pad(gate, ((0, pad), (0, 0))).reshape(nblk, blk, PEER_HEADS * PEER_TOPK)

    def block(args):
        xi, ii, gi = args
        act = jax.nn.gelu(jnp.einsum('tkd,td->tk', u_tab[ii], xi), approximate=False) * gi
        return jnp.einsum('tk,tkd->td', act, v_tab[ii])

    out = lax.map(block, (xb, ib, gb)).reshape(nblk * blk, D)[:T]
    return out.reshape(nb, L, D)


def _trunk(x, s5_re, s5_im, s_hg, norm_mix, norm_ffn, norm_final, s5_lambda_re, s5_lambda_im,
           s5_log_dt, s5_b_re, s5_b_im, s5_c_re, s5_c_im, s5_d, s5_w_glu, hg_w_in, hg_lower_bounds,
           hg_norm_w, hg_w_out, peer_w_q, peer_keys1, peer_keys2, peer_u, peer_v):
    lb_all = jnp.cumsum(jax.nn.softmax(hg_lower_bounds.astype(F32), axis=0), axis=0)
    lb_all = lb_all - lb_all[0:1]
    new_re, new_im, new_hg = [], [], []
    for i in range(DEPTH):
        h = _rmsnorm(x, norm_mix[i])
        j = i // N_MIXERS
        if i % N_MIXERS == 0:
            y, sr, si = _s5_mixer(h, s5_re[j], s5_im[j], s5_lambda_re[j], s5_lambda_im[j], s5_log_dt[j],
                                  s5_b_re[j], s5_b_im[j], s5_c_re[j], s5_c_im[j], s5_d[j], s5_w_glu[j])
            new_re.append(sr)
            new_im.append(si)
        else:
            y, S = _hgrn2_mixer(h, s_hg[j], hg_w_in[j], lb_all[i], hg_norm_w[j], hg_w_out[j])
            new_hg.append(S)
        x = x + y
        x = x + _peer_ffn(_rmsnorm(x, norm_ffn[i]), peer_w_q[i], peer_keys1[i], peer_keys2[i],
                          peer_u[i], peer_v[i])
    return _rmsnorm(x, norm_final), jnp.stack(new_re), jnp.stack(new_im), jnp.stack(new_hg)


def setup_inputs(seed: int = 0) -> dict:
    key = jax.random.key(seed)
    ks = jax.random.split(key, 26)

    def nrm(k, shape, s):
        return jax.random.normal(k, shape, F32) * s

    n_idx = jnp.arange(S5_STATE, dtype=F32)
    g_shape = (N_A_LAYERS, S5_GROUPS, S5_STATE)
    return {
        "x_prompt": nrm(ks[0], (BATCH, SEQ, D_MODEL), 1.0),
        "x_sample": nrm(ks[1], (DEC_BATCH, DEC_SEQ, D_MODEL), 1.0),
        "state_s5_re": nrm(ks[2], (N_A_LAYERS, DEC_BATCH, S5_GROUPS, S5_STATE), 0.1),
        "state_s5_im": nrm(ks[3], (N_A_LAYERS, DEC_BATCH, S5_GROUPS, S5_STATE), 0.1),
        "state_hgrn": nrm(ks[4], (N_B_LAYERS, DEC_BATCH, HG_HEADS, HG_DK, HG_DV), 0.5),
        "norm_mix": 1.0 + nrm(ks[5], (DEPTH, D_MODEL), 0.01),
        "norm_ffn": 1.0 + nrm(ks[6], (DEPTH, D_MODEL), 0.01),
        "norm_final": 1.0 + nrm(ks[7], (D_MODEL,), 0.01),
        "s5_lambda_re": -0.5 + nrm(ks[8], g_shape, 0.01),
        "s5_lambda_im": math.pi * n_idx + nrm(ks[9], g_shape, 0.01),
        "s5_log_dt": jax.random.uniform(ks[10], (N_A_LAYERS, S5_GROUPS), F32,
                                        math.log(S5_DT_MIN), math.log(S5_DT_MAX)),
        "s5_b_re": nrm(ks[11], (N_A_LAYERS, S5_GROUPS, S5_STATE, S5_GROUP), (2 * S5_GROUP) ** -0.5),
        "s5_b_im": nrm(ks[12], (N_A_LAYERS, S5_GROUPS, S5_STATE, S5_GROUP), (2 * S5_GROUP) ** -0.5),
        "s5_c_re": nrm(ks[13], (N_A_LAYERS, S5_GROUPS, S5_GROUP, S5_STATE), S5_STATE ** -0.5),
        "s5_c_im": nrm(ks[14], (N_A_LAYERS, S5_GROUPS, S5_GROUP, S5_STATE), S5_STATE ** -0.5),
        "s5_d": nrm(ks[15], (N_A_LAYERS, D_MODEL), 1.0),
        "s5_w_glu": nrm(ks[16], (N_A_LAYERS, D_MODEL, 2 * D_MODEL), D_MODEL ** -0.5),
        "hg_w_in": nrm(ks[17], (N_B_LAYERS, D_MODEL, 2 * HG_FDIM + 2 * D_MODEL), D_MODEL ** -0.5),
        "hg_lower_bounds": nrm(ks[18], (DEPTH, HG_FDIM), 0.1),
        "hg_norm_w": 1.0 + nrm(ks[19], (N_B_LAYERS, HG_DV), 0.01),
        "hg_w_out": nrm(ks[20], (N_B_LAYERS, D_MODEL, D_MODEL), D_MODEL ** -0.5),
        "peer_w_q": nrm(ks[21], (DEPTH, D_MODEL, PEER_HEADS * PEER_DKEY), D_MODEL ** -0.5),
        "peer_keys1": nrm(ks[22], (DEPTH, PEER_NKEYS, PEER_HALF), PEER_HALF ** -0.5),
        "peer_keys2": nrm(ks[23], (DEPTH, PEER_NKEYS, PEER_HALF), PEER_HALF ** -0.5),
        "peer_u": nrm(ks[24], (DEPTH, PEER_EXPERTS, D_MODEL), D_MODEL ** -0.5),
        "peer_v": nrm(ks[25], (DEPTH, PEER_EXPERTS, D_MODEL), PEER_HEADS ** -0.5),
    }


def reference(x_prompt, x_sample, state_s5_re, state_s5_im, state_hgrn, norm_mix, norm_ffn, norm_final,
              s5_lambda_re, s5_lambda_im, s5_log_dt, s5_b_re, s5_b_im, s5_c_re, s5_c_im, s5_d, s5_w_glu,
              hg_w_in, hg_lower_bounds, hg_norm_w, hg_w_out, peer_w_q, peer_keys1, peer_keys2, peer_u, peer_v):
    nbp = x_prompt.shape[0]
    z_s5 = jnp.zeros((N_A_LAYERS, nbp, S5_GROUPS, S5_STATE), state_s5_re.dtype)
    z_hg = jnp.zeros((N_B_LAYERS, nbp, HG_HEADS, HG_DK, HG_DV), state_hgrn.dtype)
    y_prompt, s5_re_p, s5_im_p, hg_p = _trunk(
        x_prompt, z_s5, z_s5, z_hg, norm_mix, norm_ffn, norm_final, s5_lambda_re, s5_lambda_im,
        s5_log_dt, s5_b_re, s5_b_im, s5_c_re, s5_c_im, s5_d, s5_w_glu, hg_w_in, hg_lower_bounds,
        hg_norm_w, hg_w_out, peer_w_q, peer_keys1, peer_keys2, peer_u, peer_v)
    y_sample, s5_re_s, s5_im_s, hg_s = _trunk(
        x_sample, state_s5_re, state_s5_im, state_hgrn, norm_mix, norm_ffn, norm_final, s5_lambda_re,
        s5_lambda_im, s5_log_dt, s5_b_re, s5_b_im, s5_c_re, s5_c_im, s5_d, s5_w_glu, hg_w_in,
        hg_lower_bounds, hg_norm_w, hg_w_out, peer_w_q, peer_keys1, peer_keys2, peer_u, peer_v)
    return (y_prompt, y_sample, s5_re_p, s5_im_p, hg_p, s5_re_s, s5_im_s, hg_s)
```

```python
import functools
import math

import jax
import jax.numpy as jnp
from jax import lax
from jax.experimental import pallas as pl
from jax.experimental.pallas import tpu as pltpu

F32 = jnp.float32
BF16 = jnp.bfloat16

D_MODEL = 1024
DEPTH = 2
N_MIXERS = 2
S5_GROUP = 16
S5_GROUPS = D_MODEL // S5_GROUP
S5_STATE = 64
HG_HEADS = 8
HG_DK = 128
HG_DV = 128
HG_FDIM = HG_HEADS * HG_DK
HG_CHUNK = 64
PEER_HEADS = 8
PEER_NKEYS = 128
PEER_EXPERTS = PEER_NKEYS * PEER_NKEYS
PEER_TOPK = 16
PEER_DKEY = 256
PEER_HALF = PEER_DKEY // 2
RMS_EPS = 1e-6

LANES = 128
SUBLANES = 8
VMEM_LIMIT_BYTES = 56 * 2**20

S5_CB = 4
S5_CB_GROUPS = S5_GROUPS // S5_CB
S5_CB_IN = S5_CB_GROUPS * S5_GROUP
S5_CB_STATE = S5_CB_GROUPS * S5_STATE


def _const_spec(shape):
    nd = len(shape)
    return pl.BlockSpec(shape, lambda *_: (0,) * nd, pipeline_mode=pl.Buffered(1))


def _rmsnorm(x, w):
    ms = jnp.mean(x * x, axis=-1, keepdims=True)
    return x * lax.rsqrt(ms + RMS_EPS) * w


def _gelu(x):
    return 0.5 * x * (1.0 + lax.erf(x * (1.0 / math.sqrt(2.0))))


def _sigmoid(x):
    return 1.0 / (1.0 + jnp.exp(-x))


def _dot(a, b):
    return jnp.dot(a, b, preferred_element_type=F32)


def _dot_nt(a, b):
    return lax.dot_general(a, b, (((1,), (1,)), ((), ())), preferred_element_type=F32)


def _s5_discretize_kernel(lre_ref, lim_ref, ldt_ref, are_ref, aim_ref, zre_ref, zim_ref):
    lr = jnp.minimum(lre_ref[...], -1e-4)
    li = lim_ref[...]
    dt = jnp.exp(ldt_ref[...])
    mag = jnp.exp(lr * dt)
    ang = li * dt
    ab_re = mag * jnp.cos(ang)
    ab_im = mag * jnp.sin(ang)
    den = lr * lr + li * li
    nr = ab_re - 1.0
    are_ref[...] = ab_re
    aim_ref[...] = ab_im
    zre_ref[...] = (nr * lr + ab_im * li) / den
    zim_ref[...] = (ab_im * lr - nr * li) / den


def _s5_scale_b_kernel(zre_ref, zim_ref, bre_ref, bim_ref, ore_ref, oim_ref):
    zr = zre_ref[...]
    zi = zim_ref[...]
    br = bre_ref[...]
    bi = bim_ref[...]
    ore_ref[...] = zr * br - zi * bi
    oim_ref[...] = zr * bi + zi * br


def _s5_params(lam_re, lam_im, log_dt, b_re, b_im, c_re, c_im):
    g, p, hh = S5_GROUPS, S5_STATE, S5_GROUP
    gp = jax.ShapeDtypeStruct((g, p), F32)
    a_re, a_im, z_re, z_im = pl.pallas_call(
        _s5_discretize_kernel, out_shape=(gp, gp, gp, gp), name="s5_discretize",
    )(lam_re.astype(F32), lam_im.astype(F32), log_dt.astype(F32).reshape(g, 1))
    gph = jax.ShapeDtypeStruct((g * p, hh), F32)
    bb_re, bb_im = pl.pallas_call(
        _s5_scale_b_kernel, out_shape=(gph, gph), name="s5_scale_b",
    )(z_re.reshape(g * p, 1), z_im.reshape(g * p, 1),
      b_re.astype(F32).reshape(g * p, hh), b_im.astype(F32).reshape(g * p, hh))
    eye = jnp.eye(S5_CB_GROUPS, dtype=F32)

    def blockdiag_in(w):
        w4 = w.reshape(S5_CB, S5_CB_GROUPS, p, hh)
        return jnp.einsum('cgph,gk->cghkp', w4, eye).reshape(S5_CB, S5_CB_IN, S5_CB_STATE)

    def blockdiag_out(w):
        w4 = w.astype(F32).reshape(S5_CB, S5_CB_GROUPS, hh, p)
        return jnp.einsum('cghp,gk->cgpkh', w4, eye).reshape(S5_CB, S5_CB_STATE, S5_CB_IN)

    bbd = jnp.concatenate([blockdiag_in(bb_re), blockdiag_in(bb_im)], axis=-1).astype(BF16)
    ccd = jnp.concatenate([blockdiag_out(c_re), -blockdiag_out(c_im)], axis=1).astype(BF16)
    a = jnp.stack([a_re.reshape(S5_CB, S5_CB_STATE), a_im.reshape(S5_CB, S5_CB_STATE)], axis=1)
    return a, bbd, ccd


def _s5_pack_state(s_re, s_im):
    nb = s_re.shape[0]
    return jnp.stack([s_re.reshape(nb, S5_CB, S5_CB_STATE), s_im.reshape(nb, S5_CB, S5_CB_STATE)],
                     axis=2).reshape(nb, 2 * S5_GROUPS * S5_STATE)


def _s5_unpack_state(s):
    nb = s.shape[0]
    s4 = s.reshape(nb, S5_CB, 2, S5_CB_STATE)
    return (s4[:, :, 0].reshape(nb, S5_GROUPS, S5_STATE), s4[:, :, 1].reshape(nb, S5_GROUPS, S5_STATE))


def _s5_kernel(x_ref, s0_ref, nw_ref, a_ref, bbd_ref, ccd_ref, d_ref, wglu_ref,
               out_ref, sfin_ref, bu_ref, st_ref, y_ref, *, nb, lc):
    rows = nb * lc

    @pl.when(pl.program_id(0) == 0)
    def _():
        st_ref[...] = s0_ref[...]

    x = x_ref[...].reshape(rows, D_MODEL)
    h = _rmsnorm(x, nw_ref[...])
    hb = h.astype(BF16)
    nt = S5_CB_STATE // LANES
    for cb in range(S5_CB):
        bu = _dot(hb[:, cb * S5_CB_IN:(cb + 1) * S5_CB_IN], bbd_ref[cb])
        for k in range(2 * nt):
            bu_ref[k] = bu[:, k * LANES:(k + 1) * LANES]
        a_re = [jnp.broadcast_to(a_ref[cb, 0:1, k * LANES:(k + 1) * LANES], (SUBLANES, LANES))
                for k in range(nt)]
        a_im = [jnp.broadcast_to(a_ref[cb, 1:2, k * LANES:(k + 1) * LANES], (SUBLANES, LANES))
                for k in range(nt)]
        col = cb * 2 * S5_CB_STATE
        for j in range(nb // SUBLANES):
            brow = j * SUBLANES

            def step(t, carry, base=brow * lc):
                idx = pl.ds(base + t, SUBLANES, stride=lc)
                new_re, new_im = [], []
                for k in range(nt):
                    s_re, s_im = carry[k], carry[nt + k]
                    n_re = a_re[k] * s_re - a_im[k] * s_im + bu_ref[k, idx, :]
                    n_im = a_re[k] * s_im + a_im[k] * s_re + bu_ref[nt + k, idx, :]
                    bu_ref[k, idx, :] = n_re
                    bu_ref[nt + k, idx, :] = n_im
                    new_re.append(n_re)
                    new_im.append(n_im)
                return tuple(new_re) + tuple(new_im)

            carry = tuple(st_ref[brow:brow + SUBLANES, col + k * LANES:col + (k + 1) * LANES]
                          for k in range(2 * nt))
            if lc <= 8:
                for t in range(lc):
                    carry = step(t, carry)
            else:
                carry = lax.fori_loop(0, lc, step, carry, unroll=4)
            for k in range(2 * nt):
                st_ref[brow:brow + SUBLANES, col + k * LANES:col + (k + 1) * LANES] = carry[k]
        st_all = jnp.concatenate([bu_ref[k] for k in range(2 * nt)], axis=-1)
        y_ref[:, cb * S5_CB_IN:(cb + 1) * S5_CB_IN] = _dot(st_all.astype(BF16), ccd_ref[cb])
    y = _gelu(y_ref[...] + d_ref[...] * h)
    z = _dot(y.astype(BF16), wglu_ref[...])
    out = x + z[:, :D_MODEL] * _sigmoid(z[:, D_MODEL:])
    out_ref[...] = out.reshape(out_ref.shape)
    sfin_ref[...] = st_ref[...]


def _s5_layer(x, s0_re, s0_im, norm_w, a, bbd, ccd, d, w_glu, *, lc):
    nb, seq, _ = x.shape
    nstate = 2 * S5_GROUPS * S5_STATE
    rows = nb * lc
    if seq == lc:
        xin = x.reshape(rows, D_MODEL)
        x_spec = pl.BlockSpec((rows, D_MODEL), lambda c: (0, 0))
    else:
        xin = x
        x_spec = pl.BlockSpec((nb, lc, D_MODEL), lambda c: (0, c, 0))
    out, sfin = pl.pallas_call(
        functools.partial(_s5_kernel, nb=nb, lc=lc),
        grid=(seq // lc,),
        in_specs=[
            x_spec,
            _const_spec((nb, nstate)),
            _const_spec((1, D_MODEL)),
            _const_spec((S5_CB, 2, S5_CB_STATE)),
            _const_spec((S5_CB, S5_CB_IN, 2 * S5_CB_STATE)),
            _const_spec((S5_CB, 2 * S5_CB_STATE, S5_CB_IN)),
            _const_spec((1, D_MODEL)),
            _const_spec((D_MODEL, 2 * D_MODEL)),
        ],
        out_specs=(x_spec, pl.BlockSpec((nb, nstate), lambda c: (0, 0))),
        out_shape=(jax.ShapeDtypeStruct(xin.shape, F32), jax.ShapeDtypeStruct((nb, nstate), F32)),
        scratch_shapes=[
            pltpu.VMEM((2 * S5_CB_STATE // LANES, rows, LANES), F32),
            pltpu.VMEM((nb, nstate), F32),
            pltpu.VMEM((rows, D_MODEL), F32),
        ],
        compiler_params=pltpu.CompilerParams(
            dimension_semantics=("arbitrary",), vmem_limit_bytes=VMEM_LIMIT_BYTES),
        name="s5_layer",
    )(xin, _s5_pack_state(s0_re, s0_im), norm_w.reshape(1, D_MODEL), a, bbd, ccd,
      d.reshape(1, D_MODEL), w_glu)
    new_re, new_im = _s5_unpack_state(sfin)
    return out.reshape(x.shape), new_re, new_im


def _hgrn_gates(proj, lb):
    q = proj[:, 0:HG_FDIM]
    q = q * _sigmoid(q)
    fz = proj[:, HG_FDIM:2 * HG_FDIM]
    v = proj[:, 2 * HG_FDIM:2 * HG_FDIM + D_MODEL]
    g = proj[:, 2 * HG_FDIM + D_MODEL:]
    log_sig = jnp.minimum(fz, 0.0) - jnp.log1p(jnp.exp(-jnp.abs(fz)))
    t1 = jnp.log(lb)
    t2 = jnp.log1p(-lb) + log_sig
    hi = jnp.maximum(t1, t2)
    lo = jnp.minimum(t1, t2)
    logf = hi + jnp.log1p(jnp.exp(lo - hi))
    k = (1.0 - lb) * _sigmoid(-fz)
    return q, k, v, g, logf


def _hgrn_tile(q, k, v, g, logf, st_refs, nw, *, chunk):
    rows = q.shape[0]
    nseg = rows // chunk
    ri = lax.broadcasted_iota(jnp.int32, (rows, rows), 0)
    ci = lax.broadcasted_iota(jnp.int32, (rows, rows), 1)
    if nseg == 1:
        same = ci <= ri
    else:
        same = (ci <= ri) & ((ri // chunk) == (ci // chunk))
    tri = jnp.where(same, 1.0, 0.0).astype(F32)
    b = jnp.dot(tri, logf, precision=lax.Precision.HIGHEST, preferred_element_type=F32)
    rowid = lax.broadcasted_iota(jnp.int32, (rows, 1), 0)
    b_mid = b[chunk // 2:chunk // 2 + 1, :]
    b_last = b[chunk - 1:chunk, :]
    for s in range(1, nseg):
        in_s = rowid >= s * chunk
        b_mid = jnp.where(in_s, b[s * chunk + chunk // 2:s * chunk + chunk // 2 + 1, :], b_mid)
        b_last = jnp.where(in_s, b[s * chunk + chunk - 1:s * chunk + chunk, :], b_last)
    qs = (q * jnp.exp(b - b_mid)).astype(BF16)
    ks = (k * jnp.exp(b_mid - b)).astype(BF16)
    qi = (q * jnp.exp(b)).astype(BF16)
    kd = k * jnp.exp(b_last - b)
    vb = v.astype(BF16)
    outs = []
    for hd in range(HG_HEADS):
        sl = slice(hd * HG_DK, (hd + 1) * HG_DK)
        scores = jnp.where(same, _dot_nt(qs[:, sl], ks[:, sl]), 0.0)
        o = _dot(scores.astype(BF16), vb[:, sl])
        v_t = v[:, sl].T
        for s in range(nseg):
            st = st_refs[s][hd]
            o_s = _dot_nt(qi[:, sl], st.astype(BF16))
            kd_s = kd[:, sl]
            if nseg > 1:
                in_s = (rowid >= s * chunk) & (rowid < (s + 1) * chunk)
                o_s = jnp.where(in_s, o_s, 0.0)
                kd_s = jnp.where(in_s, kd_s, 0.0)
            o = o + o_s
            dec = jnp.exp(b[s * chunk + chunk - 1:s * chunk + chunk, sl])
            st_refs[s][hd] = dec * st + _dot(v_t.astype(BF16), kd_s.astype(BF16))
        o = o * lax.rsqrt(jnp.mean(o * o, axis=-1, keepdims=True) + RMS_EPS)
        gh = g[:, sl]
        outs.append(o * nw * (gh * _sigmoid(gh)))
    return jnp.concatenate(outs, axis=-1)


def _hgrn_prompt_kernel(x_ref, s0_ref, nw_ref, win_ref, lb_ref, hnw_ref, wout_ref,
                        out_ref, sout_ref, st_ref, y_ref, *, lc, chunk):
    c = pl.program_id(1)

    @pl.when(c == 0)
    def _():
        for hd in range(HG_HEADS):
            st_ref[hd] = s0_ref[0, hd].T

    x = x_ref[0]
    hb = _rmsnorm(x, nw_ref[...]).astype(BF16)
    proj = _dot(hb, win_ref[...])
    q, k, v, g, logf = _hgrn_gates(proj, lb_ref[...])
    for sub in range(lc // chunk):
        r = slice(sub * chunk, (sub + 1) * chunk)
        y_ref[r, :] = _hgrn_tile(q[r], k[r], v[r], g[r], logf[r], [st_ref], hnw_ref[...], chunk=chunk)
    out_ref[0] = x + _dot(y_ref[...].astype(BF16), wout_ref[...])

    @pl.when(c == pl.num_programs(1) - 1)
    def _():
        for hd in range(HG_HEADS):
            sout_ref[0, hd] = st_ref[hd].T


def _hgrn_sample_kernel(x_ref, s0_ref, nw_ref, win_ref, lb_ref, hnw_ref, wout_ref,
                        out_ref, sout_ref, st_ref, proj_ref, y_ref, *, seq):
    p = pl.program_id(0)
    nseg = SUBLANES // seq

    @pl.when(p == 0)
    def _():
        hb = _rmsnorm(x_ref[...], nw_ref[...]).astype(BF16)
        proj_ref[...] = _dot(hb, win_ref[...])

    for s in range(nseg):
        for hd in range(HG_HEADS):
            st_ref[s, hd] = s0_ref[s, hd].T
    r = pl.ds(pl.multiple_of(p * SUBLANES, SUBLANES), SUBLANES)
    q, k, v, g, logf = _hgrn_gates(proj_ref[r, :], lb_ref[...])
    y_ref[r, :] = _hgrn_tile(q, k, v, g, logf, [st_ref.at[s] for s in range(nseg)], hnw_ref[...],
                             chunk=seq)
    for s in range(nseg):
        for hd in range(HG_HEADS):
            sout_ref[s, hd] = st_ref[s, hd].T

    @pl.when(p == pl.num_programs(0) - 1)
    def _():
        out_ref[...] = x_ref[...] + _dot(y_ref[...].astype(BF16), wout_ref[...])


def _hgrn_layer(x, s0, norm_w, w_in, lb, hg_norm_w, w_out):
    nb, seq, _ = x.shape
    weights = (norm_w.reshape(1, D_MODEL), w_in.astype(BF16), lb.reshape(1, HG_FDIM),
               hg_norm_w.reshape(1, HG_DV), w_out.astype(BF16))
    w_specs = [
        _const_spec((1, D_MODEL)),
        _const_spec((D_MODEL, 2 * HG_FDIM + 2 * D_MODEL)),
        _const_spec((1, HG_FDIM)),
        _const_spec((1, HG_DV)),
        _const_spec((D_MODEL, D_MODEL)),
    ]
    s_shape = jax.ShapeDtypeStruct(s0.shape, F32)
    if seq >= HG_CHUNK:
        chunk = HG_CHUNK
        lc = 4 * chunk
        x_spec = pl.BlockSpec((1, lc, D_MODEL), lambda b, c: (b, c, 0))
        s_spec = pl.BlockSpec((1, HG_HEADS, HG_DK, HG_DV), lambda b, c: (b, 0, 0, 0))
        out, s_new = pl.pallas_call(
            functools.partial(_hgrn_prompt_kernel, lc=lc, chunk=chunk),
            grid=(nb, seq // lc),
            in_specs=[x_spec, s_spec] + w_specs,
            out_specs=(x_spec, s_spec),
            out_shape=(jax.ShapeDtypeStruct(x.shape, F32), s_shape),
            scratch_shapes=[
                pltpu.VMEM((HG_HEADS, HG_DV, HG_DK), F32),
                pltpu.VMEM((lc, D_MODEL), F32),
            ],
            compiler_params=pltpu.CompilerParams(
                dimension_semantics=("arbitrary", "arbitrary"), vmem_limit_bytes=VMEM_LIMIT_BYTES),
            name="hgrn_prompt",
        )(x, s0, *weights)
        return out, s_new
    assert SUBLANES % seq == 0 and nb % (SUBLANES // seq) == 0
    nseg = SUBLANES // seq
    rows = nb * seq
    x_spec = _const_spec((rows, D_MODEL))
    s_spec = pl.BlockSpec((nseg, HG_HEADS, HG_DK, HG_DV), lambda p: (p, 0, 0, 0))
    out, s_new = pl.pallas_call(
        functools.partial(_hgrn_sample_kernel, seq=seq),
        grid=(nb // nseg,),
        in_specs=[x_spec, s_spec] + w_specs,
        out_specs=(pl.BlockSpec((rows, D_MODEL), lambda p: (0, 0)), s_spec),
        out_shape=(jax.ShapeDtypeStruct((rows, D_MODEL), F32), s_shape),
        scratch_shapes=[
            pltpu.VMEM((nseg, HG_HEADS, HG_DV, HG_DK), F32),
            pltpu.VMEM((rows, 2 * HG_FDIM + 2 * D_MODEL), F32),
            pltpu.VMEM((rows, D_MODEL), F32),
        ],
        compiler_params=pltpu.CompilerParams(
            dimension_semantics=("arbitrary",), vmem_limit_bytes=VMEM_LIMIT_BYTES),
        name="hgrn_sample",
    )(x.reshape(rows, D_MODEL), s0, *weights)
    return out.reshape(x.shape), s_new


def _compare_exchange(v, i, j):
    hi = jnp.maximum(v[i], v[j])
    lo = jnp.minimum(v[i], v[j])
    v[i] = hi
    v[j] = lo


def _bitonic_merge_desc(v):
    n = len(v)
    j = n // 2
    while j >= 1:
        for i in range(n):
            if i & j == 0:
                _compare_exchange(v, i, i + j)
        j //= 2


def _sort_desc(v):
    n = len(v)
    k = 2
    while k <= n:
        j = k // 2
        while j >= 1:
            for i in range(n):
                l = i ^ j
                if l > i:
                    if i & k == 0:
                        _compare_exchange(v, i, l)
                    else:
                        _compare_exchange(v, l, i)
            j //= 2
        k *= 2
    return v


def _merge_top(a, b):
    n = len(a)
    v = [jnp.maximum(a[r], b[n - 1 - r]) for r in range(n)]
    _bitonic_merge_desc(v)
    return v


def _top16_rows(s):
    v = [s[SUBLANES * j:SUBLANES * (j + 1), :] for j in range(PEER_NKEYS // SUBLANES)]
    v = _sort_desc(v)
    for shift in (1, 2, 4):
        v = _merge_top(v, [pltpu.roll(x, shift, axis=0) for x in v])
    return v


def _peer_select_kernel(x_ref, nw_ref, wq_ref, k1_ref, k2_ref,
                        hb_ref, s1_ref, s2_ref, thr_ref, cc_ref, t1_ref, t2_ref):
    hb = _rmsnorm(x_ref[...], nw_ref[...]).astype(BF16)
    hb_ref[...] = hb
    qb = _dot(hb, wq_ref[...]).astype(BF16)
    k1 = k1_ref[...]
    k2 = k2_ref[...]
    for hd in range(PEER_HEADS):
        base = hd * PEER_DKEY
        s1 = _dot_nt(k1, qb[:, base:base + PEER_HALF])
        s2 = _dot_nt(k2, qb[:, base + PEER_HALF:base + PEER_DKEY])
        s1_ref[hd] = s1
        s2_ref[hd] = s2
        for s, t_ref in ((s1, t1_ref), (s2, t2_ref)):
            top = _top16_rows(s)
            for r in range(PEER_TOPK):
                t_ref[r, hd:hd + 1, :] = top[r][0:1, :]
    a = [t1_ref[r] for r in range(PEER_TOPK)]
    b = [t2_ref[r] for r in range(PEER_TOPK)]
    top = [a[0] + b[j] for j in range(PEER_TOPK)]
    for i in range(1, PEER_TOPK):
        top = _merge_top(top, [a[i] + b[j] for j in range(PEER_TOPK)])
    mx = top[0]
    z = jnp.exp(top[0] - mx)
    for r in range(1, PEER_TOPK):
        z = z + jnp.exp(top[r] - mx)
    thr_ref[...] = top[PEER_TOPK - 1]
    cc_ref[...] = mx + jnp.log(z)


def _peer_expert_kernel(hb_ref, x_ref, s1_ref, s2_ref, thr_ref, cc_ref, u_ref, v_ref, fw_ref,
                        out_ref, acc_ref, ht_ref, at_ref, *, tb, eb, tw, final_norm):
    e = pl.program_id(1)

    @pl.when(e == 0)
    def _():
        acc_ref[...] = jnp.zeros_like(acc_ref)

    ht_ref[...] = _dot_nt(u_ref[...].astype(BF16), hb_ref[...])

    def tile(i, carry):
        col = pl.ds(pl.multiple_of(i * tw, tw), tw)
        for r in range(eb // PEER_NKEYS):
            gate = jnp.zeros((PEER_NKEYS, tw), F32)
            for hd in range(PEER_HEADS):
                t = s2_ref[hd, :, col] + s1_ref[hd, r:r + 1, col]
                gate = gate + jnp.where(t >= thr_ref[hd:hd + 1, col],
                                        jnp.exp(t - cc_ref[hd:hd + 1, col]), 0.0)
            rr = slice(r * PEER_NKEYS, (r + 1) * PEER_NKEYS)
            at_ref[rr, col] = (_gelu(ht_ref[rr, col]) * gate).astype(BF16)
        return carry

    lax.fori_loop(0, tb // tw, tile, 0)
    acc_ref[...] += lax.dot_general(v_ref[...].astype(BF16), at_ref[...], (((0,), (0,)), ((), ())),
                                    preferred_element_type=F32)

    @pl.when(e == pl.num_programs(1) - 1)
    def _():
        y = x_ref[...] + acc_ref[...].T
        if final_norm:
            y = _rmsnorm(y, fw_ref[...])
        out_ref[...] = y


def _peer_layer(x, norm_w, w_q, keys1, keys2, u_tab, v_tab, final_w, *, final_norm):
    t_tot = x.shape[0]
    tb = min(512, t_tot)
    eb = 8 * PEER_NKEYS
    nt = t_tot // tb
    hb, s1, s2, thr, cc = pl.pallas_call(
        _peer_select_kernel,
        grid=(nt,),
        in_specs=[
            pl.BlockSpec((tb, D_MODEL), lambda i: (i, 0)),
            _const_spec((1, D_MODEL)),
            _const_spec((D_MODEL, PEER_HEADS * PEER_DKEY)),
            _const_spec((PEER_NKEYS, PEER_HALF)),
            _const_spec((PEER_NKEYS, PEER_HALF)),
        ],
        out_specs=(
            pl.BlockSpec((tb, D_MODEL), lambda i: (i, 0)),
            pl.BlockSpec((PEER_HEADS, PEER_NKEYS, tb), lambda i: (0, 0, i)),
            pl.BlockSpec((PEER_HEADS, PEER_NKEYS, tb), lambda i: (0, 0, i)),
            pl.BlockSpec((PEER_HEADS, tb), lambda i: (0, i)),
            pl.BlockSpec((PEER_HEADS, tb), lambda i: (0, i)),
        ),
        out_shape=(
            jax.ShapeDtypeStruct((t_tot, D_MODEL), BF16),
            jax.ShapeDtypeStruct((PEER_HEADS, PEER_NKEYS, t_tot), F32),
            jax.ShapeDtypeStruct((PEER_HEADS, PEER_NKEYS, t_tot), F32),
            jax.ShapeDtypeStruct((PEER_HEADS, t_tot), F32),
            jax.ShapeDtypeStruct((PEER_HEADS, t_tot), F32),
        ),
        scratch_shapes=[
            pltpu.VMEM((PEER_TOPK, PEER_HEADS, tb), F32),
            pltpu.VMEM((PEER_TOPK, PEER_HEADS, tb), F32),
        ],
        compiler_params=pltpu.CompilerParams(
            dimension_semantics=("arbitrary",), vmem_limit_bytes=VMEM_LIMIT_BYTES),
        name="peer_select",
    )(x, norm_w.reshape(1, D_MODEL), w_q.astype(BF16), keys1.astype(BF16), keys2.astype(BF16))

    tok = lambda i, e: (i, 0)
    out = pl.pallas_call(
        functools.partial(_peer_expert_kernel, tb=tb, eb=eb, tw=LANES, final_norm=final_norm),
        grid=(nt, PEER_EXPERTS // eb),
        in_specs=[
            pl.BlockSpec((tb, D_MODEL), tok),
            pl.BlockSpec((tb, D_MODEL), tok),
            pl.BlockSpec((PEER_HEADS, eb // PEER_NKEYS, tb), lambda i, e: (0, e, i)),
            pl.BlockSpec((PEER_HEADS, PEER_NKEYS, tb), lambda i, e: (0, 0, i)),
            pl.BlockSpec((PEER_HEADS, tb), lambda i, e: (0, i)),
            pl.BlockSpec((PEER_HEADS, tb), lambda i, e: (0, i)),
            pl.BlockSpec((eb, D_MODEL), lambda i, e: (e, 0)),
            pl.BlockSpec((eb, D_MODEL), lambda i, e: (e, 0)),
            _const_spec((1, D_MODEL)),
        ],
        out_specs=pl.BlockSpec((tb, D_MODEL), tok),
        out_shape=jax.ShapeDtypeStruct((t_tot, D_MODEL), F32),
        scratch_shapes=[
            pltpu.VMEM((D_MODEL, tb), F32),
            pltpu.VMEM((eb, tb), F32),
            pltpu.VMEM((eb, tb), BF16),
        ],
        compiler_params=pltpu.CompilerParams(
            dimension_semantics=("arbitrary", "arbitrary"), vmem_limit_bytes=VMEM_LIMIT_BYTES),
        name="peer_experts",
    )(hb, x, s1, s2, thr, cc, u_tab, v_tab, final_w.reshape(1, D_MODEL))
    return out


def _trunk(x, s5_re, s5_im, s_hg, p, s5_prm, lb_all):
    nb, seq, _ = x.shape
    new_re, new_im, new_hg = [], [], []
    for i in range(DEPTH):
        j = i // N_MIXERS
        if i % N_MIXERS == 0:
            a, bbd, ccd = s5_prm[j]
            x, sr, si = _s5_layer(x, s5_re[j], s5_im[j], p["norm_mix"][i], a, bbd, ccd,
                                  p["s5_d"][j], p["s5_w_glu"][j].astype(BF16), lc=min(seq, 64))
            new_re.append(sr)
            new_im.append(si)
        else:
            x, s_new = _hgrn_layer(x, s_hg[j], p["norm_mix"][i], p["hg_w_in"][j], lb_all[i],
                                   p["hg_norm_w"][j], p["hg_w_out"][j])
            new_hg.append(s_new)
        x = _peer_layer(x.reshape(nb * seq, D_MODEL), p["norm_ffn"][i], p["peer_w_q"][i],
                        p["peer_keys1"][i], p["peer_keys2"][i], p["peer_u"][i], p["peer_v"][i],
                        p["norm_final"], final_norm=(i == DEPTH - 1)).reshape(nb, seq, D_MODEL)
    return x, jnp.stack(new_re), jnp.stack(new_im), jnp.stack(new_hg)


def kernel(x_prompt, x_sample, state_s5_re, state_s5_im, state_hgrn, norm_mix, norm_ffn, norm_final,
           s5_lambda_re, s5_lambda_im, s5_log_dt, s5_b_re, s5_b_im, s5_c_re, s5_c_im, s5_d, s5_w_glu,
           hg_w_in, hg_lower_bounds, hg_norm_w, hg_w_out, peer_w_q, peer_keys1, peer_keys2, peer_u, peer_v):
    p = dict(norm_mix=norm_mix, norm_ffn=norm_ffn, norm_final=norm_final, s5_d=s5_d, s5_w_glu=s5_w_glu,
             hg_w_in=hg_w_in, hg_norm_w=hg_norm_w, hg_w_out=hg_w_out, peer_w_q=peer_w_q,
             peer_keys1=peer_keys1, peer_keys2=peer_keys2, peer_u=peer_u, peer_v=peer_v)
    n_a = s5_lambda_re.shape[0]
    s5_prm = [_s5_params(s5_lambda_re[j], s5_lambda_im[j], s5_log_dt[j], s5_b_re[j], s5_b_im[j],
                         s5_c_re[j], s5_c_im[j]) for j in range(n_a)]
    lb_all = jnp.cumsum(jax.nn.softmax(hg_lower_bounds.astype(F32), axis=0), axis=0)
    lb_all = lb_all - lb_all[0:1]
    nbp = x_prompt.shape[0]
    z_s5 = jnp.zeros((n_a, nbp, S5_GROUPS, S5_STATE), state_s5_re.dtype)
    z_hg = jnp.zeros((state_hgrn.shape[0], nbp, HG_HEADS, HG_DK, HG_DV), state_hgrn.dtype)
    y_p, re_p, im_p, hg_p = _trunk(x_prompt, z_s5, z_s5, z_hg, p, s5_prm, lb_all)
    y_s, re_s, im_s, hg_s = _trunk(x_sample, state_s5_re, state_s5_im, state_hgrn, p, s5_prm, lb_all)
    return (y_p, y_s, re_p, im_p, hg_p, re_s, im_s, hg_s)
```

```python
import functools
import math

import jax
import jax.numpy as jnp
from jax import lax
from jax.experimental import pallas as pl
from jax.experimental.pallas import tpu as pltpu

F32 = jnp.float32
BF16 = jnp.bfloat16

D_MODEL = 1024
DEPTH = 2
N_MIXERS = 2
S5_GROUP = 16
S5_GROUPS = D_MODEL // S5_GROUP
S5_STATE = 64
HG_HEADS = 8
HG_DK = 128
HG_DV = 128
HG_FDIM = HG_HEADS * HG_DK
HG_CHUNK = 64
PEER_HEADS = 8
PEER_NKEYS = 128
PEER_EXPERTS = PEER_NKEYS * PEER_NKEYS
PEER_TOPK = 16
PEER_DKEY = 256
PEER_HALF = PEER_DKEY // 2
RMS_EPS = 1e-6

LANES = 128
SUBLANES = 8
VMEM_LIMIT_BYTES = 56 * 2**20

S5_CB = 4
S5_CB_GROUPS = S5_GROUPS // S5_CB
S5_CB_IN = S5_CB_GROUPS * S5_GROUP
S5_CB_STATE = S5_CB_GROUPS * S5_STATE


def _const_spec(shape):
    nd = len(shape)
    return pl.BlockSpec(shape, lambda *_: (0,) * nd, pipeline_mode=pl.Buffered(1))


def _rmsnorm(x, w):
    ms = jnp.mean(x * x, axis=-1, keepdims=True)
    return x * lax.rsqrt(ms + RMS_EPS) * w


def _gelu(x):
    return 0.5 * x * (1.0 + lax.erf(x * (1.0 / math.sqrt(2.0))))


def _sigmoid(x):
    return 1.0 / (1.0 + jnp.exp(-x))


def _dot(a, b):
    return jnp.dot(a, b, preferred_element_type=F32)


def _dot_nt(a, b):
    return lax.dot_general(a, b, (((1,), (1,)), ((), ())), preferred_element_type=F32)


def _s5_discretize_kernel(lre_ref, lim_ref, ldt_ref, are_ref, aim_ref, zre_ref, zim_ref):
    lr = jnp.minimum(lre_ref[...], -1e-4)
    li = lim_ref[...]
    dt = jnp.exp(ldt_ref[...])
    mag = jnp.exp(lr * dt)
    ang = li * dt
    ab_re = mag * jnp.cos(ang)
    ab_im = mag * jnp.sin(ang)
    den = lr * lr + li * li
    nr = ab_re - 1.0
    are_ref[...] = ab_re
    aim_ref[...] = ab_im
    zre_ref[...] = (nr * lr + ab_im * li) / den
    zim_ref[...] = (ab_im * lr - nr * li) / den


def _s5_scale_b_kernel(zre_ref, zim_ref, bre_ref, bim_ref, ore_ref, oim_ref):
    zr = zre_ref[...]
    zi = zim_ref[...]
    br = bre_ref[...]
    bi = bim_ref[...]
    ore_ref[...] = zr * br - zi * bi
    oim_ref[...] = zr * bi + zi * br


def _s5_params(lam_re, lam_im, log_dt, b_re, b_im, c_re, c_im):
    g, p, hh = S5_GROUPS, S5_STATE, S5_GROUP
    gp = jax.ShapeDtypeStruct((g, p), F32)
    a_re, a_im, z_re, z_im = pl.pallas_call(
        _s5_discretize_kernel, out_shape=(gp, gp, gp, gp), name="s5_discretize",
    )(lam_re.astype(F32), lam_im.astype(F32), log_dt.astype(F32).reshape(g, 1))
    gph = jax.ShapeDtypeStruct((g * p, hh), F32)
    bb_re, bb_im = pl.pallas_call(
        _s5_scale_b_kernel, out_shape=(gph, gph), name="s5_scale_b",
    )(z_re.reshape(g * p, 1), z_im.reshape(g * p, 1),
      b_re.astype(F32).reshape(g * p, hh), b_im.astype(F32).reshape(g * p, hh))
    eye = jnp.eye(S5_CB_GROUPS, dtype=F32)

    def blockdiag_in(w):
        w4 = w.reshape(S5_CB, S5_CB_GROUPS, p, hh)
        return jnp.einsum('cgph,gk->cghkp', w4, eye).reshape(S5_CB, S5_CB_IN, S5_CB_STATE)

    def blockdiag_out(w):
        w4 = w.astype(F32).reshape(S5_CB, S5_CB_GROUPS, hh, p)
        return jnp.einsum('cghp,gk->cgpkh', w4, eye).reshape(S5_CB, S5_CB_STATE, S5_CB_IN)

    bbd = jnp.concatenate([blockdiag_in(bb_re), blockdiag_in(bb_im)], axis=-1).astype(BF16)
    ccd = jnp.concatenate([blockdiag_out(c_re), -blockdiag_out(c_im)], axis=1).astype(BF16)
    a = jnp.stack([a_re.reshape(S5_CB, S5_CB_STATE), a_im.reshape(S5_CB, S5_CB_STATE)], axis=1)
    return a, bbd, ccd


def _s5_pack_state(s_re, s_im):
    nb = s_re.shape[0]
    return jnp.stack([s_re.reshape(nb, S5_CB, S5_CB_STATE), s_im.reshape(nb, S5_CB, S5_CB_STATE)],
                     axis=2).reshape(nb, 2 * S5_GROUPS * S5_STATE)


def _s5_unpack_state(s):
    nb = s.shape[0]
    s4 = s.reshape(nb, S5_CB, 2, S5_CB_STATE)
    return (s4[:, :, 0].reshape(nb, S5_GROUPS, S5_STATE), s4[:, :, 1].reshape(nb, S5_GROUPS, S5_STATE))


def _s5_kernel(x_ref, s0_ref, nw_ref, a_ref, bbd_ref, ccd_ref, d_ref, wglu_ref,
               out_ref, sfin_ref, bu_ref, st_ref, y_ref, *, nb, lc):
    rows = nb * lc

    @pl.when(pl.program_id(0) == 0)
    def _():
        st_ref[...] = s0_ref[...]

    x = x_ref[...].reshape(rows, D_MODEL)
    h = _rmsnorm(x, nw_ref[...])
    hb = h.astype(BF16)
    nt = S5_CB_STATE // LANES
    for cb in range(S5_CB):
        bu = _dot(hb[:, cb * S5_CB_IN:(cb + 1) * S5_CB_IN], bbd_ref[cb])
        for k in range(2 * nt):
            bu_ref[k] = bu[:, k * LANES:(k + 1) * LANES]
        a_re = [jnp.broadcast_to(a_ref[cb, 0:1, k * LANES:(k + 1) * LANES], (SUBLANES, LANES))
                for k in range(nt)]
        a_im = [jnp.broadcast_to(a_ref[cb, 1:2, k * LANES:(k + 1) * LANES], (SUBLANES, LANES))
                for k in range(nt)]
        col = cb * 2 * S5_CB_STATE
        for j in range(nb // SUBLANES):
            brow = j * SUBLANES

            def step(t, carry, base=brow * lc):
                idx = pl.ds(base + t, SUBLANES, stride=lc)
                new_re, new_im = [], []
                for k in range(nt):
                    s_re, s_im = carry[k], carry[nt + k]
                    n_re = a_re[k] * s_re - a_im[k] * s_im + bu_ref[k, idx, :]
                    n_im = a_re[k] * s_im + a_im[k] * s_re + bu_ref[nt + k, idx, :]
                    bu_ref[k, idx, :] = n_re
                    bu_ref[nt + k, idx, :] = n_im
                    new_re.append(n_re)
                    new_im.append(n_im)
                return tuple(new_re) + tuple(new_im)

            carry = tuple(st_ref[brow:brow + SUBLANES, col + k * LANES:col + (k + 1) * LANES]
                          for k in range(2 * nt))
            if lc <= 8:
                for t in range(lc):
                    carry = step(t, carry)
            else:
                carry = lax.fori_loop(0, lc, step, carry, unroll=4)
            for k in range(2 * nt):
                st_ref[brow:brow + SUBLANES, col + k * LANES:col + (k + 1) * LANES] = carry[k]
        st_all = jnp.concatenate([bu_ref[k] for k in range(2 * nt)], axis=-1)
        y_ref[:, cb * S5_CB_IN:(cb + 1) * S5_CB_IN] = _dot(st_all.astype(BF16), ccd_ref[cb])
    y = _gelu(y_ref[...] + d_ref[...] * h)
    z = _dot(y.astype(BF16), wglu_ref[...])
    out = x + z[:, :D_MODEL] * _sigmoid(z[:, D_MODEL:])
    out_ref[...] = out.reshape(out_ref.shape)
    sfin_ref[...] = st_ref[...]


def _s5_layer(x, s0_re, s0_im, norm_w, a, bbd, ccd, d, w_glu, *, lc):
    nb, seq, _ = x.shape
    nstate = 2 * S5_GROUPS * S5_STATE
    rows = nb * lc
    if seq == lc:
        xin = x.reshape(rows, D_MODEL)
        x_spec = pl.BlockSpec((rows, D_MODEL), lambda c: (0, 0))
    else:
        xin = x
        x_spec = pl.BlockSpec((nb, lc, D_MODEL), lambda c: (0, c, 0))
    out, sfin = pl.pallas_call(
        functools.partial(_s5_kernel, nb=nb, lc=lc),
        grid=(seq // lc,),
        in_specs=[
            x_spec,
            _const_spec((nb, nstate)),
            _const_spec((1, D_MODEL)),
            _const_spec((S5_CB, 2, S5_CB_STATE)),
            _const_spec((S5_CB, S5_CB_IN, 2 * S5_CB_STATE)),
            _const_spec((S5_CB, 2 * S5_CB_STATE, S5_CB_IN)),
            _const_spec((1, D_MODEL)),
            _const_spec((D_MODEL, 2 * D_MODEL)),
        ],
        out_specs=(x_spec, pl.BlockSpec((nb, nstate), lambda c: (0, 0))),
        out_shape=(jax.ShapeDtypeStruct(xin.shape, F32), jax.ShapeDtypeStruct((nb, nstate), F32)),
        scratch_shapes=[
            pltpu.VMEM((2 * S5_CB_STATE // LANES, rows, LANES), F32),
            pltpu.VMEM((nb, nstate), F32),
            pltpu.VMEM((rows, D_MODEL), F32),
        ],
        compiler_params=pltpu.CompilerParams(
            dimension_semantics=("arbitrary",), vmem_limit_bytes=VMEM_LIMIT_BYTES),
        name="s5_layer",
    )(xin, _s5_pack_state(s0_re, s0_im), norm_w.reshape(1, D_MODEL), a, bbd, ccd,
      d.reshape(1, D_MODEL), w_glu)
    new_re, new_im = _s5_unpack_state(sfin)
    return out.reshape(x.shape), new_re, new_im


def _hgrn_gates(proj, lb):
    q = proj[:, 0:HG_FDIM]
    q = q * _sigmoid(q)
    fz = proj[:, HG_FDIM:2 * HG_FDIM]
    v = proj[:, 2 * HG_FDIM:2 * HG_FDIM + D_MODEL]
    g = proj[:, 2 * HG_FDIM + D_MODEL:]
    log_sig = jnp.minimum(fz, 0.0) - jnp.log1p(jnp.exp(-jnp.abs(fz)))
    t1 = jnp.log(lb)
    t2 = jnp.log1p(-lb) + log_sig
    hi = jnp.maximum(t1, t2)
    lo = jnp.minimum(t1, t2)
    logf = hi + jnp.log1p(jnp.exp(lo - hi))
    k = (1.0 - lb) * _sigmoid(-fz)
    return q, k, v, g, logf


def _hgrn_tile(q, k, v, g, logf, st_refs, nw, *, chunk):
    rows = q.shape[0]
    nseg = rows // chunk
    ri = lax.broadcasted_iota(jnp.int32, (rows, rows), 0)
    ci = lax.broadcasted_iota(jnp.int32, (rows, rows), 1)
    if nseg == 1:
        same = ci <= ri
    else:
        same = (ci <= ri) & ((ri // chunk) == (ci // chunk))
    tri = jnp.where(same, 1.0, 0.0).astype(F32)
    b = jnp.dot(tri, logf, precision=lax.Precision.HIGHEST, preferred_element_type=F32)
    rowid = lax.broadcasted_iota(jnp.int32, (rows, 1), 0)
    b_mid = b[chunk // 2:chunk // 2 + 1, :]
    b_last = b[chunk - 1:chunk, :]
    for s in range(1, nseg):
        in_s = rowid >= s * chunk
        b_mid = jnp.where(in_s, b[s * chunk + chunk // 2:s * chunk + chunk // 2 + 1, :], b_mid)
        b_last = jnp.where(in_s, b[s * chunk + chunk - 1:s * chunk + chunk, :], b_last)
    qs = (q * jnp.exp(b - b_mid)).astype(BF16)
    ks = (k * jnp.exp(b_mid - b)).astype(BF16)
    qi = (q * jnp.exp(b)).astype(BF16)
    kd = k * jnp.exp(b_last - b)
    vb = v.astype(BF16)
    outs = []
    for hd in range(HG_HEADS):
        sl = slice(hd * HG_DK, (hd + 1) * HG_DK)
        scores = jnp.where(same, _dot_nt(qs[:, sl], ks[:, sl]), 0.0)
        o = _dot(scores.astype(BF16), vb[:, sl])
        v_t = v[:, sl].T
        for s in range(nseg):
            st = st_refs[s][hd]
            o_s = _dot_nt(qi[:, sl], st.astype(BF16))
            kd_s = kd[:, sl]
            if nseg > 1:
                in_s = (rowid >= s * chunk) & (rowid < (s + 1) * chunk)
                o_s = jnp.where(in_s, o_s, 0.0)
                kd_s = jnp.where(in_s, kd_s, 0.0)
            o = o + o_s
            dec = jnp.exp(b[s * chunk + chunk - 1:s * chunk + chunk, sl])
            st_refs[s][hd] = dec * st + _dot(v_t.astype(BF16), kd_s.astype(BF16))
        o = o * lax.rsqrt(jnp.mean(o * o, axis=-1, keepdims=True) + RMS_EPS)
        gh = g[:, sl]
        outs.append(o * nw * (gh * _sigmoid(gh)))
    return jnp.concatenate(outs, axis=-1)


def _hgrn_prompt_kernel(x_ref, s0_ref, nw_ref, win_ref, lb_ref, hnw_ref, wout_ref,
                        out_ref, sout_ref, st_ref, y_ref, *, lc, chunk):
    c = pl.program_id(1)

    @pl.when(c == 0)
    def _():
        for hd in range(HG_HEADS):
            st_ref[hd] = s0_ref[0, hd].T

    x = x_ref[0]
    hb = _rmsnorm(x, nw_ref[...]).astype(BF16)
    proj = _dot(hb, win_ref[...])
    q, k, v, g, logf = _hgrn_gates(proj, lb_ref[...])
    for sub in range(lc // chunk):
        r = slice(sub * chunk, (sub + 1) * chunk)
        y_ref[r, :] = _hgrn_tile(q[r], k[r], v[r], g[r], logf[r], [st_ref], hnw_ref[...], chunk=chunk)
    out_ref[0] = x + _dot(y_ref[...].astype(BF16), wout_ref[...])

    @pl.when(c == pl.num_programs(1) - 1)
    def _():
        for hd in range(HG_HEADS):
            sout_ref[0, hd] = st_ref[hd].T


def _hgrn_sample_kernel(x_ref, s0_ref, nw_ref, win_ref, lb_ref, hnw_ref, wout_ref,
                        out_ref, sout_ref, st_ref, proj_ref, y_ref, *, seq):
    p = pl.program_id(0)
    nseg = SUBLANES // seq

    @pl.when(p == 0)
    def _():
        hb = _rmsnorm(x_ref[...], nw_ref[...]).astype(BF16)
        proj_ref[...] = _dot(hb, win_ref[...])

    for s in range(nseg):
        for hd in range(HG_HEADS):
            st_ref[s, hd] = s0_ref[s, hd].T
    r = pl.ds(pl.multiple_of(p * SUBLANES, SUBLANES), SUBLANES)
    q, k, v, g, logf = _hgrn_gates(proj_ref[r, :], lb_ref[...])
    y_ref[r, :] = _hgrn_tile(q, k, v, g, logf, [st_ref.at[s] for s in range(nseg)], hnw_ref[...],
                             chunk=seq)
    for s in range(nseg):
        for hd in range(HG_HEADS):
            sout_ref[s, hd] = st_ref[s, hd].T

    @pl.when(p == pl.num_programs(0) - 1)
    def _():
        out_ref[...] = x_ref[...] + _dot(y_ref[...].astype(BF16), wout_ref[...])


def _hgrn_layer(x, s0, norm_w, w_in, lb, hg_norm_w, w_out):
    nb, seq, _ = x.shape
    weights = (norm_w.reshape(1, D_MODEL), w_in.astype(BF16), lb.reshape(1, HG_FDIM),
               hg_norm_w.reshape(1, HG_DV), w_out.astype(BF16))
    w_specs = [
        _const_spec((1, D_MODEL)),
        _const_spec((D_MODEL, 2 * HG_FDIM + 2 * D_MODEL)),
        _const_spec((1, HG_FDIM)),
        _const_spec((1, HG_DV)),
        _const_spec((D_MODEL, D_MODEL)),
    ]
    s_shape = jax.ShapeDtypeStruct(s0.shape, F32)
    if seq >= HG_CHUNK:
        chunk = HG_CHUNK
        lc = 4 * chunk
        x_spec = pl.BlockSpec((1, lc, D_MODEL), lambda b, c: (b, c, 0))
        s_spec = pl.BlockSpec((1, HG_HEADS, HG_DK, HG_DV), lambda b, c: (b, 0, 0, 0))
        out, s_new = pl.pallas_call(
            functools.partial(_hgrn_prompt_kernel, lc=lc, chunk=chunk),
            grid=(nb, seq // lc),
            in_specs=[x_spec, s_spec] + w_specs,
            out_specs=(x_spec, s_spec),
            out_shape=(jax.ShapeDtypeStruct(x.shape, F32), s_shape),
            scratch_shapes=[
                pltpu.VMEM((HG_HEADS, HG_DV, HG_DK), F32),
                pltpu.VMEM((lc, D_MODEL), F32),
            ],
            compiler_params=pltpu.CompilerParams(
                dimension_semantics=("arbitrary", "arbitrary"), vmem_limit_bytes=VMEM_LIMIT_BYTES),
            name="hgrn_prompt",
        )(x, s0, *weights)
        return out, s_new
    assert SUBLANES % seq == 0 and nb % (SUBLANES // seq) == 0
    nseg = SUBLANES // seq
    rows = nb * seq
    x_spec = _const_spec((rows, D_MODEL))
    s_spec = pl.BlockSpec((nseg, HG_HEADS, HG_DK, HG_DV), lambda p: (p, 0, 0, 0))
    out, s_new = pl.pallas_call(
        functools.partial(_hgrn_sample_kernel, seq=seq),
        grid=(nb // nseg,),
        in_specs=[x_spec, s_spec] + w_specs,
        out_specs=(pl.BlockSpec((rows, D_MODEL), lambda p: (0, 0)), s_spec),
        out_shape=(jax.ShapeDtypeStruct((rows, D_MODEL), F32), s_shape),
        scratch_shapes=[
            pltpu.VMEM((nseg, HG_HEADS, HG_DV, HG_DK), F32),
            pltpu.VMEM((rows, 2 * HG_FDIM + 2 * D_MODEL), F32),
            pltpu.VMEM((rows, D_MODEL), F32),
        ],
        compiler_params=pltpu.CompilerParams(
            dimension_semantics=("arbitrary",), vmem_limit_bytes=VMEM_LIMIT_BYTES),
        name="hgrn_sample",
    )(x.reshape(rows, D_MODEL), s0, *weights)
    return out.reshape(x.shape), s_new


def _compare_exchange(v, i, j):
    hi = jnp.maximum(v[i], v[j])
    lo = jnp.minimum(v[i], v[j])
    v[i] = hi
    v[j] = lo


def _bitonic_merge_desc(v):
    n = len(v)
    j = n // 2
    while j >= 1:
        for i in range(n):
            if i & j == 0:
                _compare_exchange(v, i, i + j)
        j //= 2


def _sort_desc(v):
    n = len(v)
    k = 2
    while k <= n:
        j = k // 2
        while j >= 1:
            for i in range(n):
                l = i ^ j
                if l > i:
                    if i & k == 0:
                        _compare_exchange(v, i, l)
                    else:
                        _compare_exchange(v, l, i)
            j //= 2
        k *= 2
    return v


def _merge_top(a, b):
    n = len(a)
    v = [jnp.maximum(a[r], b[n - 1 - r]) for r in range(n)]
    _bitonic_merge_desc(v)
    return v


def _top16_rows(s):
    v = [s[SUBLANES * j:SUBLANES * (j + 1), :] for j in range(PEER_NKEYS // SUBLANES)]
    v = _sort_desc(v)
    for shift in (1, 2, 4):
        v = _merge_top(v, [pltpu.roll(x, shift, axis=0) for x in v])
    return v


def _candidate_top(a, b):
    top = [a[0] + b[j] for j in range(PEER_TOPK)]
    for i in range(1, PEER_TOPK):
        top = _merge_top(top, [a[i] + b[j] for j in range(PEER_TOPK)])
    return top


def _peer_select_kernel(x_ref, nw_ref, wq_ref, k1_ref, k2_ref,
                        hb_ref, s1_ref, s2_ref, thr_ref, t1_ref, t2_ref):
    hb = _rmsnorm(x_ref[...], nw_ref[...]).astype(BF16)
    hb_ref[...] = hb
    qb = _dot(hb, wq_ref[...]).astype(BF16)
    k1 = k1_ref[...]
    k2 = k2_ref[...]
    log2e = 1.0 / math.log(2.0)
    for hd in range(PEER_HEADS):
        base = hd * PEER_DKEY
        s1 = _dot_nt(k1, qb[:, base:base + PEER_HALF]) * log2e
        s2 = _dot_nt(k2, qb[:, base + PEER_HALF:base + PEER_DKEY]) * log2e
        s1_ref[hd] = s1
        s2_ref[hd] = s2
        for s, t_ref in ((s1, t1_ref), (s2, t2_ref)):
            top = _top16_rows(s)
            for r in range(PEER_TOPK):
                t_ref[r, hd:hd + 1, :] = top[r][0:1, :]
    a = [t1_ref[r] for r in range(PEER_TOPK)]
    b = [t2_ref[r] for r in range(PEER_TOPK)]
    top = _candidate_top(a, b)
    z = jnp.exp2(top[0] - top[0])
    for r in range(1, PEER_TOPK):
        z = z + jnp.exp2(top[r] - top[0])
    shift = top[0] + jnp.log2(z) + 1.0
    for hd in range(PEER_HEADS):
        s1_ref[hd] = s1_ref[hd] - shift[hd:hd + 1, :]
    thr_ref[...] = _candidate_top([v - shift for v in a], b)[PEER_TOPK - 1]


def _peer_expert_kernel(hb_ref, x_ref, s1_ref, s2_ref, thr_ref, u_ref, v_ref, fw_ref,
                        out_ref, acc_ref, ht_ref, at_ref, *, tb, eb, tw, final_norm):
    e = pl.program_id(1)

    @pl.when(e == 0)
    def _():
        acc_ref[...] = jnp.zeros_like(acc_ref)

    u = u_ref[...]
    v = v_ref[...]
    tiles = [slice(i * tw, (i + 1) * tw) for i in range(tb // tw)]
    for cols in tiles:
        ht_ref[:, cols] = _dot_nt(u, hb_ref[cols, :])
    for i, cols in enumerate(tiles):
        for lt in range(tw // LANES):
            col = slice(i * tw + lt * LANES, i * tw + (lt + 1) * LANES)
            for r in range(eb // PEER_NKEYS):
                gate = None
                for hd in range(PEER_HEADS):
                    t = s2_ref[hd, :, col] + s1_ref[hd, r:r + 1, col]
                    g = jnp.where(t >= thr_ref[hd:hd + 1, col], jnp.exp2(t), 0.0)
                    gate = g if gate is None else gate + g
                rr = slice(r * PEER_NKEYS, (r + 1) * PEER_NKEYS)
                h = ht_ref[rr, col]
                at_ref[rr, col] = (h * (1.0 + lax.erf(h * (1.0 / math.sqrt(2.0)))) * gate).astype(BF16)
        acc_ref[:, cols] += lax.dot_general(v, at_ref[:, cols], (((0,), (0,)), ((), ())),
                                            preferred_element_type=F32)

    @pl.when(e == pl.num_programs(1) - 1)
    def _():
        y = x_ref[...] + acc_ref[...].T
        if final_norm:
            y = _rmsnorm(y, fw_ref[...])
        out_ref[...] = y


def _peer_layer(x, norm_w, w_q, keys1, keys2, u_tab, v_tab, final_w, *, final_norm):
    t_tot = x.shape[0]
    tbs = min(512, t_tot)
    tb = min(1024, t_tot)
    eb = 8 * PEER_NKEYS
    hb, s1, s2, thr = pl.pallas_call(
        _peer_select_kernel,
        grid=(t_tot // tbs,),
        in_specs=[
            pl.BlockSpec((tbs, D_MODEL), lambda i: (i, 0)),
            _const_spec((1, D_MODEL)),
            _const_spec((D_MODEL, PEER_HEADS * PEER_DKEY)),
            _const_spec((PEER_NKEYS, PEER_HALF)),
            _const_spec((PEER_NKEYS, PEER_HALF)),
        ],
        out_specs=(
            pl.BlockSpec((tbs, D_MODEL), lambda i: (i, 0)),
            pl.BlockSpec((PEER_HEADS, PEER_NKEYS, tbs), lambda i: (0, 0, i)),
            pl.BlockSpec((PEER_HEADS, PEER_NKEYS, tbs), lambda i: (0, 0, i)),
            pl.BlockSpec((PEER_HEADS, tbs), lambda i: (0, i)),
        ),
        out_shape=(
            jax.ShapeDtypeStruct((t_tot, D_MODEL), BF16),
            jax.ShapeDtypeStruct((PEER_HEADS, PEER_NKEYS, t_tot), F32),
            jax.ShapeDtypeStruct((PEER_HEADS, PEER_NKEYS, t_tot), F32),
            jax.ShapeDtypeStruct((PEER_HEADS, t_tot), F32),
        ),
        scratch_shapes=[
            pltpu.VMEM((PEER_TOPK, PEER_HEADS, tbs), F32),
            pltpu.VMEM((PEER_TOPK, PEER_HEADS, tbs), F32),
        ],
        compiler_params=pltpu.CompilerParams(
            dimension_semantics=("arbitrary",), vmem_limit_bytes=VMEM_LIMIT_BYTES),
        name="peer_select",
    )(x, norm_w.reshape(1, D_MODEL), w_q.astype(BF16), keys1.astype(BF16), keys2.astype(BF16))

    tok = lambda i, e: (i, 0)
    once = pl.Buffered(1)
    out = pl.pallas_call(
        functools.partial(_peer_expert_kernel, tb=tb, eb=eb, tw=2 * LANES, final_norm=final_norm),
        grid=(t_tot // tb, PEER_EXPERTS // eb),
        in_specs=[
            pl.BlockSpec((tb, D_MODEL), tok, pipeline_mode=once),
            pl.BlockSpec((tb, D_MODEL), tok, pipeline_mode=once),
            pl.BlockSpec((PEER_HEADS, eb // PEER_NKEYS, tb), lambda i, e: (0, e, i)),
            pl.BlockSpec((PEER_HEADS, PEER_NKEYS, tb), lambda i, e: (0, 0, i), pipeline_mode=once),
            pl.BlockSpec((PEER_HEADS, tb), lambda i, e: (0, i), pipeline_mode=once),
            pl.BlockSpec((eb, D_MODEL), lambda i, e: (e, 0)),
            pl.BlockSpec((eb, D_MODEL), lambda i, e: (e, 0)),
            _const_spec((1, D_MODEL)),
        ],
        out_specs=pl.BlockSpec((tb, D_MODEL), tok),
        out_shape=jax.ShapeDtypeStruct((t_tot, D_MODEL), F32),
        scratch_shapes=[
            pltpu.VMEM((D_MODEL, tb), F32),
            pltpu.VMEM((eb, tb), F32),
            pltpu.VMEM((eb, tb), BF16),
        ],
        compiler_params=pltpu.CompilerParams(
            dimension_semantics=("arbitrary", "arbitrary"), vmem_limit_bytes=VMEM_LIMIT_BYTES),
        name="peer_experts",
    )(hb, x, s1, s2, thr, u_tab.astype(BF16), v_tab.astype(BF16), final_w.reshape(1, D_MODEL))
    return out


def _trunk(x, s5_re, s5_im, s_hg, p, s5_prm, lb_all):
    nb, seq, _ = x.shape
    new_re, new_im, new_hg = [], [], []
    for i in range(DEPTH):
        j = i // N_MIXERS
        if i % N_MIXERS == 0:
            a, bbd, ccd = s5_prm[j]
            x, sr, si = _s5_layer(x, s5_re[j], s5_im[j], p["norm_mix"][i], a, bbd, ccd,
                                  p["s5_d"][j], p["s5_w_glu"][j].astype(BF16), lc=min(seq, 64))
            new_re.append(sr)
            new_im.append(si)
        else:
            x, s_new = _hgrn_layer(x, s_hg[j], p["norm_mix"][i], p["hg_w_in"][j], lb_all[i],
                                   p["hg_norm_w"][j], p["hg_w_out"][j])
            new_hg.append(s_new)
        x = _peer_layer(x.reshape(nb * seq, D_MODEL), p["norm_ffn"][i], p["peer_w_q"][i],
                        p["peer_keys1"][i], p["peer_keys2"][i], p["peer_u"][i], p["peer_v"][i],
                        p["norm_final"], final_norm=(i == DEPTH - 1)).reshape(nb, seq, D_MODEL)
    return x, jnp.stack(new_re), jnp.stack(new_im), jnp.stack(new_hg)


def kernel(x_prompt, x_sample, state_s5_re, state_s5_im, state_hgrn, norm_mix, norm_ffn, norm_final,
           s5_lambda_re, s5_lambda_im, s5_log_dt, s5_b_re, s5_b_im, s5_c_re, s5_c_im, s5_d, s5_w_glu,
           hg_w_in, hg_lower_bounds, hg_norm_w, hg_w_out, peer_w_q, peer_keys1, peer_keys2, peer_u, peer_v):
    p = dict(norm_mix=norm_mix, norm_ffn=norm_ffn, norm_final=norm_final, s5_d=s5_d, s5_w_glu=s5_w_glu,
             hg_w_in=hg_w_in, hg_norm_w=hg_norm_w, hg_w_out=hg_w_out, peer_w_q=peer_w_q,
             peer_keys1=peer_keys1, peer_keys2=peer_keys2, peer_u=peer_u, peer_v=peer_v)
    n_a = s5_lambda_re.shape[0]
    s5_prm = [_s5_params(s5_lambda_re[j], s5_lambda_im[j], s5_log_dt[j], s5_b_re[j], s5_b_im[j],
                         s5_c_re[j], s5_c_im[j]) for j in range(n_a)]
    lb_all = jnp.cumsum(jax.nn.softmax(hg_lower_bounds.astype(F32), axis=0), axis=0)
    lb_all = lb_all - lb_all[0:1]
    nbp = x_prompt.shape[0]
    z_s5 = jnp.zeros((n_a, nbp, S5_GROUPS, S5_STATE), state_s5_re.dtype)
    z_hg = jnp.zeros((state_hgrn.shape[0], nbp, HG_HEADS, HG_DK, HG_DV), state_hgrn.dtype)
    y_p, re_p, im_p, hg_p = _trunk(x_prompt, z_s5, z_s5, z_hg, p, s5_prm, lb_all)
    y_s, re_s, im_s, hg_s = _trunk(x_sample, state_s5_re, state_s5_im, state_hgrn, p, s5_prm, lb_all)
    return (y_p, y_s, re_p, im_p, hg_p, re_s, im_s, hg_s)
```

```python
import functools
import math

import jax
import jax.numpy as jnp
from jax import lax
from jax.experimental import pallas as pl
from jax.experimental.pallas import tpu as pltpu

F32 = jnp.float32
BF16 = jnp.bfloat16

D_MODEL = 1024
DEPTH = 2
N_MIXERS = 2
S5_GROUP = 16
S5_GROUPS = D_MODEL // S5_GROUP
S5_STATE = 64
HG_HEADS = 8
HG_DK = 128
HG_DV = 128
HG_FDIM = HG_HEADS * HG_DK
HG_CHUNK = 64
PEER_HEADS = 8
PEER_NKEYS = 128
PEER_EXPERTS = PEER_NKEYS * PEER_NKEYS
PEER_TOPK = 16
PEER_DKEY = 256
PEER_HALF = PEER_DKEY // 2
RMS_EPS = 1e-6

LANES = 128
SUBLANES = 8
VMEM_LIMIT_BYTES = 56 * 2**20

S5_CB = 4
S5_CB_GROUPS = S5_GROUPS // S5_CB
S5_CB_IN = S5_CB_GROUPS * S5_GROUP
S5_CB_STATE = S5_CB_GROUPS * S5_STATE


def _const_spec(shape):
    nd = len(shape)
    return pl.BlockSpec(shape, lambda *_: (0,) * nd, pipeline_mode=pl.Buffered(1))


def _rmsnorm(x, w):
    ms = jnp.mean(x * x, axis=-1, keepdims=True)
    return x * lax.rsqrt(ms + RMS_EPS) * w


def _gelu(x):
    return 0.5 * x * (1.0 + lax.erf(x * (1.0 / math.sqrt(2.0))))


def _sigmoid(x):
    return 1.0 / (1.0 + jnp.exp(-x))


def _dot(a, b):
    return jnp.dot(a, b, preferred_element_type=F32)


def _dot_nt(a, b):
    return lax.dot_general(a, b, (((1,), (1,)), ((), ())), preferred_element_type=F32)


def _s5_discretize_kernel(lre_ref, lim_ref, ldt_ref, are_ref, aim_ref, zre_ref, zim_ref):
    lr = jnp.minimum(lre_ref[...], -1e-4)
    li = lim_ref[...]
    dt = jnp.exp(ldt_ref[...])
    mag = jnp.exp(lr * dt)
    ang = li * dt
    ab_re = mag * jnp.cos(ang)
    ab_im = mag * jnp.sin(ang)
    den = lr * lr + li * li
    nr = ab_re - 1.0
    are_ref[...] = ab_re
    aim_ref[...] = ab_im
    zre_ref[...] = (nr * lr + ab_im * li) / den
    zim_ref[...] = (ab_im * lr - nr * li) / den


def _s5_scale_b_kernel(zre_ref, zim_ref, bre_ref, bim_ref, ore_ref, oim_ref):
    zr = zre_ref[...]
    zi = zim_ref[...]
    br = bre_ref[...]
    bi = bim_ref[...]
    ore_ref[...] = zr * br - zi * bi
    oim_ref[...] = zr * bi + zi * br


def _s5_params(lam_re, lam_im, log_dt, b_re, b_im, c_re, c_im):
    g, p, hh = S5_GROUPS, S5_STATE, S5_GROUP
    gp = jax.ShapeDtypeStruct((g, p), F32)
    a_re, a_im, z_re, z_im = pl.pallas_call(
        _s5_discretize_kernel, out_shape=(gp, gp, gp, gp), name="s5_discretize",
    )(lam_re.astype(F32), lam_im.astype(F32), log_dt.astype(F32).reshape(g, 1))
    gph = jax.ShapeDtypeStruct((g * p, hh), F32)
    bb_re, bb_im = pl.pallas_call(
        _s5_scale_b_kernel, out_shape=(gph, gph), name="s5_scale_b",
    )(z_re.reshape(g * p, 1), z_im.reshape(g * p, 1),
      b_re.astype(F32).reshape(g * p, hh), b_im.astype(F32).reshape(g * p, hh))
    eye = jnp.eye(S5_CB_GROUPS, dtype=F32)

    def blockdiag_in(w):
        w4 = w.reshape(S5_CB, S5_CB_GROUPS, p, hh)
        return jnp.einsum('cgph,gk->cghkp', w4, eye).reshape(S5_CB, S5_CB_IN, S5_CB_STATE)

    def blockdiag_out(w):
        w4 = w.astype(F32).reshape(S5_CB, S5_CB_GROUPS, hh, p)
        return jnp.einsum('cghp,gk->cgpkh', w4, eye).reshape(S5_CB, S5_CB_STATE, S5_CB_IN)

    bbd = jnp.concatenate([blockdiag_in(bb_re), blockdiag_in(bb_im)], axis=-1).astype(BF16)
    ccd = jnp.concatenate([blockdiag_out(c_re), -blockdiag_out(c_im)], axis=1).astype(BF16)
    a = jnp.stack([a_re.reshape(S5_CB, S5_CB_STATE), a_im.reshape(S5_CB, S5_CB_STATE)], axis=1)
    return a, bbd, ccd


def _s5_pack_state(s_re, s_im):
    nb = s_re.shape[0]
    return jnp.stack([s_re.reshape(nb, S5_CB, S5_CB_STATE), s_im.reshape(nb, S5_CB, S5_CB_STATE)],
                     axis=2).reshape(nb, 2 * S5_GROUPS * S5_STATE)


def _s5_unpack_state(s):
    nb = s.shape[0]
    s4 = s.reshape(nb, S5_CB, 2, S5_CB_STATE)
    return (s4[:, :, 0].reshape(nb, S5_GROUPS, S5_STATE), s4[:, :, 1].reshape(nb, S5_GROUPS, S5_STATE))


def _s5_kernel(x_ref, s0_ref, nw_ref, a_ref, bbd_ref, ccd_ref, d_ref, wglu_ref,
               out_ref, sfin_ref, bu_ref, st_ref, hs_ref, ys_ref, *, nb, lc):
    rows = nb * lc
    nk = D_MODEL // LANES
    lane = lambda k: slice(k * LANES, (k + 1) * LANES)

    @pl.when(pl.program_id(0) == 0)
    def _():
        st_ref[...] = s0_ref[...]

    x = x_ref[...].reshape(rows, D_MODEL)
    h = _rmsnorm(x, nw_ref[...])
    if nb <= lc:
        for b in range(nb):
            for k in range(nk):
                hs_ref[k, pl.ds(b, lc, stride=nb), :] = h[b * lc:(b + 1) * lc, lane(k)]
        hb = jnp.concatenate([hs_ref[k] for k in range(nk)], axis=-1).astype(BF16)
    else:
        for k in range(nk):
            hs_ref[k] = h[:, lane(k)]
        hb = jnp.concatenate(
            [jnp.concatenate([hs_ref[k, pl.ds(t, nb, stride=lc), :] for t in range(lc)], axis=0)
             for k in range(nk)], axis=-1).astype(BF16)
    nt = S5_CB_STATE // LANES
    for cb in range(S5_CB):
        bu = _dot(hb[:, cb * S5_CB_IN:(cb + 1) * S5_CB_IN], bbd_ref[cb])
        for k in range(2 * nt):
            bu_ref[k] = bu[:, lane(k)]
        a_re = [jnp.broadcast_to(a_ref[cb, 0:1, lane(k)], (SUBLANES, LANES)) for k in range(nt)]
        a_im = [jnp.broadcast_to(a_ref[cb, 1:2, lane(k)], (SUBLANES, LANES)) for k in range(nt)]
        col = cb * 2 * S5_CB_STATE
        for j in range(nb // SUBLANES):
            brow = j * SUBLANES

            def step(t, carry, brow=brow):
                idx = pl.ds(pl.multiple_of(t * nb + brow, SUBLANES), SUBLANES)
                new_re, new_im = [], []
                for k in range(nt):
                    s_re, s_im = carry[k], carry[nt + k]
                    n_re = a_re[k] * s_re - a_im[k] * s_im + bu_ref[k, idx, :]
                    n_im = a_re[k] * s_im + a_im[k] * s_re + bu_ref[nt + k, idx, :]
                    bu_ref[k, idx, :] = n_re
                    bu_ref[nt + k, idx, :] = n_im
                    new_re.append(n_re)
                    new_im.append(n_im)
                return tuple(new_re) + tuple(new_im)

            carry = tuple(st_ref[brow:brow + SUBLANES, col + k * LANES:col + (k + 1) * LANES]
                          for k in range(2 * nt))
            if lc <= 8:
                for t in range(lc):
                    carry = step(t, carry)
            else:
                carry = lax.fori_loop(0, lc, step, carry, unroll=4)
            for k in range(2 * nt):
                st_ref[brow:brow + SUBLANES, col + k * LANES:col + (k + 1) * LANES] = carry[k]
        st_all = jnp.concatenate([bu_ref[k] for k in range(2 * nt)], axis=-1)
        y_cb = _dot(st_all.astype(BF16), ccd_ref[cb])
        for k in range(S5_CB_IN // LANES):
            ys_ref[cb * (S5_CB_IN // LANES) + k] = y_cb[:, lane(k)]
    if nb <= lc:
        y = jnp.concatenate(
            [jnp.concatenate([ys_ref[k, pl.ds(b, lc, stride=nb), :] for b in range(nb)], axis=0)
             for k in range(nk)], axis=-1)
    else:
        for t in range(lc):
            for k in range(nk):
                hs_ref[k, pl.ds(t, nb, stride=lc), :] = ys_ref[k, t * nb:(t + 1) * nb, :]
        y = jnp.concatenate([hs_ref[k] for k in range(nk)], axis=-1)
    y = _gelu(y + d_ref[...] * h)
    z = _dot(y.astype(BF16), wglu_ref[...])
    out = x + z[:, :D_MODEL] * _sigmoid(z[:, D_MODEL:])
    out_ref[...] = out.reshape(out_ref.shape)
    sfin_ref[...] = st_ref[...]


def _s5_layer(x, s0_re, s0_im, norm_w, a, bbd, ccd, d, w_glu, *, lc):
    nb, seq, _ = x.shape
    nstate = 2 * S5_GROUPS * S5_STATE
    rows = nb * lc
    if seq == lc:
        xin = x.reshape(rows, D_MODEL)
        x_spec = pl.BlockSpec((rows, D_MODEL), lambda c: (0, 0))
    else:
        xin = x
        x_spec = pl.BlockSpec((nb, lc, D_MODEL), lambda c: (0, c, 0))
    out, sfin = pl.pallas_call(
        functools.partial(_s5_kernel, nb=nb, lc=lc),
        grid=(seq // lc,),
        in_specs=[
            x_spec,
            _const_spec((nb, nstate)),
            _const_spec((1, D_MODEL)),
            _const_spec((S5_CB, 2, S5_CB_STATE)),
            _const_spec((S5_CB, S5_CB_IN, 2 * S5_CB_STATE)),
            _const_spec((S5_CB, 2 * S5_CB_STATE, S5_CB_IN)),
            _const_spec((1, D_MODEL)),
            _const_spec((D_MODEL, 2 * D_MODEL)),
        ],
        out_specs=(x_spec, pl.BlockSpec((nb, nstate), lambda c: (0, 0))),
        out_shape=(jax.ShapeDtypeStruct(xin.shape, F32), jax.ShapeDtypeStruct((nb, nstate), F32)),
        scratch_shapes=[
            pltpu.VMEM((2 * S5_CB_STATE // LANES, rows, LANES), F32),
            pltpu.VMEM((nb, nstate), F32),
            pltpu.VMEM((D_MODEL // LANES, rows, LANES), F32),
            pltpu.VMEM((D_MODEL // LANES, rows, LANES), F32),
        ],
        compiler_params=pltpu.CompilerParams(
            dimension_semantics=("arbitrary",), vmem_limit_bytes=VMEM_LIMIT_BYTES),
        name="s5_layer",
    )(xin, _s5_pack_state(s0_re, s0_im), norm_w.reshape(1, D_MODEL), a, bbd, ccd,
      d.reshape(1, D_MODEL), w_glu)
    new_re, new_im = _s5_unpack_state(sfin)
    return out.reshape(x.shape), new_re, new_im


def _hgrn_gates(proj, lb):
    q = proj[:, 0:HG_FDIM]
    q = q * _sigmoid(q)
    fz = proj[:, HG_FDIM:2 * HG_FDIM]
    v = proj[:, 2 * HG_FDIM:2 * HG_FDIM + D_MODEL]
    g = proj[:, 2 * HG_FDIM + D_MODEL:]
    log_sig = jnp.minimum(fz, 0.0) - jnp.log1p(jnp.exp(-jnp.abs(fz)))
    t1 = jnp.log(lb)
    t2 = jnp.log1p(-lb) + log_sig
    hi = jnp.maximum(t1, t2)
    lo = jnp.minimum(t1, t2)
    logf = hi + jnp.log1p(jnp.exp(lo - hi))
    k = (1.0 - lb) * _sigmoid(-fz)
    return q, k, v, g, logf


def _hgrn_tile(q, k, v, g, logf, st_refs, nw, *, chunk):
    rows = q.shape[0]
    nseg = rows // chunk
    ri = lax.broadcasted_iota(jnp.int32, (rows, rows), 0)
    ci = lax.broadcasted_iota(jnp.int32, (rows, rows), 1)
    if nseg == 1:
        same = ci <= ri
    else:
        same = (ci <= ri) & ((ri // chunk) == (ci // chunk))
    tri = jnp.where(same, 1.0, 0.0).astype(F32)
    b = jnp.dot(tri, logf, precision=lax.Precision.HIGHEST, preferred_element_type=F32)
    rowid = lax.broadcasted_iota(jnp.int32, (rows, 1), 0)
    b_mid = b[chunk // 2:chunk // 2 + 1, :]
    b_last = b[chunk - 1:chunk, :]
    for s in range(1, nseg):
        in_s = rowid >= s * chunk
        b_mid = jnp.where(in_s, b[s * chunk + chunk // 2:s * chunk + chunk // 2 + 1, :], b_mid)
        b_last = jnp.where(in_s, b[s * chunk + chunk - 1:s * chunk + chunk, :], b_last)
    qs = (q * jnp.exp(b - b_mid)).astype(BF16)
    ks = (k * jnp.exp(b_mid - b)).astype(BF16)
    qi = (q * jnp.exp(b)).astype(BF16)
    kd = k * jnp.exp(b_last - b)
    vb = v.astype(BF16)
    outs = []
    for hd in range(HG_HEADS):
        sl = slice(hd * HG_DK, (hd + 1) * HG_DK)
        scores = jnp.where(same, _dot_nt(qs[:, sl], ks[:, sl]), 0.0)
        o = _dot(scores.astype(BF16), vb[:, sl])
        v_t = v[:, sl].T
        for s in range(nseg):
            st = st_refs[s][hd]
            o_s = _dot_nt(qi[:, sl], st.astype(BF16))
            kd_s = kd[:, sl]
            if nseg > 1:
                in_s = (rowid >= s * chunk) & (rowid < (s + 1) * chunk)
                o_s = jnp.where(in_s, o_s, 0.0)
                kd_s = jnp.where(in_s, kd_s, 0.0)
            o = o + o_s
            dec = jnp.exp(b[s * chunk + chunk - 1:s * chunk + chunk, sl])
            st_refs[s][hd] = dec * st + _dot(v_t.astype(BF16), kd_s.astype(BF16))
        o = o * lax.rsqrt(jnp.mean(o * o, axis=-1, keepdims=True) + RMS_EPS)
        gh = g[:, sl]
        outs.append(o * nw * (gh * _sigmoid(gh)))
    return jnp.concatenate(outs, axis=-1)


def _hgrn_prompt_kernel(x_ref, s0_ref, nw_ref, win_ref, lb_ref, hnw_ref, wout_ref,
                        out_ref, sout_ref, st_ref, y_ref, *, lc, chunk):
    c = pl.program_id(1)

    @pl.when(c == 0)
    def _():
        for hd in range(HG_HEADS):
            st_ref[hd] = s0_ref[0, hd].T

    x = x_ref[0]
    hb = _rmsnorm(x, nw_ref[...]).astype(BF16)
    proj = _dot(hb, win_ref[...])
    q, k, v, g, logf = _hgrn_gates(proj, lb_ref[...])
    for sub in range(lc // chunk):
        r = slice(sub * chunk, (sub + 1) * chunk)
        y_ref[r, :] = _hgrn_tile(q[r], k[r], v[r], g[r], logf[r], [st_ref], hnw_ref[...], chunk=chunk)
    out_ref[0] = x + _dot(y_ref[...].astype(BF16), wout_ref[...])

    @pl.when(c == pl.num_programs(1) - 1)
    def _():
        for hd in range(HG_HEADS):
            sout_ref[0, hd] = st_ref[hd].T


def _hgrn_sample_kernel(x_ref, s0_ref, nw_ref, win_ref, lb_ref, hnw_ref, wout_ref,
                        out_ref, sout_ref, st_ref, proj_ref, y_ref, *, seq):
    p = pl.program_id(0)
    nseg = SUBLANES // seq

    @pl.when(p == 0)
    def _():
        hb = _rmsnorm(x_ref[...], nw_ref[...]).astype(BF16)
        proj_ref[...] = _dot(hb, win_ref[...])

    for s in range(nseg):
        for hd in range(HG_HEADS):
            st_ref[s, hd] = s0_ref[s, hd].T
    r = pl.ds(pl.multiple_of(p * SUBLANES, SUBLANES), SUBLANES)
    q, k, v, g, logf = _hgrn_gates(proj_ref[r, :], lb_ref[...])
    y_ref[r, :] = _hgrn_tile(q, k, v, g, logf, [st_ref.at[s] for s in range(nseg)], hnw_ref[...],
                             chunk=seq)
    for s in range(nseg):
        for hd in range(HG_HEADS):
            sout_ref[s, hd] = st_ref[s, hd].T

    @pl.when(p == pl.num_programs(0) - 1)
    def _():
        out_ref[...] = x_ref[...] + _dot(y_ref[...].astype(BF16), wout_ref[...])


def _hgrn_layer(x, s0, norm_w, w_in, lb, hg_norm_w, w_out):
    nb, seq, _ = x.shape
    weights = (norm_w.reshape(1, D_MODEL), w_in.astype(BF16), lb.reshape(1, HG_FDIM),
               hg_norm_w.reshape(1, HG_DV), w_out.astype(BF16))
    w_specs = [
        _const_spec((1, D_MODEL)),
        _const_spec((D_MODEL, 2 * HG_FDIM + 2 * D_MODEL)),
        _const_spec((1, HG_FDIM)),
        _const_spec((1, HG_DV)),
        _const_spec((D_MODEL, D_MODEL)),
    ]
    s_shape = jax.ShapeDtypeStruct(s0.shape, F32)
    if seq >= HG_CHUNK:
        chunk = HG_CHUNK
        lc = 4 * chunk
        x_spec = pl.BlockSpec((1, lc, D_MODEL), lambda b, c: (b, c, 0))
        s_spec = pl.BlockSpec((1, HG_HEADS, HG_DK, HG_DV), lambda b, c: (b, 0, 0, 0))
        out, s_new = pl.pallas_call(
            functools.partial(_hgrn_prompt_kernel, lc=lc, chunk=chunk),
            grid=(nb, seq // lc),
            in_specs=[x_spec, s_spec] + w_specs,
            out_specs=(x_spec, s_spec),
            out_shape=(jax.ShapeDtypeStruct(x.shape, F32), s_shape),
            scratch_shapes=[
                pltpu.VMEM((HG_HEADS, HG_DV, HG_DK), F32),
                pltpu.VMEM((lc, D_MODEL), F32),
            ],
            compiler_params=pltpu.CompilerParams(
                dimension_semantics=("arbitrary", "arbitrary"), vmem_limit_bytes=VMEM_LIMIT_BYTES),
            name="hgrn_prompt",
        )(x, s0, *weights)
        return out, s_new
    assert SUBLANES % seq == 0 and nb % (SUBLANES // seq) == 0
    nseg = SUBLANES // seq
    rows = nb * seq
    x_spec = _const_spec((rows, D_MODEL))
    s_spec = pl.BlockSpec((nseg, HG_HEADS, HG_DK, HG_DV), lambda p: (p, 0, 0, 0))
    out, s_new = pl.pallas_call(
        functools.partial(_hgrn_sample_kernel, seq=seq),
        grid=(nb // nseg,),
        in_specs=[x_spec, s_spec] + w_specs,
        out_specs=(pl.BlockSpec((rows, D_MODEL), lambda p: (0, 0)), s_spec),
        out_shape=(jax.ShapeDtypeStruct((rows, D_MODEL), F32), s_shape),
        scratch_shapes=[
            pltpu.VMEM((nseg, HG_HEADS, HG_DV, HG_DK), F32),
            pltpu.VMEM((rows, 2 * HG_FDIM + 2 * D_MODEL), F32),
            pltpu.VMEM((rows, D_MODEL), F32),
        ],
        compiler_params=pltpu.CompilerParams(
            dimension_semantics=("arbitrary",), vmem_limit_bytes=VMEM_LIMIT_BYTES),
        name="hgrn_sample",
    )(x.reshape(rows, D_MODEL), s0, *weights)
    return out.reshape(x.shape), s_new


def _compare_exchange(v, i, j):
    hi = jnp.maximum(v[i], v[j])
    lo = jnp.minimum(v[i], v[j])
    v[i] = hi
    v[j] = lo


def _bitonic_merge_desc(v):
    n = len(v)
    j = n // 2
    while j >= 1:
        for i in range(n):
            if i & j == 0:
                _compare_exchange(v, i, i + j)
        j //= 2


def _sort_desc(v):
    n = len(v)
    k = 2
    while k <= n:
        j = k // 2
        while j >= 1:
            for i in range(n):
                l = i ^ j
                if l > i:
                    if i & k == 0:
                        _compare_exchange(v, i, l)
                    else:
                        _compare_exchange(v, l, i)
            j //= 2
        k *= 2
    return v


def _merge_top(a, b):
    n = len(a)
    v = [jnp.maximum(a[r], b[n - 1 - r]) for r in range(n)]
    _bitonic_merge_desc(v)
    return v


def _top16_rows(s):
    v = [s[SUBLANES * j:SUBLANES * (j + 1), :] for j in range(PEER_NKEYS // SUBLANES)]
    v = _sort_desc(v)
    for shift in (1, 2, 4):
        v = _merge_top(v, [pltpu.roll(x, shift, axis=0) for x in v])
    return v


def _candidate_top(a, b):
    top = [a[0] + b[j] for j in range(PEER_TOPK)]
    for i in range(1, PEER_TOPK):
        top = _merge_top(top, [a[i] + b[j] for j in range(PEER_TOPK)])
    return top


def _peer_select_kernel(x_ref, nw_ref, wq_ref, k1_ref, k2_ref,
                        hb_ref, s1_ref, s2_ref, thr_ref, t1_ref, t2_ref):
    hb = _rmsnorm(x_ref[...], nw_ref[...]).astype(BF16)
    hb_ref[...] = hb
    qb = _dot(hb, wq_ref[...]).astype(BF16)
    k1 = k1_ref[...]
    k2 = k2_ref[...]
    log2e = 1.0 / math.log(2.0)
    for hd in range(PEER_HEADS):
        base = hd * PEER_DKEY
        s1 = _dot_nt(k1, qb[:, base:base + PEER_HALF]) * log2e
        s2 = _dot_nt(k2, qb[:, base + PEER_HALF:base + PEER_DKEY]) * log2e
        s1_ref[hd] = s1
        s2_ref[hd] = s2
        for s, t_ref in ((s1, t1_ref), (s2, t2_ref)):
            top = _top16_rows(s)
            for r in range(PEER_TOPK):
                t_ref[r, hd:hd + 1, :] = top[r][0:1, :]
    a = [t1_ref[r] for r in range(PEER_TOPK)]
    b = [t2_ref[r] for r in range(PEER_TOPK)]
    top = _candidate_top(a, b)
    z = jnp.exp2(top[0] - top[0])
    for r in range(1, PEER_TOPK):
        z = z + jnp.exp2(top[r] - top[0])
    shift = top[0] + jnp.log2(z) + 1.0
    for hd in range(PEER_HEADS):
        s1_ref[hd] = s1_ref[hd] - shift[hd:hd + 1, :]
    thr_ref[...] = _candidate_top([v - shift for v in a], b)[PEER_TOPK - 1]


def _peer_expert_kernel(hb_ref, x_ref, s1_ref, s2_ref, thr_ref, u_ref, v_ref, fw_ref,
                        out_ref, acc_ref, *tile_refs, tb, eb, tw, final_norm):
    e = pl.program_id(1)
    ntile = tb // tw
    ht_refs, at_refs = tile_refs[:ntile], tile_refs[ntile:]

    @pl.when(e == 0)
    def _():
        acc_ref[...] = jnp.zeros_like(acc_ref)

    u = u_ref[...]
    v = v_ref[...]
    for i in range(ntile):
        ht_refs[i][...] = _dot_nt(u, hb_ref[i * tw:(i + 1) * tw, :])
    for i in range(ntile):
        for lt in range(tw // LANES):
            lcol = slice(lt * LANES, (lt + 1) * LANES)
            col = slice(i * tw + lt * LANES, i * tw + (lt + 1) * LANES)
            for r in range(eb // PEER_NKEYS):
                gate = None
                for hd in range(PEER_HEADS):
                    t = s2_ref[hd, :, col] + s1_ref[hd, r:r + 1, col]
                    g = jnp.where(t >= thr_ref[hd:hd + 1, col], jnp.exp2(t), 0.0)
                    gate = g if gate is None else gate + g
                rr = slice(r * PEER_NKEYS, (r + 1) * PEER_NKEYS)
                h = ht_refs[i][rr, lcol]
                at_refs[i][rr, lcol] = (h * (1.0 + lax.erf(h * (1.0 / math.sqrt(2.0)))) * gate).astype(BF16)
        acc_ref[:, i * tw:(i + 1) * tw] += lax.dot_general(
            v, at_refs[i][...], (((0,), (0,)), ((), ())), preferred_element_type=F32)

    @pl.when(e == pl.num_programs(1) - 1)
    def _():
        y = x_ref[...] + acc_ref[...].T
        if final_norm:
            y = _rmsnorm(y, fw_ref[...])
        out_ref[...] = y


def _peer_layer(x, norm_w, w_q, keys1, keys2, u_tab, v_tab, final_w, *, final_norm):
    t_tot = x.shape[0]
    tbs = min(512, t_tot)
    tb = min(1024, t_tot)
    eb = 8 * PEER_NKEYS
    hb, s1, s2, thr = pl.pallas_call(
        _peer_select_kernel,
        grid=(t_tot // tbs,),
        in_specs=[
            pl.BlockSpec((tbs, D_MODEL), lambda i: (i, 0)),
            _const_spec((1, D_MODEL)),
            _const_spec((D_MODEL, PEER_HEADS * PEER_DKEY)),
            _const_spec((PEER_NKEYS, PEER_HALF)),
            _const_spec((PEER_NKEYS, PEER_HALF)),
        ],
        out_specs=(
            pl.BlockSpec((tbs, D_MODEL), lambda i: (i, 0)),
            pl.BlockSpec((PEER_HEADS, PEER_NKEYS, tbs), lambda i: (0, 0, i)),
            pl.BlockSpec((PEER_HEADS, PEER_NKEYS, tbs), lambda i: (0, 0, i)),
            pl.BlockSpec((PEER_HEADS, tbs), lambda i: (0, i)),
        ),
        out_shape=(
            jax.ShapeDtypeStruct((t_tot, D_MODEL), BF16),
            jax.ShapeDtypeStruct((PEER_HEADS, PEER_NKEYS, t_tot), F32),
            jax.ShapeDtypeStruct((PEER_HEADS, PEER_NKEYS, t_tot), F32),
            jax.ShapeDtypeStruct((PEER_HEADS, t_tot), F32),
        ),
        scratch_shapes=[
            pltpu.VMEM((PEER_TOPK, PEER_HEADS, tbs), F32),
            pltpu.VMEM((PEER_TOPK, PEER_HEADS, tbs), F32),
        ],
        compiler_params=pltpu.CompilerParams(
            dimension_semantics=("arbitrary",), vmem_limit_bytes=VMEM_LIMIT_BYTES),
        name="peer_select",
    )(x, norm_w.reshape(1, D_MODEL), w_q.astype(BF16), keys1.astype(BF16), keys2.astype(BF16))

    tok = lambda i, e: (i, 0)
    once = pl.Buffered(1)
    tw = 2 * LANES
    out = pl.pallas_call(
        functools.partial(_peer_expert_kernel, tb=tb, eb=eb, tw=tw, final_norm=final_norm),
        grid=(t_tot // tb, PEER_EXPERTS // eb),
        in_specs=[
            pl.BlockSpec((tb, D_MODEL), tok, pipeline_mode=once),
            pl.BlockSpec((tb, D_MODEL), tok, pipeline_mode=once),
            pl.BlockSpec((PEER_HEADS, eb // PEER_NKEYS, tb), lambda i, e: (0, e, i)),
            pl.BlockSpec((PEER_HEADS, PEER_NKEYS, tb), lambda i, e: (0, 0, i), pipeline_mode=once),
            pl.BlockSpec((PEER_HEADS, tb), lambda i, e: (0, i), pipeline_mode=once),
            pl.BlockSpec((eb, D_MODEL), lambda i, e: (e, 0)),
            pl.BlockSpec((eb, D_MODEL), lambda i, e: (e, 0)),
            _const_spec((1, D_MODEL)),
        ],
        out_specs=pl.BlockSpec((tb, D_MODEL), tok),
        out_shape=jax.ShapeDtypeStruct((t_tot, D_MODEL), F32),
        scratch_shapes=(
            [pltpu.VMEM((D_MODEL, tb), F32)]
            + [pltpu.VMEM((eb, tw), F32) for _ in range(tb // tw)]
            + [pltpu.VMEM((eb, tw), BF16) for _ in range(tb // tw)]
        ),
        compiler_params=pltpu.CompilerParams(
            dimension_semantics=("arbitrary", "arbitrary"), vmem_limit_bytes=VMEM_LIMIT_BYTES),
        name="peer_experts",
    )(hb, x, s1, s2, thr, u_tab.astype(BF16), v_tab.astype(BF16), final_w.reshape(1, D_MODEL))
    return out


def _trunk(x, s5_re, s5_im, s_hg, p, s5_prm, lb_all):
    nb, seq, _ = x.shape
    new_re, new_im, new_hg = [], [], []
    for i in range(DEPTH):
        j = i // N_MIXERS
        if i % N_MIXERS == 0:
            a, bbd, ccd = s5_prm[j]
            x, sr, si = _s5_layer(x, s5_re[j], s5_im[j], p["norm_mix"][i], a, bbd, ccd,
                                  p["s5_d"][j], p["s5_w_glu"][j].astype(BF16), lc=min(seq, 64))
            new_re.append(sr)
            new_im.append(si)
        else:
            x, s_new = _hgrn_layer(x, s_hg[j], p["norm_mix"][i], p["hg_w_in"][j], lb_all[i],
                                   p["hg_norm_w"][j], p["hg_w_out"][j])
            new_hg.append(s_new)
        x = _peer_layer(x.reshape(nb * seq, D_MODEL), p["norm_ffn"][i], p["peer_w_q"][i],
                        p["peer_keys1"][i], p["peer_keys2"][i], p["peer_u"][i], p["peer_v"][i],
                        p["norm_final"], final_norm=(i == DEPTH - 1)).reshape(nb, seq, D_MODEL)
    return x, jnp.stack(new_re), jnp.stack(new_im), jnp.stack(new_hg)


def kernel(x_prompt, x_sample, state_s5_re, state_s5_im, state_hgrn, norm_mix, norm_ffn, norm_final,
           s5_lambda_re, s5_lambda_im, s5_log_dt, s5_b_re, s5_b_im, s5_c_re, s5_c_im, s5_d, s5_w_glu,
           hg_w_in, hg_lower_bounds, hg_norm_w, hg_w_out, peer_w_q, peer_keys1, peer_keys2, peer_u, peer_v):
    p = dict(norm_mix=norm_mix, norm_ffn=norm_ffn, norm_final=norm_final, s5_d=s5_d, s5_w_glu=s5_w_glu,
             hg_w_in=hg_w_in, hg_norm_w=hg_norm_w, hg_w_out=hg_w_out, peer_w_q=peer_w_q,
             peer_keys1=peer_keys1, peer_keys2=peer_keys2, peer_u=peer_u, peer_v=peer_v)
    n_a = s5_lambda_re.shape[0]
    s5_prm = [_s5_params(s5_lambda_re[j], s5_lambda_im[j], s5_log_dt[j], s5_b_re[j], s5_b_im[j],
                         s5_c_re[j], s5_c_im[j]) for j in range(n_a)]
    lb_all = jnp.cumsum(jax.nn.softmax(hg_lower_bounds.astype(F32), axis=0), axis=0)
    lb_all = lb_all - lb_all[0:1]
    nbp = x_prompt.shape[0]
    z_s5 = jnp.zeros((n_a, nbp, S5_GROUPS, S5_STATE), state_s5_re.dtype)
    z_hg = jnp.zeros((state_hgrn.shape[0], nbp, HG_HEADS, HG_DK, HG_DV), state_hgrn.dtype)
    y_p, re_p, im_p, hg_p = _trunk(x_prompt, z_s5, z_s5, z_hg, p, s5_prm, lb_all)
    y_s, re_s, im_s, hg_s = _trunk(x_sample, state_s5_re, state_s5_im, state_hgrn, p, s5_prm, lb_all)
    return (y_p, y_s, re_p, im_p, hg_p, re_s, im_s, hg_s)
```

```python
import functools
import math

import jax
import jax.numpy as jnp
from jax import lax
from jax.experimental import pallas as pl
from jax.experimental.pallas import tpu as pltpu

F32 = jnp.float32
BF16 = jnp.bfloat16

D_MODEL = 1024
DEPTH = 2
N_MIXERS = 2
S5_GROUP = 16
S5_GROUPS = D_MODEL // S5_GROUP
S5_STATE = 64
HG_HEADS = 8
HG_DK = 128
HG_DV = 128
HG_FDIM = HG_HEADS * HG_DK
HG_CHUNK = 64
PEER_HEADS = 8
PEER_NKEYS = 128
PEER_EXPERTS = PEER_NKEYS * PEER_NKEYS
PEER_TOPK = 16
PEER_DKEY = 256
PEER_HALF = PEER_DKEY // 2
RMS_EPS = 1e-6

LANES = 128
SUBLANES = 8
VMEM_LIMIT_BYTES = 56 * 2**20

S5_CB = 4
S5_CB_GROUPS = S5_GROUPS // S5_CB
S5_CB_IN = S5_CB_GROUPS * S5_GROUP
S5_CB_STATE = S5_CB_GROUPS * S5_STATE


def _const_spec(shape):
    nd = len(shape)
    return pl.BlockSpec(shape, lambda *_: (0,) * nd, pipeline_mode=pl.Buffered(1))


def _rmsnorm(x, w):
    ms = jnp.mean(x * x, axis=-1, keepdims=True)
    return x * lax.rsqrt(ms + RMS_EPS) * w


def _gelu(x):
    return 0.5 * x * (1.0 + lax.erf(x * (1.0 / math.sqrt(2.0))))


def _sigmoid(x):
    return 1.0 / (1.0 + jnp.exp(-x))


def _dot(a, b):
    return jnp.dot(a, b, preferred_element_type=F32)


def _dot_nt(a, b):
    return lax.dot_general(a, b, (((1,), (1,)), ((), ())), preferred_element_type=F32)


def _s5_discretize_kernel(lre_ref, lim_ref, ldt_ref, are_ref, aim_ref, zre_ref, zim_ref):
    lr = jnp.minimum(lre_ref[...], -1e-4)
    li = lim_ref[...]
    dt = jnp.exp(ldt_ref[...])
    mag = jnp.exp(lr * dt)
    ang = li * dt
    ab_re = mag * jnp.cos(ang)
    ab_im = mag * jnp.sin(ang)
    den = lr * lr + li * li
    nr = ab_re - 1.0
    are_ref[...] = ab_re
    aim_ref[...] = ab_im
    zre_ref[...] = (nr * lr + ab_im * li) / den
    zim_ref[...] = (ab_im * lr - nr * li) / den


def _s5_scale_b_kernel(zre_ref, zim_ref, bre_ref, bim_ref, ore_ref, oim_ref):
    zr = zre_ref[...]
    zi = zim_ref[...]
    br = bre_ref[...]
    bi = bim_ref[...]
    ore_ref[...] = zr * br - zi * bi
    oim_ref[...] = zr * bi + zi * br


def _s5_params(lam_re, lam_im, log_dt, b_re, b_im, c_re, c_im):
    g, p, hh = S5_GROUPS, S5_STATE, S5_GROUP
    gp = jax.ShapeDtypeStruct((g, p), F32)
    a_re, a_im, z_re, z_im = pl.pallas_call(
        _s5_discretize_kernel, out_shape=(gp, gp, gp, gp), name="s5_discretize",
    )(lam_re.astype(F32), lam_im.astype(F32), log_dt.astype(F32).reshape(g, 1))
    gph = jax.ShapeDtypeStruct((g * p, hh), F32)
    bb_re, bb_im = pl.pallas_call(
        _s5_scale_b_kernel, out_shape=(gph, gph), name="s5_scale_b",
    )(z_re.reshape(g * p, 1), z_im.reshape(g * p, 1),
      b_re.astype(F32).reshape(g * p, hh), b_im.astype(F32).reshape(g * p, hh))
    eye = jnp.eye(S5_CB_GROUPS, dtype=F32)

    def blockdiag_in(w):
        w4 = w.reshape(S5_CB, S5_CB_GROUPS, p, hh)
        return jnp.einsum('cgph,gk->cghkp', w4, eye).reshape(S5_CB, S5_CB_IN, S5_CB_STATE)

    def blockdiag_out(w):
        w4 = w.astype(F32).reshape(S5_CB, S5_CB_GROUPS, hh, p)
        return jnp.einsum('cghp,gk->cgpkh', w4, eye).reshape(S5_CB, S5_CB_STATE, S5_CB_IN)

    bbd = jnp.concatenate([blockdiag_in(bb_re), blockdiag_in(bb_im)], axis=-1).astype(BF16)
    ccd = jnp.concatenate([blockdiag_out(c_re), -blockdiag_out(c_im)], axis=1).astype(BF16)
    a = jnp.stack([a_re.reshape(S5_CB, S5_CB_STATE), a_im.reshape(S5_CB, S5_CB_STATE)], axis=1)
    return a, bbd, ccd


def _s5_pack_state(s_re, s_im):
    nb = s_re.shape[0]
    return jnp.stack([s_re.reshape(nb, S5_CB, S5_CB_STATE), s_im.reshape(nb, S5_CB, S5_CB_STATE)],
                     axis=2).reshape(nb, 2 * S5_GROUPS * S5_STATE)


def _s5_unpack_state(s):
    nb = s.shape[0]
    s4 = s.reshape(nb, S5_CB, 2, S5_CB_STATE)
    return (s4[:, :, 0].reshape(nb, S5_GROUPS, S5_STATE), s4[:, :, 1].reshape(nb, S5_GROUPS, S5_STATE))


def _s5_kernel(x_ref, s0_ref, nw_ref, a_ref, bbd_ref, ccd_ref, d_ref, wglu_ref,
               out_ref, sfin_ref, bu_ref, st_ref, hs_ref, ys_ref, *, nb, lc):
    rows = nb * lc
    nk = D_MODEL // LANES
    lane = lambda k: slice(k * LANES, (k + 1) * LANES)

    @pl.when(pl.program_id(0) == 0)
    def _():
        st_ref[...] = s0_ref[...]

    x = x_ref[...].reshape(rows, D_MODEL)
    h = _rmsnorm(x, nw_ref[...])
    if nb <= lc:
        for b in range(nb):
            for k in range(nk):
                hs_ref[k, pl.ds(b, lc, stride=nb), :] = h[b * lc:(b + 1) * lc, lane(k)]
        hb = jnp.concatenate([hs_ref[k] for k in range(nk)], axis=-1).astype(BF16)
    else:
        for k in range(nk):
            hs_ref[k] = h[:, lane(k)]
        hb = jnp.concatenate(
            [jnp.concatenate([hs_ref[k, pl.ds(t, nb, stride=lc), :] for t in range(lc)], axis=0)
             for k in range(nk)], axis=-1).astype(BF16)
    nt = S5_CB_STATE // LANES
    for cb in range(S5_CB):
        bu = _dot(hb[:, cb * S5_CB_IN:(cb + 1) * S5_CB_IN], bbd_ref[cb])
        for k in range(2 * nt):
            bu_ref[k] = bu[:, lane(k)]
        a_re = [jnp.broadcast_to(a_ref[cb, 0:1, lane(k)], (SUBLANES, LANES)) for k in range(nt)]
        a_im = [jnp.broadcast_to(a_ref[cb, 1:2, lane(k)], (SUBLANES, LANES)) for k in range(nt)]
        col = cb * 2 * S5_CB_STATE
        for j in range(nb // SUBLANES):
            brow = j * SUBLANES

            def step(t, carry, brow=brow):
                idx = pl.ds(pl.multiple_of(t * nb + brow, SUBLANES), SUBLANES)
                new_re, new_im = [], []
                for k in range(nt):
                    s_re, s_im = carry[k], carry[nt + k]
                    n_re = a_re[k] * s_re - a_im[k] * s_im + bu_ref[k, idx, :]
                    n_im = a_re[k] * s_im + a_im[k] * s_re + bu_ref[nt + k, idx, :]
                    bu_ref[k, idx, :] = n_re
                    bu_ref[nt + k, idx, :] = n_im
                    new_re.append(n_re)
                    new_im.append(n_im)
                return tuple(new_re) + tuple(new_im)

            carry = tuple(st_ref[brow:brow + SUBLANES, col + k * LANES:col + (k + 1) * LANES]
                          for k in range(2 * nt))
            if lc <= 8:
                for t in range(lc):
                    carry = step(t, carry)
            else:
                carry = lax.fori_loop(0, lc, step, carry, unroll=4)
            for k in range(2 * nt):
                st_ref[brow:brow + SUBLANES, col + k * LANES:col + (k + 1) * LANES] = carry[k]
        st_all = jnp.concatenate([bu_ref[k] for k in range(2 * nt)], axis=-1)
        y_cb = _dot(st_all.astype(BF16), ccd_ref[cb])
        for k in range(S5_CB_IN // LANES):
            ys_ref[cb * (S5_CB_IN // LANES) + k] = y_cb[:, lane(k)]
    if nb <= lc:
        y = jnp.concatenate(
            [jnp.concatenate([ys_ref[k, pl.ds(b, lc, stride=nb), :] for b in range(nb)], axis=0)
             for k in range(nk)], axis=-1)
    else:
        for t in range(lc):
            for k in range(nk):
                hs_ref[k, pl.ds(t, nb, stride=lc), :] = ys_ref[k, t * nb:(t + 1) * nb, :]
        y = jnp.concatenate([hs_ref[k] for k in range(nk)], axis=-1)
    y = _gelu(y + d_ref[...] * h)
    z = _dot(y.astype(BF16), wglu_ref[...])
    out = x + z[:, :D_MODEL] * _sigmoid(z[:, D_MODEL:])
    out_ref[...] = out.reshape(out_ref.shape)
    sfin_ref[...] = st_ref[...]


def _s5_layer(x, s0_re, s0_im, norm_w, a, bbd, ccd, d, w_glu, *, lc):
    nb, seq, _ = x.shape
    nstate = 2 * S5_GROUPS * S5_STATE
    rows = nb * lc
    if seq == lc:
        xin = x.reshape(rows, D_MODEL)
        x_spec = pl.BlockSpec((rows, D_MODEL), lambda c: (0, 0))
    else:
        xin = x
        x_spec = pl.BlockSpec((nb, lc, D_MODEL), lambda c: (0, c, 0))
    out, sfin = pl.pallas_call(
        functools.partial(_s5_kernel, nb=nb, lc=lc),
        grid=(seq // lc,),
        in_specs=[
            x_spec,
            _const_spec((nb, nstate)),
            _const_spec((1, D_MODEL)),
            _const_spec((S5_CB, 2, S5_CB_STATE)),
            _const_spec((S5_CB, S5_CB_IN, 2 * S5_CB_STATE)),
            _const_spec((S5_CB, 2 * S5_CB_STATE, S5_CB_IN)),
            _const_spec((1, D_MODEL)),
            _const_spec((D_MODEL, 2 * D_MODEL)),
        ],
        out_specs=(x_spec, pl.BlockSpec((nb, nstate), lambda c: (0, 0))),
        out_shape=(jax.ShapeDtypeStruct(xin.shape, F32), jax.ShapeDtypeStruct((nb, nstate), F32)),
        scratch_shapes=[
            pltpu.VMEM((2 * S5_CB_STATE // LANES, rows, LANES), F32),
            pltpu.VMEM((nb, nstate), F32),
            pltpu.VMEM((D_MODEL // LANES, rows, LANES), F32),
            pltpu.VMEM((D_MODEL // LANES, rows, LANES), F32),
        ],
        compiler_params=pltpu.CompilerParams(
            dimension_semantics=("arbitrary",), vmem_limit_bytes=VMEM_LIMIT_BYTES),
        name="s5_layer",
    )(xin, _s5_pack_state(s0_re, s0_im), norm_w.reshape(1, D_MODEL), a, bbd, ccd,
      d.reshape(1, D_MODEL), w_glu)
    new_re, new_im = _s5_unpack_state(sfin)
    return out.reshape(x.shape), new_re, new_im


def _hgrn_gates(proj, lb):
    q = proj[:, 0:HG_FDIM]
    q = q * _sigmoid(q)
    fz = proj[:, HG_FDIM:2 * HG_FDIM]
    v = proj[:, 2 * HG_FDIM:2 * HG_FDIM + D_MODEL]
    g = proj[:, 2 * HG_FDIM + D_MODEL:]
    log_sig = jnp.minimum(fz, 0.0) - jnp.log1p(jnp.exp(-jnp.abs(fz)))
    t1 = jnp.log(lb)
    t2 = jnp.log1p(-lb) + log_sig
    hi = jnp.maximum(t1, t2)
    lo = jnp.minimum(t1, t2)
    logf = hi + jnp.log1p(jnp.exp(lo - hi))
    k = (1.0 - lb) * _sigmoid(-fz)
    return q, k, v, g, logf


def _hgrn_tile(q, k, v, g, logf, st_refs, nw, *, chunk):
    rows = q.shape[0]
    nseg = rows // chunk
    ri = lax.broadcasted_iota(jnp.int32, (rows, rows), 0)
    ci = lax.broadcasted_iota(jnp.int32, (rows, rows), 1)
    if nseg == 1:
        same = ci <= ri
    else:
        same = (ci <= ri) & ((ri // chunk) == (ci // chunk))
    tri = jnp.where(same, 1.0, 0.0).astype(F32)
    b = jnp.dot(tri, logf, precision=lax.Precision.HIGHEST, preferred_element_type=F32)
    rowid = lax.broadcasted_iota(jnp.int32, (rows, 1), 0)
    b_mid = b[chunk // 2:chunk // 2 + 1, :]
    b_last = b[chunk - 1:chunk, :]
    for s in range(1, nseg):
        in_s = rowid >= s * chunk
        b_mid = jnp.where(in_s, b[s * chunk + chunk // 2:s * chunk + chunk // 2 + 1, :], b_mid)
        b_last = jnp.where(in_s, b[s * chunk + chunk - 1:s * chunk + chunk, :], b_last)
    qs = (q * jnp.exp(b - b_mid)).astype(BF16)
    ks = (k * jnp.exp(b_mid - b)).astype(BF16)
    qi = (q * jnp.exp(b)).astype(BF16)
    kd = k * jnp.exp(b_last - b)
    vb = v.astype(BF16)
    outs = []
    for hd in range(HG_HEADS):
        sl = slice(hd * HG_DK, (hd + 1) * HG_DK)
        scores = jnp.where(same, _dot_nt(qs[:, sl], ks[:, sl]), 0.0)
        o = _dot(scores.astype(BF16), vb[:, sl])
        v_t = v[:, sl].T
        for s in range(nseg):
            st = st_refs[s][hd]
            o_s = _dot_nt(qi[:, sl], st.astype(BF16))
            kd_s = kd[:, sl]
            if nseg > 1:
                in_s = (rowid >= s * chunk) & (rowid < (s + 1) * chunk)
                o_s = jnp.where(in_s, o_s, 0.0)
                kd_s = jnp.where(in_s, kd_s, 0.0)
            o = o + o_s
            dec = jnp.exp(b[s * chunk + chunk - 1:s * chunk + chunk, sl])
            st_refs[s][hd] = dec * st + _dot(v_t.astype(BF16), kd_s.astype(BF16))
        o = o * lax.rsqrt(jnp.mean(o * o, axis=-1, keepdims=True) + RMS_EPS)
        gh = g[:, sl]
        outs.append(o * nw * (gh * _sigmoid(gh)))
    return jnp.concatenate(outs, axis=-1)


def _hgrn_prompt_kernel(x_ref, s0_ref, nw_ref, win_ref, lb_ref, hnw_ref, wout_ref,
                        out_ref, sout_ref, st_ref, y_ref, *, lc, chunk):
    c = pl.program_id(1)

    @pl.when(c == 0)
    def _():
        for hd in range(HG_HEADS):
            st_ref[hd] = s0_ref[0, hd].T

    x = x_ref[0]
    hb = _rmsnorm(x, nw_ref[...]).astype(BF16)
    proj = _dot(hb, win_ref[...])
    q, k, v, g, logf = _hgrn_gates(proj, lb_ref[...])
    for sub in range(lc // chunk):
        r = slice(sub * chunk, (sub + 1) * chunk)
        y_ref[r, :] = _hgrn_tile(q[r], k[r], v[r], g[r], logf[r], [st_ref], hnw_ref[...], chunk=chunk)
    out_ref[0] = x + _dot(y_ref[...].astype(BF16), wout_ref[...])

    @pl.when(c == pl.num_programs(1) - 1)
    def _():
        for hd in range(HG_HEADS):
            sout_ref[0, hd] = st_ref[hd].T


def _hgrn_sample_kernel(x_ref, s0_ref, nw_ref, win_ref, lb_ref, hnw_ref, wout_ref,
                        out_ref, sout_ref, st_ref, proj_ref, y_ref, *, seq):
    p = pl.program_id(0)
    nseg = SUBLANES // seq

    @pl.when(p == 0)
    def _():
        hb = _rmsnorm(x_ref[...], nw_ref[...]).astype(BF16)
        proj_ref[...] = _dot(hb, win_ref[...])

    for s in range(nseg):
        for hd in range(HG_HEADS):
            st_ref[s, hd] = s0_ref[s, hd].T
    r = pl.ds(pl.multiple_of(p * SUBLANES, SUBLANES), SUBLANES)
    q, k, v, g, logf = _hgrn_gates(proj_ref[r, :], lb_ref[...])
    y_ref[r, :] = _hgrn_tile(q, k, v, g, logf, [st_ref.at[s] for s in range(nseg)], hnw_ref[...],
                             chunk=seq)
    for s in range(nseg):
        for hd in range(HG_HEADS):
            sout_ref[s, hd] = st_ref[s, hd].T

    @pl.when(p == pl.num_programs(0) - 1)
    def _():
        out_ref[...] = x_ref[...] + _dot(y_ref[...].astype(BF16), wout_ref[...])


def _hgrn_layer(x, s0, norm_w, w_in, lb, hg_norm_w, w_out):
    nb, seq, _ = x.shape
    weights = (norm_w.reshape(1, D_MODEL), w_in.astype(BF16), lb.reshape(1, HG_FDIM),
               hg_norm_w.reshape(1, HG_DV), w_out.astype(BF16))
    w_specs = [
        _const_spec((1, D_MODEL)),
        _const_spec((D_MODEL, 2 * HG_FDIM + 2 * D_MODEL)),
        _const_spec((1, HG_FDIM)),
        _const_spec((1, HG_DV)),
        _const_spec((D_MODEL, D_MODEL)),
    ]
    s_shape = jax.ShapeDtypeStruct(s0.shape, F32)
    if seq >= HG_CHUNK:
        chunk = HG_CHUNK
        lc = 4 * chunk
        x_spec = pl.BlockSpec((1, lc, D_MODEL), lambda b, c: (b, c, 0))
        s_spec = pl.BlockSpec((1, HG_HEADS, HG_DK, HG_DV), lambda b, c: (b, 0, 0, 0))
        out, s_new = pl.pallas_call(
            functools.partial(_hgrn_prompt_kernel, lc=lc, chunk=chunk),
            grid=(nb, seq // lc),
            in_specs=[x_spec, s_spec] + w_specs,
            out_specs=(x_spec, s_spec),
            out_shape=(jax.ShapeDtypeStruct(x.shape, F32), s_shape),
            scratch_shapes=[
                pltpu.VMEM((HG_HEADS, HG_DV, HG_DK), F32),
                pltpu.VMEM((lc, D_MODEL), F32),
            ],
            compiler_params=pltpu.CompilerParams(
                dimension_semantics=("arbitrary", "arbitrary"), vmem_limit_bytes=VMEM_LIMIT_BYTES),
            name="hgrn_prompt",
        )(x, s0, *weights)
        return out, s_new
    assert SUBLANES % seq == 0 and nb % (SUBLANES // seq) == 0
    nseg = SUBLANES // seq
    rows = nb * seq
    x_spec = _const_spec((rows, D_MODEL))
    s_spec = pl.BlockSpec((nseg, HG_HEADS, HG_DK, HG_DV), lambda p: (p, 0, 0, 0))
    out, s_new = pl.pallas_call(
        functools.partial(_hgrn_sample_kernel, seq=seq),
        grid=(nb // nseg,),
        in_specs=[x_spec, s_spec] + w_specs,
        out_specs=(pl.BlockSpec((rows, D_MODEL), lambda p: (0, 0)), s_spec),
        out_shape=(jax.ShapeDtypeStruct((rows, D_MODEL), F32), s_shape),
        scratch_shapes=[
            pltpu.VMEM((nseg, HG_HEADS, HG_DV, HG_DK), F32),
            pltpu.VMEM((rows, 2 * HG_FDIM + 2 * D_MODEL), F32),
            pltpu.VMEM((rows, D_MODEL), F32),
        ],
        compiler_params=pltpu.CompilerParams(
            dimension_semantics=("arbitrary",), vmem_limit_bytes=VMEM_LIMIT_BYTES),
        name="hgrn_sample",
    )(x.reshape(rows, D_MODEL), s0, *weights)
    return out.reshape(x.shape), s_new


def _compare_exchange(v, i, j):
    hi = jnp.maximum(v[i], v[j])
    lo = jnp.minimum(v[i], v[j])
    v[i] = hi
    v[j] = lo


def _bitonic_merge_desc(v):
    n = len(v)
    j = n // 2
    while j >= 1:
        for i in range(n):
            if i & j == 0:
                _compare_exchange(v, i, i + j)
        j //= 2


def _sort_desc(v):
    n = len(v)
    k = 2
    while k <= n:
        j = k // 2
        while j >= 1:
            for i in range(n):
                l = i ^ j
                if l > i:
                    if i & k == 0:
                        _compare_exchange(v, i, l)
                    else:
                        _compare_exchange(v, l, i)
            j //= 2
        k *= 2
    return v


def _merge_top(a, b):
    n = len(a)
    v = [jnp.maximum(a[r], b[n - 1 - r]) for r in range(n)]
    _bitonic_merge_desc(v)
    return v


def _top16_rows(s):
    v = [s[SUBLANES * j:SUBLANES * (j + 1), :] for j in range(PEER_NKEYS // SUBLANES)]
    v = _sort_desc(v)
    for shift in (1, 2, 4):
        v = _merge_top(v, [pltpu.roll(x, shift, axis=0) for x in v])
    return v


def _candidate_top(a, b):
    top = [a[0] + b[j] for j in range(PEER_TOPK)]
    for i in range(1, PEER_TOPK):
        top = _merge_top(top, [a[i] + b[j] for j in range(PEER_TOPK)])
    return top


def _peer_select_kernel(x_ref, nw_ref, wq_ref, k1_ref, k2_ref,
                        hb_ref, s1_ref, s2_ref, thr_ref, t1_ref, t2_ref):
    hb = _rmsnorm(x_ref[...], nw_ref[...]).astype(BF16)
    hb_ref[...] = hb
    qb = _dot(hb, wq_ref[...]).astype(BF16)
    k1 = k1_ref[...]
    k2 = k2_ref[...]
    log2e = 1.0 / math.log(2.0)
    for hd in range(PEER_HEADS):
        base = hd * PEER_DKEY
        s1 = _dot_nt(k1, qb[:, base:base + PEER_HALF]) * log2e
        s2 = _dot_nt(k2, qb[:, base + PEER_HALF:base + PEER_DKEY]) * log2e
        s1_ref[hd] = s1
        s2_ref[hd] = s2
        for s, t_ref in ((s1, t1_ref), (s2, t2_ref)):
            top = _top16_rows(s)
            for r in range(PEER_TOPK):
                t_ref[r, hd:hd + 1, :] = top[r][0:1, :]
    a = [t1_ref[r] for r in range(PEER_TOPK)]
    b = [t2_ref[r] for r in range(PEER_TOPK)]
    top = _candidate_top(a, b)
    z = jnp.exp2(top[0] - top[0])
    for r in range(1, PEER_TOPK):
        z = z + jnp.exp2(top[r] - top[0])
    shift = top[0] + jnp.log2(z) + 1.0
    for hd in range(PEER_HEADS):
        s1_ref[hd] = s1_ref[hd] - shift[hd:hd + 1, :]
    thr_ref[...] = _candidate_top([v - shift for v in a], b)[PEER_TOPK - 1]


def _peer_expert_kernel(hb_ref, x_ref, s1_ref, s2_ref, thr_ref, u_ref, v_ref, fw_ref,
                        out_ref, acc_ref, *tile_refs, tb, eb, tw, final_norm):
    e = pl.program_id(1)
    ntile = tb // tw
    ht_refs, at_refs = tile_refs[:ntile], tile_refs[ntile:]

    @pl.when(e == 0)
    def _():
        acc_ref[...] = jnp.zeros_like(acc_ref)

    u = u_ref[...]
    v = v_ref[...]
    for i in range(ntile):
        ht_refs[i][...] = _dot_nt(u, hb_ref[i * tw:(i + 1) * tw, :])
    for i in range(ntile):
        for lt in range(tw // LANES):
            lcol = slice(lt * LANES, (lt + 1) * LANES)
            col = slice(i * tw + lt * LANES, i * tw + (lt + 1) * LANES)
            for r in range(eb // PEER_NKEYS):
                gate = None
                for hd in range(PEER_HEADS):
                    t = s2_ref[hd, :, col] + s1_ref[hd, r:r + 1, col]
                    g = jnp.where(t >= thr_ref[hd:hd + 1, col], jnp.exp2(t), 0.0)
                    gate = g if gate is None else gate + g
                rr = slice(r * PEER_NKEYS, (r + 1) * PEER_NKEYS)
                h = ht_refs[i][rr, lcol]
                at_refs[i][rr, lcol] = (h * (1.0 + lax.erf(h * (1.0 / math.sqrt(2.0)))) * gate).astype(BF16)
        acc_ref[:, i * tw:(i + 1) * tw] += _dot(v, at_refs[i][...])

    @pl.when(e == pl.num_programs(1) - 1)
    def _():
        y = x_ref[...] + acc_ref[...].T
        if final_norm:
            y = _rmsnorm(y, fw_ref[...])
        out_ref[...] = y


def _peer_layer(x, norm_w, w_q, keys1, keys2, u_tab, vt_tab, final_w, *, final_norm):
    t_tot = x.shape[0]
    tbs = min(512, t_tot)
    tb = min(1024, t_tot)
    eb = 8 * PEER_NKEYS
    hb, s1, s2, thr = pl.pallas_call(
        _peer_select_kernel,
        grid=(t_tot // tbs,),
        in_specs=[
            pl.BlockSpec((tbs, D_MODEL), lambda i: (i, 0)),
            _const_spec((1, D_MODEL)),
            _const_spec((D_MODEL, PEER_HEADS * PEER_DKEY)),
            _const_spec((PEER_NKEYS, PEER_HALF)),
            _const_spec((PEER_NKEYS, PEER_HALF)),
        ],
        out_specs=(
            pl.BlockSpec((tbs, D_MODEL), lambda i: (i, 0)),
            pl.BlockSpec((PEER_HEADS, PEER_NKEYS, tbs), lambda i: (0, 0, i)),
            pl.BlockSpec((PEER_HEADS, PEER_NKEYS, tbs), lambda i: (0, 0, i)),
            pl.BlockSpec((PEER_HEADS, tbs), lambda i: (0, i)),
        ),
        out_shape=(
            jax.ShapeDtypeStruct((t_tot, D_MODEL), BF16),
            jax.ShapeDtypeStruct((PEER_HEADS, PEER_NKEYS, t_tot), F32),
            jax.ShapeDtypeStruct((PEER_HEADS, PEER_NKEYS, t_tot), F32),
            jax.ShapeDtypeStruct((PEER_HEADS, t_tot), F32),
        ),
        scratch_shapes=[
            pltpu.VMEM((PEER_TOPK, PEER_HEADS, tbs), F32),
            pltpu.VMEM((PEER_TOPK, PEER_HEADS, tbs), F32),
        ],
        compiler_params=pltpu.CompilerParams(
            dimension_semantics=("arbitrary",), vmem_limit_bytes=VMEM_LIMIT_BYTES),
        name="peer_select",
    )(x, norm_w.reshape(1, D_MODEL), w_q.astype(BF16), keys1.astype(BF16), keys2.astype(BF16))

    tok = lambda i, e: (i, 0)
    once = pl.Buffered(1)
    tw = 2 * LANES
    out = pl.pallas_call(
        functools.partial(_peer_expert_kernel, tb=tb, eb=eb, tw=tw, final_norm=final_norm),
        grid=(t_tot // tb, PEER_EXPERTS // eb),
        in_specs=[
            pl.BlockSpec((tb, D_MODEL), tok, pipeline_mode=once),
            pl.BlockSpec((tb, D_MODEL), tok, pipeline_mode=once),
            pl.BlockSpec((PEER_HEADS, eb // PEER_NKEYS, tb), lambda i, e: (0, e, i)),
            pl.BlockSpec((PEER_HEADS, PEER_NKEYS, tb), lambda i, e: (0, 0, i), pipeline_mode=once),
            pl.BlockSpec((PEER_HEADS, tb), lambda i, e: (0, i), pipeline_mode=once),
            pl.BlockSpec((eb, D_MODEL), lambda i, e: (e, 0)),
            pl.BlockSpec((D_MODEL, eb), lambda i, e: (0, e)),
            _const_spec((1, D_MODEL)),
        ],
        out_specs=pl.BlockSpec((tb, D_MODEL), tok),
        out_shape=jax.ShapeDtypeStruct((t_tot, D_MODEL), F32),
        scratch_shapes=(
            [pltpu.VMEM((D_MODEL, tb), F32)]
            + [pltpu.VMEM((eb, tw), F32) for _ in range(tb // tw)]
            + [pltpu.VMEM((eb, tw), BF16) for _ in range(tb // tw)]
        ),
        compiler_params=pltpu.CompilerParams(
            dimension_semantics=("arbitrary", "arbitrary"), vmem_limit_bytes=VMEM_LIMIT_BYTES),
        name="peer_experts",
    )(hb, x, s1, s2, thr, u_tab, vt_tab, final_w.reshape(1, D_MODEL))
    return out


def _trunk(x, s5_re, s5_im, s_hg, p, s5_prm, lb_all):
    nb, seq, _ = x.shape
    new_re, new_im, new_hg = [], [], []
    for i in range(DEPTH):
        j = i // N_MIXERS
        if i % N_MIXERS == 0:
            a, bbd, ccd = s5_prm[j]
            x, sr, si = _s5_layer(x, s5_re[j], s5_im[j], p["norm_mix"][i], a, bbd, ccd,
                                  p["s5_d"][j], p["s5_w_glu"][j].astype(BF16), lc=min(seq, 64))
            new_re.append(sr)
            new_im.append(si)
        else:
            x, s_new = _hgrn_layer(x, s_hg[j], p["norm_mix"][i], p["hg_w_in"][j], lb_all[i],
                                   p["hg_norm_w"][j], p["hg_w_out"][j])
            new_hg.append(s_new)
        x = _peer_layer(x.reshape(nb * seq, D_MODEL), p["norm_ffn"][i], p["peer_w_q"][i],
                        p["peer_keys1"][i], p["peer_keys2"][i], p["peer_u"][i], p["peer_v"][i],
                        p["norm_final"], final_norm=(i == DEPTH - 1)).reshape(nb, seq, D_MODEL)
    return x, jnp.stack(new_re), jnp.stack(new_im), jnp.stack(new_hg)


def kernel(x_prompt, x_sample, state_s5_re, state_s5_im, state_hgrn, norm_mix, norm_ffn, norm_final,
           s5_lambda_re, s5_lambda_im, s5_log_dt, s5_b_re, s5_b_im, s5_c_re, s5_c_im, s5_d, s5_w_glu,
           hg_w_in, hg_lower_bounds, hg_norm_w, hg_w_out, peer_w_q, peer_keys1, peer_keys2, peer_u, peer_v):
    p = dict(norm_mix=norm_mix, norm_ffn=norm_ffn, norm_final=norm_final, s5_d=s5_d, s5_w_glu=s5_w_glu,
             hg_w_in=hg_w_in, hg_norm_w=hg_norm_w, hg_w_out=hg_w_out, peer_w_q=peer_w_q,
             peer_keys1=peer_keys1, peer_keys2=peer_keys2,
             peer_u=[peer_u[i].astype(BF16) for i in range(DEPTH)],
             peer_v=[peer_v[i].T.astype(BF16) for i in range(DEPTH)])
    n_a = s5_lambda_re.shape[0]
    s5_prm = [_s5_params(s5_lambda_re[j], s5_lambda_im[j], s5_log_dt[j], s5_b_re[j], s5_b_im[j],
                         s5_c_re[j], s5_c_im[j]) for j in range(n_a)]
    lb_all = jnp.cumsum(jax.nn.softmax(hg_lower_bounds.astype(F32), axis=0), axis=0)
    lb_all = lb_all - lb_all[0:1]
    nbp = x_prompt.shape[0]
    z_s5 = jnp.zeros((n_a, nbp, S5_GROUPS, S5_STATE), state_s5_re.dtype)
    z_hg = jnp.zeros((state_hgrn.shape[0], nbp, HG_HEADS, HG_DK, HG_DV), state_hgrn.dtype)
    y_p, re_p, im_p, hg_p = _trunk(x_prompt, z_s5, z_s5, z_hg, p, s5_prm, lb_all)
    y_s, re_s, im_s, hg_s = _trunk(x_sample, state_s5_re, state_s5_im, state_hgrn, p, s5_prm, lb_all)
    return (y_p, y_s, re_p, im_p, hg_p, re_s, im_s, hg_s)
```

```python
import functools
import math

import jax
import jax.numpy as jnp
from jax import lax
from jax.experimental import pallas as pl
from jax.experimental.pallas import tpu as pltpu

F32 = jnp.float32
BF16 = jnp.bfloat16

D_MODEL = 1024
DEPTH = 2
N_MIXERS = 2
S5_GROUP = 16
S5_GROUPS = D_MODEL // S5_GROUP
S5_STATE = 64
HG_HEADS = 8
HG_DK = 128
HG_DV = 128
HG_FDIM = HG_HEADS * HG_DK
HG_CHUNK = 64
PEER_HEADS = 8
PEER_NKEYS = 128
PEER_EXPERTS = PEER_NKEYS * PEER_NKEYS
PEER_TOPK = 16
PEER_DKEY = 256
PEER_HALF = PEER_DKEY // 2
RMS_EPS = 1e-6

LANES = 128
SUBLANES = 8
VMEM_LIMIT_BYTES = 56 * 2**20

S5_CB = 4
S5_CB_GROUPS = S5_GROUPS // S5_CB
S5_CB_IN = S5_CB_GROUPS * S5_GROUP
S5_CB_STATE = S5_CB_GROUPS * S5_STATE


def _const_spec(shape):
    nd = len(shape)
    return pl.BlockSpec(shape, lambda *_: (0,) * nd, pipeline_mode=pl.Buffered(1))


def _rmsnorm(x, w):
    ms = jnp.mean(x * x, axis=-1, keepdims=True)
    return x * lax.rsqrt(ms + RMS_EPS) * w


def _gelu(x):
    return 0.5 * x * (1.0 + lax.erf(x * (1.0 / math.sqrt(2.0))))


def _sigmoid(x):
    return 1.0 / (1.0 + jnp.exp(-x))


def _dot(a, b):
    return jnp.dot(a, b, preferred_element_type=F32)


def _dot_nt(a, b):
    return lax.dot_general(a, b, (((1,), (1,)), ((), ())), preferred_element_type=F32)


def _dot_tn(a, b):
    return lax.dot_general(a, b, (((0,), (0,)), ((), ())), preferred_element_type=F32)


def _s5_discretize_kernel(lre_ref, lim_ref, ldt_ref, are_ref, aim_ref, zre_ref, zim_ref):
    lr = jnp.minimum(lre_ref[...], -1e-4)
    li = lim_ref[...]
    dt = jnp.exp(ldt_ref[...])
    mag = jnp.exp(lr * dt)
    ang = li * dt
    ab_re = mag * jnp.cos(ang)
    ab_im = mag * jnp.sin(ang)
    den = lr * lr + li * li
    nr = ab_re - 1.0
    are_ref[...] = ab_re
    aim_ref[...] = ab_im
    zre_ref[...] = (nr * lr + ab_im * li) / den
    zim_ref[...] = (ab_im * lr - nr * li) / den


def _s5_scale_b_kernel(zre_ref, zim_ref, bre_ref, bim_ref, ore_ref, oim_ref):
    zr = zre_ref[...]
    zi = zim_ref[...]
    br = bre_ref[...]
    bi = bim_ref[...]
    ore_ref[...] = zr * br - zi * bi
    oim_ref[...] = zr * bi + zi * br


def _s5_params(lam_re, lam_im, log_dt, b_re, b_im, c_re, c_im):
    g, p, hh = S5_GROUPS, S5_STATE, S5_GROUP
    gp = jax.ShapeDtypeStruct((g, p), F32)
    a_re, a_im, z_re, z_im = pl.pallas_call(
        _s5_discretize_kernel, out_shape=(gp, gp, gp, gp), name="s5_discretize",
    )(lam_re.astype(F32), lam_im.astype(F32), log_dt.astype(F32).reshape(g, 1))
    gph = jax.ShapeDtypeStruct((g * p, hh), F32)
    bb_re, bb_im = pl.pallas_call(
        _s5_scale_b_kernel, out_shape=(gph, gph), name="s5_scale_b",
    )(z_re.reshape(g * p, 1), z_im.reshape(g * p, 1),
      b_re.astype(F32).reshape(g * p, hh), b_im.astype(F32).reshape(g * p, hh))
    eye = jnp.eye(S5_CB_GROUPS, dtype=F32)

    def blockdiag_in(w):
        w4 = w.reshape(S5_CB, S5_CB_GROUPS, p, hh)
        return jnp.einsum('cgph,gk->cghkp', w4, eye).reshape(S5_CB, S5_CB_IN, S5_CB_STATE)

    def blockdiag_out(w):
        w4 = w.astype(F32).reshape(S5_CB, S5_CB_GROUPS, hh, p)
        return jnp.einsum('cghp,gk->cgpkh', w4, eye).reshape(S5_CB, S5_CB_STATE, S5_CB_IN)

    bbd = jnp.concatenate([blockdiag_in(bb_re), blockdiag_in(bb_im)], axis=-1).astype(BF16)
    ccd = jnp.concatenate([blockdiag_out(c_re), -blockdiag_out(c_im)], axis=1).astype(BF16)
    a = jnp.stack([a_re.reshape(S5_CB, S5_CB_STATE), a_im.reshape(S5_CB, S5_CB_STATE)], axis=1)
    return a, bbd, ccd


def _s5_pack_state(s_re, s_im):
    nb = s_re.shape[0]
    return jnp.stack([s_re.reshape(nb, S5_CB, S5_CB_STATE), s_im.reshape(nb, S5_CB, S5_CB_STATE)],
                     axis=2).reshape(nb, 2 * S5_GROUPS * S5_STATE)


def _s5_unpack_state(s):
    nb = s.shape[0]
    s4 = s.reshape(nb, S5_CB, 2, S5_CB_STATE)
    return (s4[:, :, 0].reshape(nb, S5_GROUPS, S5_STATE), s4[:, :, 1].reshape(nb, S5_GROUPS, S5_STATE))


def _s5_kernel(x_ref, s0_ref, nw_ref, a_ref, bbd_ref, ccd_ref, d_ref, wglu_ref,
               out_ref, sfin_ref, bu_ref, st_ref, hs_ref, ys_ref, *, nb, lc):
    rows = nb * lc
    nk = D_MODEL // LANES
    lane = lambda k: slice(k * LANES, (k + 1) * LANES)

    @pl.when(pl.program_id(0) == 0)
    def _():
        st_ref[...] = s0_ref[...]

    x = x_ref[...].reshape(rows, D_MODEL)
    h = _rmsnorm(x, nw_ref[...])
    if nb <= lc:
        for b in range(nb):
            for k in range(nk):
                hs_ref[k, pl.ds(b, lc, stride=nb), :] = h[b * lc:(b + 1) * lc, lane(k)]
        hb = jnp.concatenate([hs_ref[k] for k in range(nk)], axis=-1).astype(BF16)
    else:
        for k in range(nk):
            hs_ref[k] = h[:, lane(k)]
        hb = jnp.concatenate(
            [jnp.concatenate([hs_ref[k, pl.ds(t, nb, stride=lc), :] for t in range(lc)], axis=0)
             for k in range(nk)], axis=-1).astype(BF16)
    nt = S5_CB_STATE // LANES
    for cb in range(S5_CB):
        bu = _dot(hb[:, cb * S5_CB_IN:(cb + 1) * S5_CB_IN], bbd_ref[cb])
        for k in range(2 * nt):
            bu_ref[k] = bu[:, lane(k)]
        a_re = [jnp.broadcast_to(a_ref[cb, 0:1, lane(k)], (SUBLANES, LANES)) for k in range(nt)]
        a_im = [jnp.broadcast_to(a_ref[cb, 1:2, lane(k)], (SUBLANES, LANES)) for k in range(nt)]
        col = cb * 2 * S5_CB_STATE
        for j in range(nb // SUBLANES):
            brow = j * SUBLANES

            def step(t, carry, brow=brow):
                idx = pl.ds(pl.multiple_of(t * nb + brow, SUBLANES), SUBLANES)
                new_re, new_im = [], []
                for k in range(nt):
                    s_re, s_im = carry[k], carry[nt + k]
                    n_re = a_re[k] * s_re - a_im[k] * s_im + bu_ref[k, idx, :]
                    n_im = a_re[k] * s_im + a_im[k] * s_re + bu_ref[nt + k, idx, :]
                    bu_ref[k, idx, :] = n_re
                    bu_ref[nt + k, idx, :] = n_im
                    new_re.append(n_re)
                    new_im.append(n_im)
                return tuple(new_re) + tuple(new_im)

            carry = tuple(st_ref[brow:brow + SUBLANES, col + k * LANES:col + (k + 1) * LANES]
                          for k in range(2 * nt))
            if lc <= 8:
                for t in range(lc):
                    carry = step(t, carry)
            else:
                carry = lax.fori_loop(0, lc, step, carry, unroll=4)
            for k in range(2 * nt):
                st_ref[brow:brow + SUBLANES, col + k * LANES:col + (k + 1) * LANES] = carry[k]
        st_all = jnp.concatenate([bu_ref[k] for k in range(2 * nt)], axis=-1)
        y_cb = _dot(st_all.astype(BF16), ccd_ref[cb])
        for k in range(S5_CB_IN // LANES):
            ys_ref[cb * (S5_CB_IN // LANES) + k] = y_cb[:, lane(k)]
    if nb <= lc:
        y = jnp.concatenate(
            [jnp.concatenate([ys_ref[k, pl.ds(b, lc, stride=nb), :] for b in range(nb)], axis=0)
             for k in range(nk)], axis=-1)
    else:
        for t in range(lc):
            for k in range(nk):
                hs_ref[k, pl.ds(t, nb, stride=lc), :] = ys_ref[k, t * nb:(t + 1) * nb, :]
        y = jnp.concatenate([hs_ref[k] for k in range(nk)], axis=-1)
    y = _gelu(y + d_ref[...] * h)
    z = _dot(y.astype(BF16), wglu_ref[...])
    out = x + z[:, :D_MODEL] * _sigmoid(z[:, D_MODEL:])
    out_ref[...] = out.reshape(out_ref.shape)
    sfin_ref[...] = st_ref[...]


def _s5_layer(x, s0_re, s0_im, norm_w, a, bbd, ccd, d, w_glu, *, lc):
    nb, seq, _ = x.shape
    nstate = 2 * S5_GROUPS * S5_STATE
    rows = nb * lc
    if seq == lc:
        xin = x.reshape(rows, D_MODEL)
        x_spec = pl.BlockSpec((rows, D_MODEL), lambda c: (0, 0))
    else:
        xin = x
        x_spec = pl.BlockSpec((nb, lc, D_MODEL), lambda c: (0, c, 0))
    out, sfin = pl.pallas_call(
        functools.partial(_s5_kernel, nb=nb, lc=lc),
        grid=(seq // lc,),
        in_specs=[
            x_spec,
            _const_spec((nb, nstate)),
            _const_spec((1, D_MODEL)),
            _const_spec((S5_CB, 2, S5_CB_STATE)),
            _const_spec((S5_CB, S5_CB_IN, 2 * S5_CB_STATE)),
            _const_spec((S5_CB, 2 * S5_CB_STATE, S5_CB_IN)),
            _const_spec((1, D_MODEL)),
            _const_spec((D_MODEL, 2 * D_MODEL)),
        ],
        out_specs=(x_spec, pl.BlockSpec((nb, nstate), lambda c: (0, 0))),
        out_shape=(jax.ShapeDtypeStruct(xin.shape, F32), jax.ShapeDtypeStruct((nb, nstate), F32)),
        scratch_shapes=[
            pltpu.VMEM((2 * S5_CB_STATE // LANES, rows, LANES), F32),
            pltpu.VMEM((nb, nstate), F32),
            pltpu.VMEM((D_MODEL // LANES, rows, LANES), F32),
            pltpu.VMEM((D_MODEL // LANES, rows, LANES), F32),
        ],
        compiler_params=pltpu.CompilerParams(
            dimension_semantics=("arbitrary",), vmem_limit_bytes=VMEM_LIMIT_BYTES),
        name="s5_layer",
    )(xin, _s5_pack_state(s0_re, s0_im), norm_w.reshape(1, D_MODEL), a, bbd, ccd,
      d.reshape(1, D_MODEL), w_glu)
    new_re, new_im = _s5_unpack_state(sfin)
    return out.reshape(x.shape), new_re, new_im


def _hgrn_gates(proj, lb):
    q = proj[:, 0:HG_FDIM]
    q = q * _sigmoid(q)
    fz = proj[:, HG_FDIM:2 * HG_FDIM]
    v = proj[:, 2 * HG_FDIM:2 * HG_FDIM + D_MODEL]
    g = proj[:, 2 * HG_FDIM + D_MODEL:]
    log_sig = jnp.minimum(fz, 0.0) - jnp.log1p(jnp.exp(-jnp.abs(fz)))
    t1 = jnp.log(lb)
    t2 = jnp.log1p(-lb) + log_sig
    hi = jnp.maximum(t1, t2)
    lo = jnp.minimum(t1, t2)
    logf = hi + jnp.log1p(jnp.exp(lo - hi))
    k = (1.0 - lb) * _sigmoid(-fz)
    return q, k, v, g, logf


def _hgrn_tile(q, k, v, g, logf, st_refs, nw, *, chunk):
    rows = q.shape[0]
    nseg = rows // chunk
    ri = lax.broadcasted_iota(jnp.int32, (rows, rows), 0)
    ci = lax.broadcasted_iota(jnp.int32, (rows, rows), 1)
    if nseg == 1:
        same = ci <= ri
    else:
        same = (ci <= ri) & ((ri // chunk) == (ci // chunk))
    tri = jnp.where(same, 1.0, 0.0).astype(F32)
    b = jnp.dot(tri, logf, precision=lax.Precision.HIGHEST, preferred_element_type=F32)
    rowid = lax.broadcasted_iota(jnp.int32, (rows, 1), 0)
    b_mid = b[chunk // 2:chunk // 2 + 1, :]
    b_last = b[chunk - 1:chunk, :]
    for s in range(1, nseg):
        in_s = rowid >= s * chunk
        b_mid = jnp.where(in_s, b[s * chunk + chunk // 2:s * chunk + chunk // 2 + 1, :], b_mid)
        b_last = jnp.where(in_s, b[s * chunk + chunk - 1:s * chunk + chunk, :], b_last)
    qs = (q * jnp.exp(b - b_mid)).astype(BF16)
    ks = (k * jnp.exp(b_mid - b)).astype(BF16)
    qi = (q * jnp.exp(b)).astype(BF16)
    kd = k * jnp.exp(b_last - b)
    vb = v.astype(BF16)
    outs = []
    for hd in range(HG_HEADS):
        sl = slice(hd * HG_DK, (hd + 1) * HG_DK)
        scores = jnp.where(same, _dot_nt(qs[:, sl], ks[:, sl]), 0.0)
        o = _dot(scores.astype(BF16), vb[:, sl])
        v_t = v[:, sl].T
        for s in range(nseg):
            st = st_refs[s][hd]
            o_s = _dot_nt(qi[:, sl], st.astype(BF16))
            kd_s = kd[:, sl]
            if nseg > 1:
                in_s = (rowid >= s * chunk) & (rowid < (s + 1) * chunk)
                o_s = jnp.where(in_s, o_s, 0.0)
                kd_s = jnp.where(in_s, kd_s, 0.0)
            o = o + o_s
            dec = jnp.exp(b[s * chunk + chunk - 1:s * chunk + chunk, sl])
            st_refs[s][hd] = dec * st + _dot(v_t.astype(BF16), kd_s.astype(BF16))
        o = o * lax.rsqrt(jnp.mean(o * o, axis=-1, keepdims=True) + RMS_EPS)
        gh = g[:, sl]
        outs.append(o * nw * (gh * _sigmoid(gh)))
    return jnp.concatenate(outs, axis=-1)


def _hgrn_prompt_kernel(x_ref, s0_ref, nw_ref, win_ref, lb_ref, hnw_ref, wout_ref,
                        out_ref, sout_ref, st_ref, y_ref, *, lc, chunk):
    c = pl.program_id(1)

    @pl.when(c == 0)
    def _():
        for hd in range(HG_HEADS):
            st_ref[hd] = s0_ref[0, hd].T

    x = x_ref[0]
    hb = _rmsnorm(x, nw_ref[...]).astype(BF16)
    proj = _dot(hb, win_ref[...])
    q, k, v, g, logf = _hgrn_gates(proj, lb_ref[...])
    for sub in range(lc // chunk):
        r = slice(sub * chunk, (sub + 1) * chunk)
        y_ref[r, :] = _hgrn_tile(q[r], k[r], v[r], g[r], logf[r], [st_ref], hnw_ref[...], chunk=chunk)
    out_ref[0] = x + _dot(y_ref[...].astype(BF16), wout_ref[...])

    @pl.when(c == pl.num_programs(1) - 1)
    def _():
        for hd in range(HG_HEADS):
            sout_ref[0, hd] = st_ref[hd].T


def _hgrn_sample_kernel(x_ref, s0_ref, nw_ref, win_ref, lb_ref, hnw_ref, wout_ref,
                        out_ref, sout_ref, st_ref, proj_ref, y_ref, *, seq):
    p = pl.program_id(0)
    nseg = SUBLANES // seq

    @pl.when(p == 0)
    def _():
        hb = _rmsnorm(x_ref[...], nw_ref[...]).astype(BF16)
        proj_ref[...] = _dot(hb, win_ref[...])

    for s in range(nseg):
        for hd in range(HG_HEADS):
            st_ref[s, hd] = s0_ref[s, hd].T
    r = pl.ds(pl.multiple_of(p * SUBLANES, SUBLANES), SUBLANES)
    q, k, v, g, logf = _hgrn_gates(proj_ref[r, :], lb_ref[...])
    y_ref[r, :] = _hgrn_tile(q, k, v, g, logf, [st_ref.at[s] for s in range(nseg)], hnw_ref[...],
                             chunk=seq)
    for s in range(nseg):
        for hd in range(HG_HEADS):
            sout_ref[s, hd] = st_ref[s, hd].T

    @pl.when(p == pl.num_programs(0) - 1)
    def _():
        out_ref[...] = x_ref[...] + _dot(y_ref[...].astype(BF16), wout_ref[...])


def _hgrn_layer(x, s0, norm_w, w_in, lb, hg_norm_w, w_out):
    nb, seq, _ = x.shape
    weights = (norm_w.reshape(1, D_MODEL), w_in.astype(BF16), lb.reshape(1, HG_FDIM),
               hg_norm_w.reshape(1, HG_DV), w_out.astype(BF16))
    w_specs = [
        _const_spec((1, D_MODEL)),
        _const_spec((D_MODEL, 2 * HG_FDIM + 2 * D_MODEL)),
        _const_spec((1, HG_FDIM)),
        _const_spec((1, HG_DV)),
        _const_spec((D_MODEL, D_MODEL)),
    ]
    s_shape = jax.ShapeDtypeStruct(s0.shape, F32)
    if seq >= HG_CHUNK:
        chunk = HG_CHUNK
        lc = 4 * chunk
        x_spec = pl.BlockSpec((1, lc, D_MODEL), lambda b, c: (b, c, 0))
        s_spec = pl.BlockSpec((1, HG_HEADS, HG_DK, HG_DV), lambda b, c: (b, 0, 0, 0))
        out, s_new = pl.pallas_call(
            functools.partial(_hgrn_prompt_kernel, lc=lc, chunk=chunk),
            grid=(nb, seq // lc),
            in_specs=[x_spec, s_spec] + w_specs,
            out_specs=(x_spec, s_spec),
            out_shape=(jax.ShapeDtypeStruct(x.shape, F32), s_shape),
            scratch_shapes=[
                pltpu.VMEM((HG_HEADS, HG_DV, HG_DK), F32),
                pltpu.VMEM((lc, D_MODEL), F32),
            ],
            compiler_params=pltpu.CompilerParams(
                dimension_semantics=("arbitrary", "arbitrary"), vmem_limit_bytes=VMEM_LIMIT_BYTES),
            name="hgrn_prompt",
        )(x, s0, *weights)
        return out, s_new
    assert SUBLANES % seq == 0 and nb % (SUBLANES // seq) == 0
    nseg = SUBLANES // seq
    rows = nb * seq
    x_spec = _const_spec((rows, D_MODEL))
    s_spec = pl.BlockSpec((nseg, HG_HEADS, HG_DK, HG_DV), lambda p: (p, 0, 0, 0))
    out, s_new = pl.pallas_call(
        functools.partial(_hgrn_sample_kernel, seq=seq),
        grid=(nb // nseg,),
        in_specs=[x_spec, s_spec] + w_specs,
        out_specs=(pl.BlockSpec((rows, D_MODEL), lambda p: (0, 0)), s_spec),
        out_shape=(jax.ShapeDtypeStruct((rows, D_MODEL), F32), s_shape),
        scratch_shapes=[
            pltpu.VMEM((nseg, HG_HEADS, HG_DV, HG_DK), F32),
            pltpu.VMEM((rows, 2 * HG_FDIM + 2 * D_MODEL), F32),
            pltpu.VMEM((rows, D_MODEL), F32),
        ],
        compiler_params=pltpu.CompilerParams(
            dimension_semantics=("arbitrary",), vmem_limit_bytes=VMEM_LIMIT_BYTES),
        name="hgrn_sample",
    )(x.reshape(rows, D_MODEL), s0, *weights)
    return out.reshape(x.shape), s_new


def _compare_exchange(v, i, j):
    hi = jnp.maximum(v[i], v[j])
    lo = jnp.minimum(v[i], v[j])
    v[i] = hi
    v[j] = lo


def _bitonic_merge_desc(v):
    n = len(v)
    j = n // 2
    while j >= 1:
        for i in range(n):
            if i & j == 0:
                _compare_exchange(v, i, i + j)
        j //= 2


def _sort_desc(v):
    n = len(v)
    k = 2
    while k <= n:
        j = k // 2
        while j >= 1:
            for i in range(n):
                l = i ^ j
                if l > i:
                    if i & k == 0:
                        _compare_exchange(v, i, l)
                    else:
                        _compare_exchange(v, l, i)
            j //= 2
        k *= 2
    return v


def _merge_top(a, b):
    n = len(a)
    v = [jnp.maximum(a[r], b[n - 1 - r]) for r in range(n)]
    _bitonic_merge_desc(v)
    return v


def _top16_rows(s):
    v = [s[SUBLANES * j:SUBLANES * (j + 1), :] for j in range(PEER_NKEYS // SUBLANES)]
    v = _sort_desc(v)
    for shift in (1, 2, 4):
        v = _merge_top(v, [pltpu.roll(x, shift, axis=0) for x in v])
    return v


def _candidate_top(a, b):
    top = [a[0] + b[j] for j in range(PEER_TOPK)]
    for i in range(1, PEER_TOPK):
        top = _merge_top(top, [a[i] + b[j] for j in range(PEER_TOPK)])
    return top


def _peer_select_kernel(x_ref, nw_ref, wq_ref, k1_ref, k2_ref,
                        hb_ref, s1_ref, s2_ref, thr_ref, t1_ref, t2_ref):
    hb = _rmsnorm(x_ref[...], nw_ref[...]).astype(BF16)
    hb_ref[...] = hb
    qb = _dot(hb, wq_ref[...]).astype(BF16)
    k1 = k1_ref[...]
    k2 = k2_ref[...]
    log2e = 1.0 / math.log(2.0)
    for hd in range(PEER_HEADS):
        base = hd * PEER_DKEY
        s1 = _dot_nt(k1, qb[:, base:base + PEER_HALF]) * log2e
        s2 = _dot_nt(k2, qb[:, base + PEER_HALF:base + PEER_DKEY]) * log2e
        s1_ref[hd] = s1
        s2_ref[hd] = s2
        for s, t_ref in ((s1, t1_ref), (s2, t2_ref)):
            top = _top16_rows(s)
            for r in range(PEER_TOPK):
                t_ref[r, hd:hd + 1, :] = top[r][0:1, :]
    a = [t1_ref[r] for r in range(PEER_TOPK)]
    b = [t2_ref[r] for r in range(PEER_TOPK)]
    top = _candidate_top(a, b)
    z = jnp.exp2(top[0] - top[0])
    for r in range(1, PEER_TOPK):
        z = z + jnp.exp2(top[r] - top[0])
    shift = top[0] + jnp.log2(z) + 1.0
    for hd in range(PEER_HEADS):
        s1_ref[hd] = s1_ref[hd] - shift[hd:hd + 1, :]
    thr_ref[...] = _candidate_top([v - shift for v in a], b)[PEER_TOPK - 1]


def _peer_expert_kernel(hb_ref, x_ref, s1_ref, s2_ref, thr_ref, u_ref, vp_ref, v_ref, fw_ref,
                        out_ref, acc_ref, *tile_refs, tb, eb, tw, final_norm):
    e = pl.program_id(1)
    ntile = tb // tw
    ht_refs, at_refs = tile_refs[:ntile], tile_refs[ntile:]

    @pl.when(e == 0)
    def _():
        acc_ref[...] = jnp.zeros_like(acc_ref)
        for at_ref in at_refs:
            at_ref[...] = jnp.zeros_like(at_ref)

    u = u_ref[...]
    vp = vp_ref[...]
    for i in range(ntile):
        ht_refs[i][...] = _dot_nt(u, hb_ref[i * tw:(i + 1) * tw, :])
        acc_ref[:, i * tw:(i + 1) * tw] += _dot_tn(vp, at_refs[i][...])
    for i in range(ntile):
        for lt in range(tw // LANES):
            lcol = slice(lt * LANES, (lt + 1) * LANES)
            col = slice(i * tw + lt * LANES, i * tw + (lt + 1) * LANES)
            for r in range(eb // PEER_NKEYS):
                gate = None
                for hd in range(PEER_HEADS):
                    t = s2_ref[hd, :, col] + s1_ref[hd, r:r + 1, col]
                    g = jnp.where(t >= thr_ref[hd:hd + 1, col], jnp.exp2(t), 0.0)
                    gate = g if gate is None else gate + g
                rr = slice(r * PEER_NKEYS, (r + 1) * PEER_NKEYS)
                h = ht_refs[i][rr, lcol]
                at_refs[i][rr, lcol] = (h * (1.0 + lax.erf(h * (1.0 / math.sqrt(2.0)))) * gate).astype(BF16)

    @pl.when(e == pl.num_programs(1) - 1)
    def _():
        v = v_ref[...]
        for i in range(ntile):
            acc_ref[:, i * tw:(i + 1) * tw] += _dot_tn(v, at_refs[i][...])
        y = x_ref[...] + acc_ref[...].T
        if final_norm:
            y = _rmsnorm(y, fw_ref[...])
        out_ref[...] = y


def _peer_layer(x, norm_w, w_q, keys1, keys2, u_tab, v_tab, final_w, *, final_norm):
    t_tot = x.shape[0]
    tbs = min(512, t_tot)
    tb = min(1024, t_tot)
    eb = 8 * PEER_NKEYS
    hb, s1, s2, thr = pl.pallas_call(
        _peer_select_kernel,
        grid=(t_tot // tbs,),
        in_specs=[
            pl.BlockSpec((tbs, D_MODEL), lambda i: (i, 0)),
            _const_spec((1, D_MODEL)),
            _const_spec((D_MODEL, PEER_HEADS * PEER_DKEY)),
            _const_spec((PEER_NKEYS, PEER_HALF)),
            _const_spec((PEER_NKEYS, PEER_HALF)),
        ],
        out_specs=(
            pl.BlockSpec((tbs, D_MODEL), lambda i: (i, 0)),
            pl.BlockSpec((PEER_HEADS, PEER_NKEYS, tbs), lambda i: (0, 0, i)),
            pl.BlockSpec((PEER_HEADS, PEER_NKEYS, tbs), lambda i: (0, 0, i)),
            pl.BlockSpec((PEER_HEADS, tbs), lambda i: (0, i)),
        ),
        out_shape=(
            jax.ShapeDtypeStruct((t_tot, D_MODEL), BF16),
            jax.ShapeDtypeStruct((PEER_HEADS, PEER_NKEYS, t_tot), F32),
            jax.ShapeDtypeStruct((PEER_HEADS, PEER_NKEYS, t_tot), F32),
            jax.ShapeDtypeStruct((PEER_HEADS, t_tot), F32),
        ),
        scratch_shapes=[
            pltpu.VMEM((PEER_TOPK, PEER_HEADS, tbs), F32),
            pltpu.VMEM((PEER_TOPK, PEER_HEADS, tbs), F32),
        ],
        compiler_params=pltpu.CompilerParams(
            dimension_semantics=("arbitrary",), vmem_limit_bytes=VMEM_LIMIT_BYTES),
        name="peer_select",
    )(x, norm_w.reshape(1, D_MODEL), w_q.astype(BF16), keys1.astype(BF16), keys2.astype(BF16))

    tok = lambda i, e: (i, 0)
    once = pl.Buffered(1)
    tw = 2 * LANES
    out = pl.pallas_call(
        functools.partial(_peer_expert_kernel, tb=tb, eb=eb, tw=tw, final_norm=final_norm),
        grid=(t_tot // tb, PEER_EXPERTS // eb),
        in_specs=[
            pl.BlockSpec((tb, D_MODEL), tok, pipeline_mode=once),
            pl.BlockSpec((tb, D_MODEL), tok, pipeline_mode=once),
            pl.BlockSpec((PEER_HEADS, eb // PEER_NKEYS, tb), lambda i, e: (0, e, i)),
            pl.BlockSpec((PEER_HEADS, PEER_NKEYS, tb), lambda i, e: (0, 0, i), pipeline_mode=once),
            pl.BlockSpec((PEER_HEADS, tb), lambda i, e: (0, i), pipeline_mode=once),
            pl.BlockSpec((eb, D_MODEL), lambda i, e: (e, 0)),
            pl.BlockSpec((eb, D_MODEL), lambda i, e: (jnp.maximum(e - 1, 0), 0)),
            pl.BlockSpec((eb, D_MODEL), lambda i, e: (PEER_EXPERTS // eb - 1, 0), pipeline_mode=once),
            _const_spec((1, D_MODEL)),
        ],
        out_specs=pl.BlockSpec((tb, D_MODEL), tok),
        out_shape=jax.ShapeDtypeStruct((t_tot, D_MODEL), F32),
        scratch_shapes=(
            [pltpu.VMEM((D_MODEL, tb), F32)]
            + [pltpu.VMEM((eb, tw), F32) for _ in range(tb // tw)]
            + [pltpu.VMEM((eb, tw), BF16) for _ in range(tb // tw)]
        ),
        compiler_params=pltpu.CompilerParams(
            dimension_semantics=("arbitrary", "arbitrary"), vmem_limit_bytes=VMEM_LIMIT_BYTES),
        name="peer_experts",
    )(hb, x, s1, s2, thr, u_tab, v_tab, v_tab, final_w.reshape(1, D_MODEL))
    return out


def _trunk(x, s5_re, s5_im, s_hg, p, s5_prm, lb_all):
    nb, seq, _ = x.shape
    new_re, new_im, new_hg = [], [], []
    for i in range(DEPTH):
        j = i // N_MIXERS
        if i % N_MIXERS == 0:
            a, bbd, ccd = s5_prm[j]
            x, sr, si = _s5_layer(x, s5_re[j], s5_im[j], p["norm_mix"][i], a, bbd, ccd,
                                  p["s5_d"][j], p["s5_w_glu"][j].astype(BF16), lc=min(seq, 64))
            new_re.append(sr)
            new_im.append(si)
        else:
            x, s_new = _hgrn_layer(x, s_hg[j], p["norm_mix"][i], p["hg_w_in"][j], lb_all[i],
                                   p["hg_norm_w"][j], p["hg_w_out"][j])
            new_hg.append(s_new)
        x = _peer_layer(x.reshape(nb * seq, D_MODEL), p["norm_ffn"][i], p["peer_w_q"][i],
                        p["peer_keys1"][i], p["peer_keys2"][i], p["peer_u"][i], p["peer_v"][i],
                        p["norm_final"], final_norm=(i == DEPTH - 1)).reshape(nb, seq, D_MODEL)
    return x, jnp.stack(new_re), jnp.stack(new_im), jnp.stack(new_hg)


def kernel(x_prompt, x_sample, state_s5_re, state_s5_im, state_hgrn, norm_mix, norm_ffn, norm_final,
           s5_lambda_re, s5_lambda_im, s5_log_dt, s5_b_re, s5_b_im, s5_c_re, s5_c_im, s5_d, s5_w_glu,
           hg_w_in, hg_lower_bounds, hg_norm_w, hg_w_out, peer_w_q, peer_keys1, peer_keys2, peer_u, peer_v):
    p = dict(norm_mix=norm_mix, norm_ffn=norm_ffn, norm_final=norm_final, s5_d=s5_d, s5_w_glu=s5_w_glu,
             hg_w_in=hg_w_in, hg_norm_w=hg_norm_w, hg_w_out=hg_w_out, peer_w_q=peer_w_q,
             peer_keys1=peer_keys1, peer_keys2=peer_keys2,
             peer_u=[peer_u[i].astype(BF16) for i in range(DEPTH)],
             peer_v=[peer_v[i].astype(BF16) for i in range(DEPTH)])
    n_a = s5_lambda_re.shape[0]
    s5_prm = [_s5_params(s5_lambda_re[j], s5_lambda_im[j], s5_log_dt[j], s5_b_re[j], s5_b_im[j],
                         s5_c_re[j], s5_c_im[j]) for j in range(n_a)]
    lb_all = jnp.cumsum(jax.nn.softmax(hg_lower_bounds.astype(F32), axis=0), axis=0)
    lb_all = lb_all - lb_all[0:1]
    nbp = x_prompt.shape[0]
    z_s5 = jnp.zeros((n_a, nbp, S5_GROUPS, S5_STATE), state_s5_re.dtype)
    z_hg = jnp.zeros((state_hgrn.shape[0], nbp, HG_HEADS, HG_DK, HG_DV), state_hgrn.dtype)
    y_p, re_p, im_p, hg_p = _trunk(x_prompt, z_s5, z_s5, z_hg, p, s5_prm, lb_all)
    y_s, re_s, im_s, hg_s = _trunk(x_sample, state_s5_re, state_s5_im, state_hgrn, p, s5_prm, lb_all)
    return (y_p, y_s, re_p, im_p, hg_p, re_s, im_s, hg_s)
```

```python
import functools
import math

import jax
import jax.numpy as jnp
from jax import lax
from jax.experimental import pallas as pl
from jax.experimental.pallas import tpu as pltpu

F32 = jnp.float32
BF16 = jnp.bfloat16

D_MODEL = 1024
DEPTH = 2
N_MIXERS = 2
S5_GROUP = 16
S5_GROUPS = D_MODEL // S5_GROUP
S5_STATE = 64
HG_HEADS = 8
HG_DK = 128
HG_DV = 128
HG_FDIM = HG_HEADS * HG_DK
HG_CHUNK = 64
PEER_HEADS = 8
PEER_NKEYS = 128
PEER_EXPERTS = PEER_NKEYS * PEER_NKEYS
PEER_TOPK = 16
PEER_DKEY = 256
PEER_HALF = PEER_DKEY // 2
RMS_EPS = 1e-6

LANES = 128
SUBLANES = 8
VMEM_LIMIT_BYTES = 56 * 2**20

S5_CB = 4
S5_CB_GROUPS = S5_GROUPS // S5_CB
S5_CB_IN = S5_CB_GROUPS * S5_GROUP
S5_CB_STATE = S5_CB_GROUPS * S5_STATE


def _const_spec(shape):
    nd = len(shape)
    return pl.BlockSpec(shape, lambda *_: (0,) * nd, pipeline_mode=pl.Buffered(1))


def _rmsnorm(x, w):
    ms = jnp.mean(x * x, axis=-1, keepdims=True)
    return x * lax.rsqrt(ms + RMS_EPS) * w


def _gelu(x):
    return 0.5 * x * (1.0 + lax.erf(x * (1.0 / math.sqrt(2.0))))


def _sigmoid(x):
    return 1.0 / (1.0 + jnp.exp(-x))


def _dot(a, b):
    return jnp.dot(a, b, preferred_element_type=F32)


def _dot_nt(a, b):
    return lax.dot_general(a, b, (((1,), (1,)), ((), ())), preferred_element_type=F32)


def _dot_tn(a, b):
    return lax.dot_general(a, b, (((0,), (0,)), ((), ())), preferred_element_type=F32)


def _s5_discretize_kernel(lre_ref, lim_ref, ldt_ref, are_ref, aim_ref, zre_ref, zim_ref):
    lr = jnp.minimum(lre_ref[...], -1e-4)
    li = lim_ref[...]
    dt = jnp.exp(ldt_ref[...])
    mag = jnp.exp(lr * dt)
    ang = li * dt
    ab_re = mag * jnp.cos(ang)
    ab_im = mag * jnp.sin(ang)
    den = lr * lr + li * li
    nr = ab_re - 1.0
    are_ref[...] = ab_re
    aim_ref[...] = ab_im
    zre_ref[...] = (nr * lr + ab_im * li) / den
    zim_ref[...] = (ab_im * lr - nr * li) / den


def _s5_scale_b_kernel(zre_ref, zim_ref, bre_ref, bim_ref, ore_ref, oim_ref):
    zr = zre_ref[...]
    zi = zim_ref[...]
    br = bre_ref[...]
    bi = bim_ref[...]
    ore_ref[...] = zr * br - zi * bi
    oim_ref[...] = zr * bi + zi * br


def _s5_params(lam_re, lam_im, log_dt, b_re, b_im, c_re, c_im):
    g, p, hh = S5_GROUPS, S5_STATE, S5_GROUP
    gp = jax.ShapeDtypeStruct((g, p), F32)
    a_re, a_im, z_re, z_im = pl.pallas_call(
        _s5_discretize_kernel, out_shape=(gp, gp, gp, gp), name="s5_discretize",
    )(lam_re.astype(F32), lam_im.astype(F32), log_dt.astype(F32).reshape(g, 1))
    gph = jax.ShapeDtypeStruct((g * p, hh), F32)
    bb_re, bb_im = pl.pallas_call(
        _s5_scale_b_kernel, out_shape=(gph, gph), name="s5_scale_b",
    )(z_re.reshape(g * p, 1), z_im.reshape(g * p, 1),
      b_re.astype(F32).reshape(g * p, hh), b_im.astype(F32).reshape(g * p, hh))
    eye = jnp.eye(S5_CB_GROUPS, dtype=F32)

    def blockdiag_in(w):
        w4 = w.reshape(S5_CB, S5_CB_GROUPS, p, hh)
        return jnp.einsum('cgph,gk->cghkp', w4, eye).reshape(S5_CB, S5_CB_IN, S5_CB_STATE)

    def blockdiag_out(w):
        w4 = w.astype(F32).reshape(S5_CB, S5_CB_GROUPS, hh, p)
        return jnp.einsum('cghp,gk->cgpkh', w4, eye).reshape(S5_CB, S5_CB_STATE, S5_CB_IN)

    bbd = jnp.concatenate([blockdiag_in(bb_re), blockdiag_in(bb_im)], axis=-1).astype(BF16)
    ccd = jnp.concatenate([blockdiag_out(c_re), -blockdiag_out(c_im)], axis=1).astype(BF16)
    a = jnp.stack([a_re.reshape(S5_CB, S5_CB_STATE), a_im.reshape(S5_CB, S5_CB_STATE)], axis=1)
    return a, bbd, ccd


def _s5_pack_state(s_re, s_im):
    nb = s_re.shape[0]
    return jnp.stack([s_re.reshape(nb, S5_CB, S5_CB_STATE), s_im.reshape(nb, S5_CB, S5_CB_STATE)],
                     axis=2).reshape(nb, 2 * S5_GROUPS * S5_STATE)


def _s5_unpack_state(s):
    nb = s.shape[0]
    s4 = s.reshape(nb, S5_CB, 2, S5_CB_STATE)
    return (s4[:, :, 0].reshape(nb, S5_GROUPS, S5_STATE), s4[:, :, 1].reshape(nb, S5_GROUPS, S5_STATE))


def _s5_kernel(x_ref, s0_ref, nw_ref, a_ref, bbd_ref, ccd_ref, d_ref, wglu_ref,
               out_ref, sfin_ref, bu_ref, st_ref, hs_ref, ys_ref, *, nb, lc):
    rows = nb * lc
    nk = D_MODEL // LANES
    lane = lambda k: slice(k * LANES, (k + 1) * LANES)

    @pl.when(pl.program_id(0) == 0)
    def _():
        st_ref[...] = s0_ref[...]

    x = x_ref[...].reshape(rows, D_MODEL)
    h = _rmsnorm(x, nw_ref[...])
    if nb <= lc:
        for b in range(nb):
            for k in range(nk):
                hs_ref[k, pl.ds(b, lc, stride=nb), :] = h[b * lc:(b + 1) * lc, lane(k)]
        hb = jnp.concatenate([hs_ref[k] for k in range(nk)], axis=-1).astype(BF16)
    else:
        for k in range(nk):
            hs_ref[k] = h[:, lane(k)]
        hb = jnp.concatenate(
            [jnp.concatenate([hs_ref[k, pl.ds(t, nb, stride=lc), :] for t in range(lc)], axis=0)
             for k in range(nk)], axis=-1).astype(BF16)
    nt = S5_CB_STATE // LANES
    for cb in range(S5_CB):
        bu = _dot(hb[:, cb * S5_CB_IN:(cb + 1) * S5_CB_IN], bbd_ref[cb])
        for k in range(2 * nt):
            bu_ref[k] = bu[:, lane(k)]
        a_re = [jnp.broadcast_to(a_ref[cb, 0:1, lane(k)], (SUBLANES, LANES)) for k in range(nt)]
        a_im = [jnp.broadcast_to(a_ref[cb, 1:2, lane(k)], (SUBLANES, LANES)) for k in range(nt)]
        col = cb * 2 * S5_CB_STATE
        for j in range(nb // SUBLANES):
            brow = j * SUBLANES

            def step(t, carry, brow=brow):
                idx = pl.ds(pl.multiple_of(t * nb + brow, SUBLANES), SUBLANES)
                new_re, new_im = [], []
                for k in range(nt):
                    s_re, s_im = carry[k], carry[nt + k]
                    n_re = a_re[k] * s_re - a_im[k] * s_im + bu_ref[k, idx, :]
                    n_im = a_re[k] * s_im + a_im[k] * s_re + bu_ref[nt + k, idx, :]
                    bu_ref[k, idx, :] = n_re
                    bu_ref[nt + k, idx, :] = n_im
                    new_re.append(n_re)
                    new_im.append(n_im)
                return tuple(new_re) + tuple(new_im)

            carry = tuple(st_ref[brow:brow + SUBLANES, col + k * LANES:col + (k + 1) * LANES]
                          for k in range(2 * nt))
            if lc <= 8:
                for t in range(lc):
                    carry = step(t, carry)
            else:
                carry = lax.fori_loop(0, lc, step, carry, unroll=4)
            for k in range(2 * nt):
                st_ref[brow:brow + SUBLANES, col + k * LANES:col + (k + 1) * LANES] = carry[k]
        st_all = jnp.concatenate([bu_ref[k] for k in range(2 * nt)], axis=-1)
        y_cb = _dot(st_all.astype(BF16), ccd_ref[cb])
        for k in range(S5_CB_IN // LANES):
            ys_ref[cb * (S5_CB_IN // LANES) + k] = y_cb[:, lane(k)]
    if nb <= lc:
        y = jnp.concatenate(
            [jnp.concatenate([ys_ref[k, pl.ds(b, lc, stride=nb), :] for b in range(nb)], axis=0)
             for k in range(nk)], axis=-1)
    else:
        for t in range(lc):
            for k in range(nk):
                hs_ref[k, pl.ds(t, nb, stride=lc), :] = ys_ref[k, t * nb:(t + 1) * nb, :]
        y = jnp.concatenate([hs_ref[k] for k in range(nk)], axis=-1)
    y = _gelu(y + d_ref[...] * h)
    z = _dot(y.astype(BF16), wglu_ref[...])
    out = x + z[:, :D_MODEL] * _sigmoid(z[:, D_MODEL:])
    out_ref[...] = out.reshape(out_ref.shape)
    sfin_ref[...] = st_ref[...]


def _s5_layer(x, s0_re, s0_im, norm_w, a, bbd, ccd, d, w_glu, *, lc):
    nb, seq, _ = x.shape
    nstate = 2 * S5_GROUPS * S5_STATE
    rows = nb * lc
    if seq == lc:
        xin = x.reshape(rows, D_MODEL)
        x_spec = pl.BlockSpec((rows, D_MODEL), lambda c: (0, 0))
    else:
        xin = x
        x_spec = pl.BlockSpec((nb, lc, D_MODEL), lambda c: (0, c, 0))
    out, sfin = pl.pallas_call(
        functools.partial(_s5_kernel, nb=nb, lc=lc),
        grid=(seq // lc,),
        in_specs=[
            x_spec,
            _const_spec((nb, nstate)),
            _const_spec((1, D_MODEL)),
            _const_spec((S5_CB, 2, S5_CB_STATE)),
            _const_spec((S5_CB, S5_CB_IN, 2 * S5_CB_STATE)),
            _const_spec((S5_CB, 2 * S5_CB_STATE, S5_CB_IN)),
            _const_spec((1, D_MODEL)),
            _const_spec((D_MODEL, 2 * D_MODEL)),
        ],
        out_specs=(x_spec, pl.BlockSpec((nb, nstate), lambda c: (0, 0))),
        out_shape=(jax.ShapeDtypeStruct(xin.shape, F32), jax.ShapeDtypeStruct((nb, nstate), F32)),
        scratch_shapes=[
            pltpu.VMEM((2 * S5_CB_STATE // LANES, rows, LANES), F32),
            pltpu.VMEM((nb, nstate), F32),
            pltpu.VMEM((D_MODEL // LANES, rows, LANES), F32),
            pltpu.VMEM((D_MODEL // LANES, rows, LANES), F32),
        ],
        compiler_params=pltpu.CompilerParams(
            dimension_semantics=("arbitrary",), vmem_limit_bytes=VMEM_LIMIT_BYTES),
        name="s5_layer",
    )(xin, _s5_pack_state(s0_re, s0_im), norm_w.reshape(1, D_MODEL), a, bbd, ccd,
      d.reshape(1, D_MODEL), w_glu)
    new_re, new_im = _s5_unpack_state(sfin)
    return out.reshape(x.shape), new_re, new_im


def _hgrn_gates(proj, lb):
    q = proj[:, 0:HG_FDIM]
    q = q * _sigmoid(q)
    fz = proj[:, HG_FDIM:2 * HG_FDIM]
    v = proj[:, 2 * HG_FDIM:2 * HG_FDIM + D_MODEL]
    g = proj[:, 2 * HG_FDIM + D_MODEL:]
    log_sig = jnp.minimum(fz, 0.0) - jnp.log1p(jnp.exp(-jnp.abs(fz)))
    t1 = jnp.log(lb)
    t2 = jnp.log1p(-lb) + log_sig
    hi = jnp.maximum(t1, t2)
    lo = jnp.minimum(t1, t2)
    logf = hi + jnp.log1p(jnp.exp(lo - hi))
    k = (1.0 - lb) * _sigmoid(-fz)
    return q, k, v, g, logf


def _hgrn_tile(q, k, v, g, logf, st_refs, nw, *, chunk):
    rows = q.shape[0]
    nseg = rows // chunk
    ri = lax.broadcasted_iota(jnp.int32, (rows, rows), 0)
    ci = lax.broadcasted_iota(jnp.int32, (rows, rows), 1)
    if nseg == 1:
        same = ci <= ri
    else:
        same = (ci <= ri) & ((ri // chunk) == (ci // chunk))
    tri = jnp.where(same, 1.0, 0.0).astype(F32)
    b = jnp.dot(tri, logf, precision=lax.Precision.HIGHEST, preferred_element_type=F32)
    rowid = lax.broadcasted_iota(jnp.int32, (rows, 1), 0)
    b_mid = b[chunk // 2:chunk // 2 + 1, :]
    b_last = b[chunk - 1:chunk, :]
    for s in range(1, nseg):
        in_s = rowid >= s * chunk
        b_mid = jnp.where(in_s, b[s * chunk + chunk // 2:s * chunk + chunk // 2 + 1, :], b_mid)
        b_last = jnp.where(in_s, b[s * chunk + chunk - 1:s * chunk + chunk, :], b_last)
    qs = (q * jnp.exp(b - b_mid)).astype(BF16)
    ks = (k * jnp.exp(b_mid - b)).astype(BF16)
    qi = (q * jnp.exp(b)).astype(BF16)
    kd = k * jnp.exp(b_last - b)
    vb = v.astype(BF16)
    outs = []
    for hd in range(HG_HEADS):
        sl = slice(hd * HG_DK, (hd + 1) * HG_DK)
        scores = jnp.where(same, _dot_nt(qs[:, sl], ks[:, sl]), 0.0)
        o = _dot(scores.astype(BF16), vb[:, sl])
        v_t = v[:, sl].T
        for s in range(nseg):
            st = st_refs[s][hd]
            o_s = _dot_nt(qi[:, sl], st.astype(BF16))
            kd_s = kd[:, sl]
            if nseg > 1:
                in_s = (rowid >= s * chunk) & (rowid < (s + 1) * chunk)
                o_s = jnp.where(in_s, o_s, 0.0)
                kd_s = jnp.where(in_s, kd_s, 0.0)
            o = o + o_s
            dec = jnp.exp(b[s * chunk + chunk - 1:s * chunk + chunk, sl])
            st_refs[s][hd] = dec * st + _dot(v_t.astype(BF16), kd_s.astype(BF16))
        o = o * lax.rsqrt(jnp.mean(o * o, axis=-1, keepdims=True) + RMS_EPS)
        gh = g[:, sl]
        outs.append(o * nw * (gh * _sigmoid(gh)))
    return jnp.concatenate(outs, axis=-1)


def _hgrn_prompt_kernel(x_ref, s0_ref, nw_ref, win_ref, lb_ref, hnw_ref, wout_ref,
                        out_ref, sout_ref, st_ref, y_ref, *, lc, chunk):
    c = pl.program_id(1)

    @pl.when(c == 0)
    def _():
        for hd in range(HG_HEADS):
            st_ref[hd] = s0_ref[0, hd].T

    x = x_ref[0]
    hb = _rmsnorm(x, nw_ref[...]).astype(BF16)
    proj = _dot(hb, win_ref[...])
    q, k, v, g, logf = _hgrn_gates(proj, lb_ref[...])
    for sub in range(lc // chunk):
        r = slice(sub * chunk, (sub + 1) * chunk)
        y_ref[r, :] = _hgrn_tile(q[r], k[r], v[r], g[r], logf[r], [st_ref], hnw_ref[...], chunk=chunk)
    out_ref[0] = x + _dot(y_ref[...].astype(BF16), wout_ref[...])

    @pl.when(c == pl.num_programs(1) - 1)
    def _():
        for hd in range(HG_HEADS):
            sout_ref[0, hd] = st_ref[hd].T


def _hgrn_sample_kernel(x_ref, s0_ref, nw_ref, win_ref, lb_ref, hnw_ref, wout_ref,
                        out_ref, sout_ref, st_ref, proj_ref, y_ref, *, seq):
    p = pl.program_id(0)
    nseg = SUBLANES // seq

    @pl.when(p == 0)
    def _():
        hb = _rmsnorm(x_ref[...], nw_ref[...]).astype(BF16)
        proj_ref[...] = _dot(hb, win_ref[...])

    for s in range(nseg):
        for hd in range(HG_HEADS):
            st_ref[s, hd] = s0_ref[s, hd].T
    r = pl.ds(pl.multiple_of(p * SUBLANES, SUBLANES), SUBLANES)
    q, k, v, g, logf = _hgrn_gates(proj_ref[r, :], lb_ref[...])
    y_ref[r, :] = _hgrn_tile(q, k, v, g, logf, [st_ref.at[s] for s in range(nseg)], hnw_ref[...],
                             chunk=seq)
    for s in range(nseg):
        for hd in range(HG_HEADS):
            sout_ref[s, hd] = st_ref[s, hd].T

    @pl.when(p == pl.num_programs(0) - 1)
    def _():
        out_ref[...] = x_ref[...] + _dot(y_ref[...].astype(BF16), wout_ref[...])


def _hgrn_layer(x, s0, norm_w, w_in, lb, hg_norm_w, w_out):
    nb, seq, _ = x.shape
    weights = (norm_w.reshape(1, D_MODEL), w_in.astype(BF16), lb.reshape(1, HG_FDIM),
               hg_norm_w.reshape(1, HG_DV), w_out.astype(BF16))
    w_specs = [
        _const_spec((1, D_MODEL)),
        _const_spec((D_MODEL, 2 * HG_FDIM + 2 * D_MODEL)),
        _const_spec((1, HG_FDIM)),
        _const_spec((1, HG_DV)),
        _const_spec((D_MODEL, D_MODEL)),
    ]
    s_shape = jax.ShapeDtypeStruct(s0.shape, F32)
    if seq >= HG_CHUNK:
        chunk = HG_CHUNK
        lc = 4 * chunk
        x_spec = pl.BlockSpec((1, lc, D_MODEL), lambda b, c: (b, c, 0))
        s_spec = pl.BlockSpec((1, HG_HEADS, HG_DK, HG_DV), lambda b, c: (b, 0, 0, 0))
        out, s_new = pl.pallas_call(
            functools.partial(_hgrn_prompt_kernel, lc=lc, chunk=chunk),
            grid=(nb, seq // lc),
            in_specs=[x_spec, s_spec] + w_specs,
            out_specs=(x_spec, s_spec),
            out_shape=(jax.ShapeDtypeStruct(x.shape, F32), s_shape),
            scratch_shapes=[
                pltpu.VMEM((HG_HEADS, HG_DV, HG_DK), F32),
                pltpu.VMEM((lc, D_MODEL), F32),
            ],
            compiler_params=pltpu.CompilerParams(
                dimension_semantics=("arbitrary", "arbitrary"), vmem_limit_bytes=VMEM_LIMIT_BYTES),
            name="hgrn_prompt",
        )(x, s0, *weights)
        return out, s_new
    assert SUBLANES % seq == 0 and nb % (SUBLANES // seq) == 0
    nseg = SUBLANES // seq
    rows = nb * seq
    x_spec = _const_spec((rows, D_MODEL))
    s_spec = pl.BlockSpec((nseg, HG_HEADS, HG_DK, HG_DV), lambda p: (p, 0, 0, 0))
    out, s_new = pl.pallas_call(
        functools.partial(_hgrn_sample_kernel, seq=seq),
        grid=(nb // nseg,),
        in_specs=[x_spec, s_spec] + w_specs,
        out_specs=(pl.BlockSpec((rows, D_MODEL), lambda p: (0, 0)), s_spec),
        out_shape=(jax.ShapeDtypeStruct((rows, D_MODEL), F32), s_shape),
        scratch_shapes=[
            pltpu.VMEM((nseg, HG_HEADS, HG_DV, HG_DK), F32),
            pltpu.VMEM((rows, 2 * HG_FDIM + 2 * D_MODEL), F32),
            pltpu.VMEM((rows, D_MODEL), F32),
        ],
        compiler_params=pltpu.CompilerParams(
            dimension_semantics=("arbitrary",), vmem_limit_bytes=VMEM_LIMIT_BYTES),
        name="hgrn_sample",
    )(x.reshape(rows, D_MODEL), s0, *weights)
    return out.reshape(x.shape), s_new


def _compare_exchange(v, i, j):
    hi = jnp.maximum(v[i], v[j])
    lo = jnp.minimum(v[i], v[j])
    v[i] = hi
    v[j] = lo


def _bitonic_merge_desc(v):
    n = len(v)
    j = n // 2
    while j >= 1:
        for i in range(n):
            if i & j == 0:
                _compare_exchange(v, i, i + j)
        j //= 2


def _sort_desc(v):
    n = len(v)
    k = 2
    while k <= n:
        j = k // 2
        while j >= 1:
            for i in range(n):
                l = i ^ j
                if l > i:
                    if i & k == 0:
                        _compare_exchange(v, i, l)
                    else:
                        _compare_exchange(v, l, i)
            j //= 2
        k *= 2
    return v


def _merge_top(a, b):
    n = len(a)
    v = [jnp.maximum(a[r], b[n - 1 - r]) for r in range(n)]
    _bitonic_merge_desc(v)
    return v


def _top16_rows(s):
    v = [s[SUBLANES * j:SUBLANES * (j + 1), :] for j in range(PEER_NKEYS // SUBLANES)]
    v = _sort_desc(v)
    for shift in (1, 2, 4):
        v = _merge_top(v, [pltpu.roll(x, shift, axis=0) for x in v])
    return v


def _candidate_top(a, b):
    top = [a[0] + b[j] for j in range(PEER_TOPK)]
    for i in range(1, PEER_TOPK):
        top = _merge_top(top, [a[i] + b[j] for j in range(PEER_TOPK)])
    return top


def _peer_select_kernel(x_ref, nw_ref, wq_ref, k1_ref, k2_ref,
                        hb_ref, c1_ref, e1_ref, r2_ref, e2_ref, t1_ref, t2_ref, s1_ref):
    hb = _rmsnorm(x_ref[...], nw_ref[...]).astype(BF16)
    hb_ref[...] = hb
    qb = _dot(hb, wq_ref[...]).astype(BF16)
    k1 = k1_ref[...]
    k2 = k2_ref[...]
    log2e = 1.0 / math.log(2.0)
    for hd in range(PEER_HEADS):
        base = hd * PEER_DKEY
        s1 = _dot_nt(k1, qb[:, base:base + PEER_HALF]) * log2e
        s2 = _dot_nt(k2, qb[:, base + PEER_HALF:base + PEER_DKEY]) * log2e
        s1_ref[hd] = s1
        top1 = _top16_rows(s1)
        top2 = _top16_rows(s2)
        rank = None
        for r in range(PEER_TOPK):
            t1_ref[r, hd:hd + 1, :] = top1[r][0:1, :]
            t2_ref[r, hd:hd + 1, :] = top2[r][0:1, :]
            above = jnp.where(top2[r][0:1, :] > s2, 1.0, 0.0)
            rank = above if rank is None else rank + above
        r2_ref[hd] = rank.astype(r2_ref.dtype)
        e2_ref[hd] = jnp.exp2(s2 - top2[0][0:1, :]).astype(e2_ref.dtype)
    a = [t1_ref[r] for r in range(PEER_TOPK)]
    b = [t2_ref[r] for r in range(PEER_TOPK)]
    top = _candidate_top(a, b)
    z = jnp.exp2(top[0] - top[0])
    for r in range(1, PEER_TOPK):
        z = z + jnp.exp2(top[r] - top[0])
    thr = top[PEER_TOPK - 1]
    scale = 0.5 / z
    for hd in range(PEER_HEADS):
        s1 = s1_ref[hd]
        row = slice(hd, hd + 1)
        cnt = None
        for r in range(PEER_TOPK):
            sel = jnp.where(s1 + t2_ref[r, row, :] >= thr[row, :], 1.0, 0.0)
            cnt = sel if cnt is None else cnt + sel
        c1_ref[hd] = jnp.where(s1 >= t1_ref[PEER_TOPK - 1, row, :], cnt, 0.0)
        e1_ref[hd] = jnp.exp2(s1 - t1_ref[0, row, :]) * scale[row, :]


PEER_ROW_GROUP = 4


def _peer_expert_kernel(hb_ref, x_ref, c1_ref, e1_ref, r2_ref, e2_ref, u_ref, vp_ref, v_ref, fw_ref,
                        out_ref, acc_ref, *tile_refs, tb, eb, tw, final_norm):
    e = pl.program_id(1)
    ntile = tb // tw
    ht_refs, at_refs = tile_refs[:ntile], tile_refs[ntile:]
    pack = 2 * SUBLANES
    npk = PEER_NKEYS // pack

    @pl.when(e == 0)
    def _():
        acc_ref[...] = jnp.zeros_like(acc_ref)
        for at_ref in at_refs:
            at_ref[...] = jnp.zeros_like(at_ref)

    u = u_ref[...]
    vp = vp_ref[...]
    for i in range(ntile):
        ht_refs[i][...] = _dot_nt(u, hb_ref[i * tw:(i + 1) * tw, :])
        acc_ref[:, i * tw:(i + 1) * tw] += _dot_tn(vp, at_refs[i][...])
    for i in range(ntile):
        for lt in range(tw // LANES):
            lcol = slice(lt * LANES, (lt + 1) * LANES)
            col = slice(i * tw + lt * LANES, i * tw + (lt + 1) * LANES)
            for r0 in range(0, eb // PEER_NKEYS, PEER_ROW_GROUP):
                rows = range(r0, r0 + PEER_ROW_GROUP)
                gates = {}
                for hd in range(PEER_HEADS):
                    cn = [jnp.broadcast_to(c1_ref[hd, r:r + 1, col], (pack, LANES)).astype(BF16)
                          for r in rows]
                    ew = [jnp.broadcast_to(e1_ref[hd, r:r + 1, col], (pack, LANES)).astype(BF16)
                          for r in rows]
                    for j in range(npk):
                        rk = r2_ref[hd, j * pack:(j + 1) * pack, col]
                        ev = e2_ref[hd, j * pack:(j + 1) * pack, col]
                        for k, r in enumerate(rows):
                            g = jnp.where(rk < cn[k], ev * ew[k], jnp.zeros_like(ev))
                            gates[r, j] = g if hd == 0 else gates[r, j] + g
                for r in rows:
                    for j in range(npk):
                        rr = slice(r * PEER_NKEYS + j * pack, r * PEER_NKEYS + (j + 1) * pack)
                        h = ht_refs[i][rr, lcol]
                        ge = (h * (1.0 + lax.erf(h * (1.0 / math.sqrt(2.0))))).astype(BF16)
                        at_refs[i][rr, lcol] = ge * gates[r, j]

    @pl.when(e == pl.num_programs(1) - 1)
    def _():
        v = v_ref[...]
        for i in range(ntile):
            acc_ref[:, i * tw:(i + 1) * tw] += _dot_tn(v, at_refs[i][...])
        y = x_ref[...] + acc_ref[...].T
        if final_norm:
            y = _rmsnorm(y, fw_ref[...])
        out_ref[...] = y


def _peer_layer(x, norm_w, w_q, keys1, keys2, u_tab, v_tab, final_w, *, final_norm):
    t_tot = x.shape[0]
    tbs = min(512, t_tot)
    tb = min(1024, t_tot)
    eb = 8 * PEER_NKEYS
    key_spec = pl.BlockSpec((PEER_HEADS, PEER_NKEYS, tbs), lambda i: (0, 0, i))
    key_f32 = jax.ShapeDtypeStruct((PEER_HEADS, PEER_NKEYS, t_tot), F32)
    key_bf16 = jax.ShapeDtypeStruct((PEER_HEADS, PEER_NKEYS, t_tot), BF16)
    hb, c1, e1, r2, e2 = pl.pallas_call(
        _peer_select_kernel,
        grid=(t_tot // tbs,),
        in_specs=[
            pl.BlockSpec((tbs, D_MODEL), lambda i: (i, 0)),
            _const_spec((1, D_MODEL)),
            _const_spec((D_MODEL, PEER_HEADS * PEER_DKEY)),
            _const_spec((PEER_NKEYS, PEER_HALF)),
            _const_spec((PEER_NKEYS, PEER_HALF)),
        ],
        out_specs=(pl.BlockSpec((tbs, D_MODEL), lambda i: (i, 0)), key_spec, key_spec, key_spec, key_spec),
        out_shape=(jax.ShapeDtypeStruct((t_tot, D_MODEL), BF16), key_f32, key_f32, key_bf16, key_bf16),
        scratch_shapes=[
            pltpu.VMEM((PEER_TOPK, PEER_HEADS, tbs), F32),
            pltpu.VMEM((PEER_TOPK, PEER_HEADS, tbs), F32),
            pltpu.VMEM((PEER_HEADS, PEER_NKEYS, tbs), F32),
        ],
        compiler_params=pltpu.CompilerParams(
            dimension_semantics=("arbitrary",), vmem_limit_bytes=VMEM_LIMIT_BYTES),
        name="peer_select",
    )(x, norm_w.reshape(1, D_MODEL), w_q.astype(BF16), keys1.astype(BF16), keys2.astype(BF16))

    tok = lambda i, e: (i, 0)
    once = pl.Buffered(1)
    tw = 2 * LANES
    out = pl.pallas_call(
        functools.partial(_peer_expert_kernel, tb=tb, eb=eb, tw=tw, final_norm=final_norm),
        grid=(t_tot // tb, PEER_EXPERTS // eb),
        in_specs=[
            pl.BlockSpec((tb, D_MODEL), tok, pipeline_mode=once),
            pl.BlockSpec((tb, D_MODEL), tok, pipeline_mode=once),
            pl.BlockSpec((PEER_HEADS, eb // PEER_NKEYS, tb), lambda i, e: (0, e, i)),
            pl.BlockSpec((PEER_HEADS, eb // PEER_NKEYS, tb), lambda i, e: (0, e, i)),
            pl.BlockSpec((PEER_HEADS, PEER_NKEYS, tb), lambda i, e: (0, 0, i), pipeline_mode=once),
            pl.BlockSpec((PEER_HEADS, PEER_NKEYS, tb), lambda i, e: (0, 0, i), pipeline_mode=once),
            pl.BlockSpec((eb, D_MODEL), lambda i, e: (e, 0)),
            pl.BlockSpec((eb, D_MODEL), lambda i, e: (jnp.maximum(e - 1, 0), 0)),
            pl.BlockSpec((eb, D_MODEL), lambda i, e: (PEER_EXPERTS // eb - 1, 0), pipeline_mode=once),
            _const_spec((1, D_MODEL)),
        ],
        out_specs=pl.BlockSpec((tb, D_MODEL), tok),
        out_shape=jax.ShapeDtypeStruct((t_tot, D_MODEL), F32),
        scratch_shapes=(
            [pltpu.VMEM((D_MODEL, tb), F32)]
            + [pltpu.VMEM((eb, tw), F32) for _ in range(tb // tw)]
            + [pltpu.VMEM((eb, tw), BF16) for _ in range(tb // tw)]
        ),
        compiler_params=pltpu.CompilerParams(
            dimension_semantics=("arbitrary", "arbitrary"), vmem_limit_bytes=VMEM_LIMIT_BYTES),
        name="peer_experts",
    )(hb, x, c1, e1, r2, e2, u_tab, v_tab, v_tab, final_w.reshape(1, D_MODEL))
    return out


def _trunk(x, s5_re, s5_im, s_hg, p, s5_prm, lb_all):
    nb, seq, _ = x.shape
    new_re, new_im, new_hg = [], [], []
    for i in range(DEPTH):
        j = i // N_MIXERS
        if i % N_MIXERS == 0:
            a, bbd, ccd = s5_prm[j]
            x, sr, si = _s5_layer(x, s5_re[j], s5_im[j], p["norm_mix"][i], a, bbd, ccd,
                                  p["s5_d"][j], p["s5_w_glu"][j].astype(BF16), lc=min(seq, 64))
            new_re.append(sr)
            new_im.append(si)
        else:
            x, s_new = _hgrn_layer(x, s_hg[j], p["norm_mix"][i], p["hg_w_in"][j], lb_all[i],
                                   p["hg_norm_w"][j], p["hg_w_out"][j])
            new_hg.append(s_new)
        x = _peer_layer(x.reshape(nb * seq, D_MODEL), p["norm_ffn"][i], p["peer_w_q"][i],
                        p["peer_keys1"][i], p["peer_keys2"][i], p["peer_u"][i], p["peer_v"][i],
                        p["norm_final"], final_norm=(i == DEPTH - 1)).reshape(nb, seq, D_MODEL)
    return x, jnp.stack(new_re), jnp.stack(new_im), jnp.stack(new_hg)


def kernel(x_prompt, x_sample, state_s5_re, state_s5_im, state_hgrn, norm_mix, norm_ffn, norm_final,
           s5_lambda_re, s5_lambda_im, s5_log_dt, s5_b_re, s5_b_im, s5_c_re, s5_c_im, s5_d, s5_w_glu,
           hg_w_in, hg_lower_bounds, hg_norm_w, hg_w_out, peer_w_q, peer_keys1, peer_keys2, peer_u, peer_v):
    p = dict(norm_mix=norm_mix, norm_ffn=norm_ffn, norm_final=norm_final, s5_d=s5_d, s5_w_glu=s5_w_glu,
             hg_w_in=hg_w_in, hg_norm_w=hg_norm_w, hg_w_out=hg_w_out, peer_w_q=peer_w_q,
             peer_keys1=peer_keys1, peer_keys2=peer_keys2,
             peer_u=[peer_u[i].astype(BF16) for i in range(DEPTH)],
             peer_v=[peer_v[i].astype(BF16) for i in range(DEPTH)])
    n_a = s5_lambda_re.shape[0]
    s5_prm = [_s5_params(s5_lambda_re[j], s5_lambda_im[j], s5_log_dt[j], s5_b_re[j], s5_b_im[j],
                         s5_c_re[j], s5_c_im[j]) for j in range(n_a)]
    lb_all = jnp.cumsum(jax.nn.softmax(hg_lower_bounds.astype(F32), axis=0), axis=0)
    lb_all = lb_all - lb_all[0:1]
    nbp = x_prompt.shape[0]
    z_s5 = jnp.zeros((n_a, nbp, S5_GROUPS, S5_STATE), state_s5_re.dtype)
    z_hg = jnp.zeros((state_hgrn.shape[0], nbp, HG_HEADS, HG_DK, HG_DV), state_hgrn.dtype)
    y_p, re_p, im_p, hg_p = _trunk(x_prompt, z_s5, z_s5, z_hg, p, s5_prm, lb_all)
    y_s, re_s, im_s, hg_s = _trunk(x_sample, state_s5_re, state_s5_im, state_hgrn, p, s5_prm, lb_all)
    return (y_p, y_s, re_p, im_p, hg_p, re_s, im_s, hg_s)
```

```python
import functools
import math

import jax
import jax.numpy as jnp
from jax import lax
from jax.experimental import pallas as pl
from jax.experimental.pallas import tpu as pltpu

F32 = jnp.float32
BF16 = jnp.bfloat16

D_MODEL = 1024
DEPTH = 2
N_MIXERS = 2
S5_GROUP = 16
S5_GROUPS = D_MODEL // S5_GROUP
S5_STATE = 64
HG_HEADS = 8
HG_DK = 128
HG_DV = 128
HG_FDIM = HG_HEADS * HG_DK
HG_CHUNK = 64
PEER_HEADS = 8
PEER_NKEYS = 128
PEER_EXPERTS = PEER_NKEYS * PEER_NKEYS
PEER_TOPK = 16
PEER_DKEY = 256
PEER_HALF = PEER_DKEY // 2
RMS_EPS = 1e-6

LANES = 128
SUBLANES = 8
VMEM_LIMIT_BYTES = 56 * 2**20

S5_CB = 4
S5_CB_GROUPS = S5_GROUPS // S5_CB
S5_CB_IN = S5_CB_GROUPS * S5_GROUP
S5_CB_STATE = S5_CB_GROUPS * S5_STATE


def _const_spec(shape):
    nd = len(shape)
    return pl.BlockSpec(shape, lambda *_: (0,) * nd, pipeline_mode=pl.Buffered(1))


def _rmsnorm(x, w):
    ms = jnp.mean(x * x, axis=-1, keepdims=True)
    return x * lax.rsqrt(ms + RMS_EPS) * w


def _gelu(x):
    return 0.5 * x * (1.0 + lax.erf(x * (1.0 / math.sqrt(2.0))))


def _sigmoid(x):
    return 1.0 / (1.0 + jnp.exp(-x))


def _dot(a, b):
    return jnp.dot(a, b, preferred_element_type=F32)


def _dot_nt(a, b):
    return lax.dot_general(a, b, (((1,), (1,)), ((), ())), preferred_element_type=F32)


def _dot_tn(a, b):
    return lax.dot_general(a, b, (((0,), (0,)), ((), ())), preferred_element_type=F32)


def _s5_discretize_kernel(lre_ref, lim_ref, ldt_ref, are_ref, aim_ref, zre_ref, zim_ref):
    lr = jnp.minimum(lre_ref[...], -1e-4)
    li = lim_ref[...]
    dt = jnp.exp(ldt_ref[...])
    mag = jnp.exp(lr * dt)
    ang = li * dt
    ab_re = mag * jnp.cos(ang)
    ab_im = mag * jnp.sin(ang)
    den = lr * lr + li * li
    nr = ab_re - 1.0
    are_ref[...] = ab_re
    aim_ref[...] = ab_im
    zre_ref[...] = (nr * lr + ab_im * li) / den
    zim_ref[...] = (ab_im * lr - nr * li) / den


def _s5_scale_b_kernel(zre_ref, zim_ref, bre_ref, bim_ref, ore_ref, oim_ref):
    zr = zre_ref[...]
    zi = zim_ref[...]
    br = bre_ref[...]
    bi = bim_ref[...]
    ore_ref[...] = zr * br - zi * bi
    oim_ref[...] = zr * bi + zi * br


def _s5_params(lam_re, lam_im, log_dt, b_re, b_im, c_re, c_im):
    g, p, hh = S5_GROUPS, S5_STATE, S5_GROUP
    gp = jax.ShapeDtypeStruct((g, p), F32)
    a_re, a_im, z_re, z_im = pl.pallas_call(
        _s5_discretize_kernel, out_shape=(gp, gp, gp, gp), name="s5_discretize",
    )(lam_re.astype(F32), lam_im.astype(F32), log_dt.astype(F32).reshape(g, 1))
    gph = jax.ShapeDtypeStruct((g * p, hh), F32)
    bb_re, bb_im = pl.pallas_call(
        _s5_scale_b_kernel, out_shape=(gph, gph), name="s5_scale_b",
    )(z_re.reshape(g * p, 1), z_im.reshape(g * p, 1),
      b_re.astype(F32).reshape(g * p, hh), b_im.astype(F32).reshape(g * p, hh))
    eye = jnp.eye(S5_CB_GROUPS, dtype=F32)

    def blockdiag_in(w):
        w4 = w.reshape(S5_CB, S5_CB_GROUPS, p, hh)
        return jnp.einsum('cgph,gk->cghkp', w4, eye).reshape(S5_CB, S5_CB_IN, S5_CB_STATE)

    def blockdiag_out(w):
        w4 = w.astype(F32).reshape(S5_CB, S5_CB_GROUPS, hh, p)
        return jnp.einsum('cghp,gk->cgpkh', w4, eye).reshape(S5_CB, S5_CB_STATE, S5_CB_IN)

    bbd = jnp.concatenate([blockdiag_in(bb_re), blockdiag_in(bb_im)], axis=-1).astype(BF16)
    ccd = jnp.concatenate([blockdiag_out(c_re), -blockdiag_out(c_im)], axis=1).astype(BF16)
    a = jnp.stack([a_re.reshape(S5_CB, S5_CB_STATE), a_im.reshape(S5_CB, S5_CB_STATE)], axis=1)
    return a, bbd, ccd


def _s5_pack_state(s_re, s_im):
    nb = s_re.shape[0]
    return jnp.stack([s_re.reshape(nb, S5_CB, S5_CB_STATE), s_im.reshape(nb, S5_CB, S5_CB_STATE)],
                     axis=2).reshape(nb, 2 * S5_GROUPS * S5_STATE)


def _s5_unpack_state(s):
    nb = s.shape[0]
    s4 = s.reshape(nb, S5_CB, 2, S5_CB_STATE)
    return (s4[:, :, 0].reshape(nb, S5_GROUPS, S5_STATE), s4[:, :, 1].reshape(nb, S5_GROUPS, S5_STATE))


def _s5_kernel(x_ref, s0_ref, nw_ref, a_ref, bbd_ref, ccd_ref, d_ref, wglu_ref,
               out_ref, sfin_ref, bu_ref, st_ref, hs_ref, ys_ref, *, nb, lc):
    rows = nb * lc
    nk = D_MODEL // LANES
    lane = lambda k: slice(k * LANES, (k + 1) * LANES)

    @pl.when(pl.program_id(0) == 0)
    def _():
        st_ref[...] = s0_ref[...]

    x = x_ref[...].reshape(rows, D_MODEL)
    h = _rmsnorm(x, nw_ref[...])
    if nb <= lc:
        for b in range(nb):
            for k in range(nk):
                hs_ref[k, pl.ds(b, lc, stride=nb), :] = h[b * lc:(b + 1) * lc, lane(k)]
        hb = jnp.concatenate([hs_ref[k] for k in range(nk)], axis=-1).astype(BF16)
    else:
        for k in range(nk):
            hs_ref[k] = h[:, lane(k)]
        hb = jnp.concatenate(
            [jnp.concatenate([hs_ref[k, pl.ds(t, nb, stride=lc), :] for t in range(lc)], axis=0)
             for k in range(nk)], axis=-1).astype(BF16)
    nt = S5_CB_STATE // LANES
    for cb in range(S5_CB):
        bu = _dot(hb[:, cb * S5_CB_IN:(cb + 1) * S5_CB_IN], bbd_ref[cb])
        for k in range(2 * nt):
            bu_ref[k] = bu[:, lane(k)]
        a_re = [jnp.broadcast_to(a_ref[cb, 0:1, lane(k)], (SUBLANES, LANES)) for k in range(nt)]
        a_im = [jnp.broadcast_to(a_ref[cb, 1:2, lane(k)], (SUBLANES, LANES)) for k in range(nt)]
        col = cb * 2 * S5_CB_STATE
        for j in range(nb // SUBLANES):
            brow = j * SUBLANES

            def step(t, carry, brow=brow):
                idx = pl.ds(pl.multiple_of(t * nb + brow, SUBLANES), SUBLANES)
                new_re, new_im = [], []
                for k in range(nt):
                    s_re, s_im = carry[k], carry[nt + k]
                    n_re = a_re[k] * s_re - a_im[k] * s_im + bu_ref[k, idx, :]
                    n_im = a_re[k] * s_im + a_im[k] * s_re + bu_ref[nt + k, idx, :]
                    bu_ref[k, idx, :] = n_re
                    bu_ref[nt + k, idx, :] = n_im
                    new_re.append(n_re)
                    new_im.append(n_im)
                return tuple(new_re) + tuple(new_im)

            carry = tuple(st_ref[brow:brow + SUBLANES, col + k * LANES:col + (k + 1) * LANES]
                          for k in range(2 * nt))
            if lc <= 8:
                for t in range(lc):
                    carry = step(t, carry)
            else:
                carry = lax.fori_loop(0, lc, step, carry, unroll=4)
            for k in range(2 * nt):
                st_ref[brow:brow + SUBLANES, col + k * LANES:col + (k + 1) * LANES] = carry[k]
        st_all = jnp.concatenate([bu_ref[k] for k in range(2 * nt)], axis=-1)
        y_cb = _dot(st_all.astype(BF16), ccd_ref[cb])
        for k in range(S5_CB_IN // LANES):
            ys_ref[cb * (S5_CB_IN // LANES) + k] = y_cb[:, lane(k)]
    if nb <= lc:
        y = jnp.concatenate(
            [jnp.concatenate([ys_ref[k, pl.ds(b, lc, stride=nb), :] for b in range(nb)], axis=0)
             for k in range(nk)], axis=-1)
    else:
        for t in range(lc):
            for k in range(nk):
                hs_ref[k, pl.ds(t, nb, stride=lc), :] = ys_ref[k, t * nb:(t + 1) * nb, :]
        y = jnp.concatenate([hs_ref[k] for k in range(nk)], axis=-1)
    y = _gelu(y + d_ref[...] * h)
    z = _dot(y.astype(BF16), wglu_ref[...])
    out = x + z[:, :D_MODEL] * _sigmoid(z[:, D_MODEL:])
    out_ref[...] = out.reshape(out_ref.shape)
    sfin_ref[...] = st_ref[...]


def _s5_layer(x, s0_re, s0_im, norm_w, a, bbd, ccd, d, w_glu, *, lc):
    nb, seq, _ = x.shape
    nstate = 2 * S5_GROUPS * S5_STATE
    rows = nb * lc
    if seq == lc:
        xin = x.reshape(rows, D_MODEL)
        x_spec = pl.BlockSpec((rows, D_MODEL), lambda c: (0, 0))
    else:
        xin = x
        x_spec = pl.BlockSpec((nb, lc, D_MODEL), lambda c: (0, c, 0))
    out, sfin = pl.pallas_call(
        functools.partial(_s5_kernel, nb=nb, lc=lc),
        grid=(seq // lc,),
        in_specs=[
            x_spec,
            _const_spec((nb, nstate)),
            _const_spec((1, D_MODEL)),
            _const_spec((S5_CB, 2, S5_CB_STATE)),
            _const_spec((S5_CB, S5_CB_IN, 2 * S5_CB_STATE)),
            _const_spec((S5_CB, 2 * S5_CB_STATE, S5_CB_IN)),
            _const_spec((1, D_MODEL)),
            _const_spec((D_MODEL, 2 * D_MODEL)),
        ],
        out_specs=(x_spec, pl.BlockSpec((nb, nstate), lambda c: (0, 0))),
        out_shape=(jax.ShapeDtypeStruct(xin.shape, F32), jax.ShapeDtypeStruct((nb, nstate), F32)),
        scratch_shapes=[
            pltpu.VMEM((2 * S5_CB_STATE // LANES, rows, LANES), F32),
            pltpu.VMEM((nb, nstate), F32),
            pltpu.VMEM((D_MODEL // LANES, rows, LANES), F32),
            pltpu.VMEM((D_MODEL // LANES, rows, LANES), F32),
        ],
        compiler_params=pltpu.CompilerParams(
            dimension_semantics=("arbitrary",), vmem_limit_bytes=VMEM_LIMIT_BYTES),
        name="s5_layer",
    )(xin, _s5_pack_state(s0_re, s0_im), norm_w.reshape(1, D_MODEL), a, bbd, ccd,
      d.reshape(1, D_MODEL), w_glu)
    new_re, new_im = _s5_unpack_state(sfin)
    return out.reshape(x.shape), new_re, new_im


def _hgrn_gates(proj, lb):
    q = proj[:, 0:HG_FDIM]
    q = q * _sigmoid(q)
    fz = proj[:, HG_FDIM:2 * HG_FDIM]
    v = proj[:, 2 * HG_FDIM:2 * HG_FDIM + D_MODEL]
    g = proj[:, 2 * HG_FDIM + D_MODEL:]
    log_sig = jnp.minimum(fz, 0.0) - jnp.log1p(jnp.exp(-jnp.abs(fz)))
    t1 = jnp.log(lb)
    t2 = jnp.log1p(-lb) + log_sig
    hi = jnp.maximum(t1, t2)
    lo = jnp.minimum(t1, t2)
    logf = hi + jnp.log1p(jnp.exp(lo - hi))
    k = (1.0 - lb) * _sigmoid(-fz)
    return q, k, v, g, logf


def _hgrn_tile(q, k, v, g, logf, st_refs, nw, *, chunk):
    rows = q.shape[0]
    nseg = rows // chunk
    ri = lax.broadcasted_iota(jnp.int32, (rows, rows), 0)
    ci = lax.broadcasted_iota(jnp.int32, (rows, rows), 1)
    if nseg == 1:
        same = ci <= ri
    else:
        same = (ci <= ri) & ((ri // chunk) == (ci // chunk))
    tri = jnp.where(same, 1.0, 0.0).astype(F32)
    b = jnp.dot(tri, logf, precision=lax.Precision.HIGHEST, preferred_element_type=F32)
    rowid = lax.broadcasted_iota(jnp.int32, (rows, 1), 0)
    b_mid = b[chunk // 2:chunk // 2 + 1, :]
    b_last = b[chunk - 1:chunk, :]
    for s in range(1, nseg):
        in_s = rowid >= s * chunk
        b_mid = jnp.where(in_s, b[s * chunk + chunk // 2:s * chunk + chunk // 2 + 1, :], b_mid)
        b_last = jnp.where(in_s, b[s * chunk + chunk - 1:s * chunk + chunk, :], b_last)
    qs = (q * jnp.exp(b - b_mid)).astype(BF16)
    ks = (k * jnp.exp(b_mid - b)).astype(BF16)
    qi = (q * jnp.exp(b)).astype(BF16)
    kd = k * jnp.exp(b_last - b)
    vb = v.astype(BF16)
    outs = []
    for hd in range(HG_HEADS):
        sl = slice(hd * HG_DK, (hd + 1) * HG_DK)
        scores = jnp.where(same, _dot_nt(qs[:, sl], ks[:, sl]), 0.0)
        o = _dot(scores.astype(BF16), vb[:, sl])
        v_t = v[:, sl].T
        for s in range(nseg):
            st = st_refs[s][hd]
            o_s = _dot_nt(qi[:, sl], st.astype(BF16))
            kd_s = kd[:, sl]
            if nseg > 1:
                in_s = (rowid >= s * chunk) & (rowid < (s + 1) * chunk)
                o_s = jnp.where(in_s, o_s, 0.0)
                kd_s = jnp.where(in_s, kd_s, 0.0)
            o = o + o_s
            dec = jnp.exp(b[s * chunk + chunk - 1:s * chunk + chunk, sl])
            st_refs[s][hd] = dec * st + _dot(v_t.astype(BF16), kd_s.astype(BF16))
        o = o * lax.rsqrt(jnp.mean(o * o, axis=-1, keepdims=True) + RMS_EPS)
        gh = g[:, sl]
        outs.append(o * nw * (gh * _sigmoid(gh)))
    return jnp.concatenate(outs, axis=-1)


def _hgrn_prompt_kernel(x_ref, s0_ref, nw_ref, win_ref, lb_ref, hnw_ref, wout_ref,
                        out_ref, sout_ref, st_ref, y_ref, *, lc, chunk):
    c = pl.program_id(1)

    @pl.when(c == 0)
    def _():
        for hd in range(HG_HEADS):
            st_ref[hd] = s0_ref[0, hd].T

    x = x_ref[0]
    hb = _rmsnorm(x, nw_ref[...]).astype(BF16)
    proj = _dot(hb, win_ref[...])
    q, k, v, g, logf = _hgrn_gates(proj, lb_ref[...])
    for sub in range(lc // chunk):
        r = slice(sub * chunk, (sub + 1) * chunk)
        y_ref[r, :] = _hgrn_tile(q[r], k[r], v[r], g[r], logf[r], [st_ref], hnw_ref[...], chunk=chunk)
    out_ref[0] = x + _dot(y_ref[...].astype(BF16), wout_ref[...])

    @pl.when(c == pl.num_programs(1) - 1)
    def _():
        for hd in range(HG_HEADS):
            sout_ref[0, hd] = st_ref[hd].T


def _hgrn_sample_kernel(x_ref, s0_ref, nw_ref, win_ref, lb_ref, hnw_ref, wout_ref,
                        out_ref, sout_ref, st_ref, proj_ref, y_ref, *, seq):
    p = pl.program_id(0)
    nseg = SUBLANES // seq

    @pl.when(p == 0)
    def _():
        hb = _rmsnorm(x_ref[...], nw_ref[...]).astype(BF16)
        proj_ref[...] = _dot(hb, win_ref[...])

    for s in range(nseg):
        for hd in range(HG_HEADS):
            st_ref[s, hd] = s0_ref[s, hd].T
    r = pl.ds(pl.multiple_of(p * SUBLANES, SUBLANES), SUBLANES)
    q, k, v, g, logf = _hgrn_gates(proj_ref[r, :], lb_ref[...])
    y_ref[r, :] = _hgrn_tile(q, k, v, g, logf, [st_ref.at[s] for s in range(nseg)], hnw_ref[...],
                             chunk=seq)
    for s in range(nseg):
        for hd in range(HG_HEADS):
            sout_ref[s, hd] = st_ref[s, hd].T

    @pl.when(p == pl.num_programs(0) - 1)
    def _():
        out_ref[...] = x_ref[...] + _dot(y_ref[...].astype(BF16), wout_ref[...])


def _hgrn_layer(x, s0, norm_w, w_in, lb, hg_norm_w, w_out):
    nb, seq, _ = x.shape
    weights = (norm_w.reshape(1, D_MODEL), w_in.astype(BF16), lb.reshape(1, HG_FDIM),
               hg_norm_w.reshape(1, HG_DV), w_out.astype(BF16))
    w_specs = [
        _const_spec((1, D_MODEL)),
        _const_spec((D_MODEL, 2 * HG_FDIM + 2 * D_MODEL)),
        _const_spec((1, HG_FDIM)),
        _const_spec((1, HG_DV)),
        _const_spec((D_MODEL, D_MODEL)),
    ]
    s_shape = jax.ShapeDtypeStruct(s0.shape, F32)
    if seq >= HG_CHUNK:
        chunk = HG_CHUNK
        lc = 4 * chunk
        x_spec = pl.BlockSpec((1, lc, D_MODEL), lambda b, c: (b, c, 0))
        s_spec = pl.BlockSpec((1, HG_HEADS, HG_DK, HG_DV), lambda b, c: (b, 0, 0, 0))
        out, s_new = pl.pallas_call(
            functools.partial(_hgrn_prompt_kernel, lc=lc, chunk=chunk),
            grid=(nb, seq // lc),
            in_specs=[x_spec, s_spec] + w_specs,
            out_specs=(x_spec, s_spec),
            out_shape=(jax.ShapeDtypeStruct(x.shape, F32), s_shape),
            scratch_shapes=[
                pltpu.VMEM((HG_HEADS, HG_DV, HG_DK), F32),
                pltpu.VMEM((lc, D_MODEL), F32),
            ],
            compiler_params=pltpu.CompilerParams(
                dimension_semantics=("arbitrary", "arbitrary"), vmem_limit_bytes=VMEM_LIMIT_BYTES),
            name="hgrn_prompt",
        )(x, s0, *weights)
        return out, s_new
    assert SUBLANES % seq == 0 and nb % (SUBLANES // seq) == 0
    nseg = SUBLANES // seq
    rows = nb * seq
    x_spec = _const_spec((rows, D_MODEL))
    s_spec = pl.BlockSpec((nseg, HG_HEADS, HG_DK, HG_DV), lambda p: (p, 0, 0, 0))
    out, s_new = pl.pallas_call(
        functools.partial(_hgrn_sample_kernel, seq=seq),
        grid=(nb // nseg,),
        in_specs=[x_spec, s_spec] + w_specs,
        out_specs=(pl.BlockSpec((rows, D_MODEL), lambda p: (0, 0)), s_spec),
        out_shape=(jax.ShapeDtypeStruct((rows, D_MODEL), F32), s_shape),
        scratch_shapes=[
            pltpu.VMEM((nseg, HG_HEADS, HG_DV, HG_DK), F32),
            pltpu.VMEM((rows, 2 * HG_FDIM + 2 * D_MODEL), F32),
            pltpu.VMEM((rows, D_MODEL), F32),
        ],
        compiler_params=pltpu.CompilerParams(
            dimension_semantics=("arbitrary",), vmem_limit_bytes=VMEM_LIMIT_BYTES),
        name="hgrn_sample",
    )(x.reshape(rows, D_MODEL), s0, *weights)
    return out.reshape(x.shape), s_new


def _compare_exchange(v, i, j):
    hi = jnp.maximum(v[i], v[j])
    lo = jnp.minimum(v[i], v[j])
    v[i] = hi
    v[j] = lo


def _bitonic_merge_desc(v):
    n = len(v)
    j = n // 2
    while j >= 1:
        for i in range(n):
            if i & j == 0:
                _compare_exchange(v, i, i + j)
        j //= 2


def _sort_desc(v):
    n = len(v)
    k = 2
    while k <= n:
        j = k // 2
        while j >= 1:
            for i in range(n):
                l = i ^ j
                if l > i:
                    if i & k == 0:
                        _compare_exchange(v, i, l)
                    else:
                        _compare_exchange(v, l, i)
            j //= 2
        k *= 2
    return v


def _merge_top(a, b):
    n = len(a)
    v = [jnp.maximum(a[r], b[n - 1 - r]) for r in range(n)]
    _bitonic_merge_desc(v)
    return v


def _top16_rows(s):
    v = [s[SUBLANES * j:SUBLANES * (j + 1), :] for j in range(PEER_NKEYS // SUBLANES)]
    v = _sort_desc(v)
    for shift in (1, 2, 4):
        v = _merge_top(v, [pltpu.roll(x, shift, axis=0) for x in v])
    return v


def _candidate_top(a, b):
    n = PEER_TOPK
    low = jnp.full(a[0].shape, -jnp.inf, F32)
    runs = [
        [a[i] + b[0] for i in range(1, n)],
        [a[1] + b[j] for j in range(1, 8)],
        [a[2] + b[j] for j in range(1, 5)],
        [a[3] + b[j] for j in range(1, 4)],
        [a[4] + b[j] for j in range(1, 3)],
        [a[i] + b[1] for i in range(5, 8)],
    ]
    assert all((i + 1) * (j + 1) > n for i in range(1, n) for j in range(1, n)
               if not ((i == 1 and j < 8) or (i == 2 and j < 5) or (i == 3 and j < 4)
                       or (i == 4 and j < 3) or (5 <= i < 8 and j == 1)))
    top = [a[0] + b[j] for j in range(n)]
    for run in runs:
        top = _merge_top(top, run + [low] * (n - len(run)))
    return top


def _peer_select_kernel(x_ref, nw_ref, wq_ref, k1_ref, k2_ref,
                        hb_ref, s1_ref, s2_ref, thr_ref, t1_ref, t2_ref):
    hb = _rmsnorm(x_ref[...], nw_ref[...]).astype(BF16)
    hb_ref[...] = hb
    qb = _dot(hb, wq_ref[...]).astype(BF16)
    k1 = k1_ref[...]
    k2 = k2_ref[...]
    log2e = 1.0 / math.log(2.0)
    for hd in range(PEER_HEADS):
        base = hd * PEER_DKEY
        s1 = _dot_nt(k1, qb[:, base:base + PEER_HALF]) * log2e
        s2 = _dot_nt(k2, qb[:, base + PEER_HALF:base + PEER_DKEY]) * log2e
        s1_ref[hd] = s1
        s2_ref[hd] = s2
        for s, t_ref in ((s1, t1_ref), (s2, t2_ref)):
            top = _top16_rows(s)
            for r in range(PEER_TOPK):
                t_ref[r, hd:hd + 1, :] = top[r][0:1, :]
    a = [t1_ref[r] for r in range(PEER_TOPK)]
    b = [t2_ref[r] for r in range(PEER_TOPK)]
    top = _candidate_top(a, b)
    z = jnp.exp2(top[0] - top[0])
    for r in range(1, PEER_TOPK):
        z = z + jnp.exp2(top[r] - top[0])
    shift = top[0] + jnp.log2(z) + 1.0
    for hd in range(PEER_HEADS):
        s1_ref[hd] = s1_ref[hd] - shift[hd:hd + 1, :]
    thr_ref[...] = _candidate_top([v - shift for v in a], b)[PEER_TOPK - 1]


def _peer_expert_kernel(hb_ref, x_ref, s1_ref, s2_ref, thr_ref, u_ref, vp_ref, v_ref, fw_ref,
                        out_ref, acc_ref, *tile_refs, tb, eb, tw, final_norm):
    e = pl.program_id(1)
    ntile = tb // tw
    ht_refs, at_refs = tile_refs[:ntile], tile_refs[ntile:]

    @pl.when(e == 0)
    def _():
        acc_ref[...] = jnp.zeros_like(acc_ref)
        for at_ref in at_refs:
            at_ref[...] = jnp.zeros_like(at_ref)

    u = u_ref[...]
    vp = vp_ref[...]
    for i in range(ntile):
        ht_refs[i][...] = _dot_nt(u, hb_ref[i * tw:(i + 1) * tw, :])
        acc_ref[:, i * tw:(i + 1) * tw] += _dot_tn(vp, at_refs[i][...])
    for i in range(ntile):
        for lt in range(tw // LANES):
            lcol = slice(lt * LANES, (lt + 1) * LANES)
            col = slice(i * tw + lt * LANES, i * tw + (lt + 1) * LANES)
            for r in range(eb // PEER_NKEYS):
                gate = None
                for hd in range(PEER_HEADS):
                    t = s2_ref[hd, :, col] + s1_ref[hd, r:r + 1, col]
                    g = jnp.where(t >= thr_ref[hd:hd + 1, col], jnp.exp2(t), 0.0)
                    gate = g if gate is None else gate + g
                rr = slice(r * PEER_NKEYS, (r + 1) * PEER_NKEYS)
                h = ht_refs[i][rr, lcol]
                at_refs[i][rr, lcol] = (h * (1.0 + lax.erf(h * (1.0 / math.sqrt(2.0)))) * gate).astype(BF16)

    @pl.when(e == pl.num_programs(1) - 1)
    def _():
        v = v_ref[...]
        for i in range(ntile):
            acc_ref[:, i * tw:(i + 1) * tw] += _dot_tn(v, at_refs[i][...])
        y = x_ref[...] + acc_ref[...].T
        if final_norm:
            y = _rmsnorm(y, fw_ref[...])
        out_ref[...] = y


def _peer_layer(x, norm_w, w_q, keys1, keys2, u_tab, v_tab, final_w, *, layer, final_norm):
    t_tot = x.shape[0]
    tbs = min(512, t_tot)
    tb = min(1024, t_tot)
    eb = 8 * PEER_NKEYS
    key_spec = pl.BlockSpec((PEER_HEADS, PEER_NKEYS, tbs), lambda i: (0, 0, i))
    key_f32 = jax.ShapeDtypeStruct((PEER_HEADS, PEER_NKEYS, t_tot), F32)
    hb, s1, s2, thr = pl.pallas_call(
        _peer_select_kernel,
        grid=(t_tot // tbs,),
        in_specs=[
            pl.BlockSpec((tbs, D_MODEL), lambda i: (i, 0)),
            _const_spec((1, D_MODEL)),
            _const_spec((D_MODEL, PEER_HEADS * PEER_DKEY)),
            _const_spec((PEER_NKEYS, PEER_HALF)),
            _const_spec((PEER_NKEYS, PEER_HALF)),
        ],
        out_specs=(pl.BlockSpec((tbs, D_MODEL), lambda i: (i, 0)), key_spec, key_spec,
                   pl.BlockSpec((PEER_HEADS, tbs), lambda i: (0, i))),
        out_shape=(jax.ShapeDtypeStruct((t_tot, D_MODEL), BF16), key_f32, key_f32,
                   jax.ShapeDtypeStruct((PEER_HEADS, t_tot), F32)),
        scratch_shapes=[
            pltpu.VMEM((PEER_TOPK, PEER_HEADS, tbs), F32),
            pltpu.VMEM((PEER_TOPK, PEER_HEADS, tbs), F32),
        ],
        compiler_params=pltpu.CompilerParams(
            dimension_semantics=("arbitrary",), vmem_limit_bytes=VMEM_LIMIT_BYTES),
        name="peer_select",
    )(x, norm_w.reshape(1, D_MODEL), w_q.astype(BF16), keys1.astype(BF16), keys2.astype(BF16))

    tok = lambda i, e: (i, 0)
    once = pl.Buffered(1)
    tw = 2 * LANES
    out = pl.pallas_call(
        functools.partial(_peer_expert_kernel, tb=tb, eb=eb, tw=tw, final_norm=final_norm),
        grid=(t_tot // tb, PEER_EXPERTS // eb),
        in_specs=[
            pl.BlockSpec((tb, D_MODEL), tok, pipeline_mode=once),
            pl.BlockSpec((tb, D_MODEL), tok, pipeline_mode=once),
            pl.BlockSpec((PEER_HEADS, eb // PEER_NKEYS, tb), lambda i, e: (0, e, i)),
            pl.BlockSpec((PEER_HEADS, PEER_NKEYS, tb), lambda i, e: (0, 0, i), pipeline_mode=once),
            pl.BlockSpec((PEER_HEADS, tb), lambda i, e: (0, i), pipeline_mode=once),
            pl.BlockSpec((None, eb, D_MODEL), lambda i, e: (layer, e, 0)),
            pl.BlockSpec((None, eb, D_MODEL), lambda i, e: (layer, jnp.maximum(e - 1, 0), 0)),
            pl.BlockSpec((None, eb, D_MODEL), lambda i, e: (layer, PEER_EXPERTS // eb - 1, 0),
                         pipeline_mode=once),
            _const_spec((1, D_MODEL)),
        ],
        out_specs=pl.BlockSpec((tb, D_MODEL), tok),
        out_shape=jax.ShapeDtypeStruct((t_tot, D_MODEL), F32),
        scratch_shapes=(
            [pltpu.VMEM((D_MODEL, tb), F32)]
            + [pltpu.VMEM((eb, tw), F32) for _ in range(tb // tw)]
            + [pltpu.VMEM((eb, tw), BF16) for _ in range(tb // tw)]
        ),
        compiler_params=pltpu.CompilerParams(
            dimension_semantics=("arbitrary", "arbitrary"), vmem_limit_bytes=VMEM_LIMIT_BYTES),
        name="peer_experts",
    )(hb, x, s1, s2, thr, u_tab, v_tab, v_tab, final_w.reshape(1, D_MODEL))
    return out


def _trunk(x, s5_re, s5_im, s_hg, p, s5_prm, lb_all):
    nb, seq, _ = x.shape
    new_re, new_im, new_hg = [], [], []
    for i in range(DEPTH):
        j = i // N_MIXERS
        if i % N_MIXERS == 0:
            a, bbd, ccd = s5_prm[j]
            x, sr, si = _s5_layer(x, s5_re[j], s5_im[j], p["norm_mix"][i], a, bbd, ccd,
                                  p["s5_d"][j], p["s5_w_glu"][j].astype(BF16), lc=min(seq, 64))
            new_re.append(sr)
            new_im.append(si)
        else:
            x, s_new = _hgrn_layer(x, s_hg[j], p["norm_mix"][i], p["hg_w_in"][j], lb_all[i],
                                   p["hg_norm_w"][j], p["hg_w_out"][j])
            new_hg.append(s_new)
        x = _peer_layer(x.reshape(nb * seq, D_MODEL), p["norm_ffn"][i], p["peer_w_q"][i],
                        p["peer_keys1"][i], p["peer_keys2"][i], p["peer_u"], p["peer_v"],
                        p["norm_final"], layer=i, final_norm=(i == DEPTH - 1)).reshape(nb, seq, D_MODEL)
    return x, jnp.stack(new_re), jnp.stack(new_im), jnp.stack(new_hg)


def kernel(x_prompt, x_sample, state_s5_re, state_s5_im, state_hgrn, norm_mix, norm_ffn, norm_final,
           s5_lambda_re, s5_lambda_im, s5_log_dt, s5_b_re, s5_b_im, s5_c_re, s5_c_im, s5_d, s5_w_glu,
           hg_w_in, hg_lower_bounds, hg_norm_w, hg_w_out, peer_w_q, peer_keys1, peer_keys2, peer_u, peer_v):
    p = dict(norm_mix=norm_mix, norm_ffn=norm_ffn, norm_final=norm_final, s5_d=s5_d, s5_w_glu=s5_w_glu,
             hg_w_in=hg_w_in, hg_norm_w=hg_norm_w, hg_w_out=hg_w_out, peer_w_q=peer_w_q,
             peer_keys1=peer_keys1, peer_keys2=peer_keys2,
             peer_u=peer_u.astype(BF16), peer_v=peer_v.astype(BF16))
    n_a = s5_lambda_re.shape[0]
    s5_prm = [_s5_params(s5_lambda_re[j], s5_lambda_im[j], s5_log_dt[j], s5_b_re[j], s5_b_im[j],
                         s5_c_re[j], s5_c_im[j]) for j in range(n_a)]
    lb_all = jnp.cumsum(jax.nn.softmax(hg_lower_bounds.astype(F32), axis=0), axis=0)
    lb_all = lb_all - lb_all[0:1]
    nbp = x_prompt.shape[0]
    z_s5 = jnp.zeros((n_a, nbp, S5_GROUPS, S5_STATE), state_s5_re.dtype)
    z_hg = jnp.zeros((state_hgrn.shape[0], nbp, HG_HEADS, HG_DK, HG_DV), state_hgrn.dtype)
    y_p, re_p, im_p, hg_p = _trunk(x_prompt, z_s5, z_s5, z_hg, p, s5_prm, lb_all)
    y_s, re_s, im_s, hg_s = _trunk(x_sample, state_s5_re, state_s5_im, state_hgrn, p, s5_prm, lb_all)
    return (y_p, y_s, re_p, im_p, hg_p, re_s, im_s, hg_s)
```

```python
import functools
import math

import jax
import jax.numpy as jnp
from jax import lax
from jax.experimental import pallas as pl
from jax.experimental.pallas import tpu as pltpu

F32 = jnp.float32
BF16 = jnp.bfloat16

D_MODEL = 1024
DEPTH = 2
N_MIXERS = 2
S5_GROUP = 16
S5_GROUPS = D_MODEL // S5_GROUP
S5_STATE = 64
HG_HEADS = 8
HG_DK = 128
HG_DV = 128
HG_FDIM = HG_HEADS * HG_DK
HG_CHUNK = 64
PEER_HEADS = 8
PEER_NKEYS = 128
PEER_EXPERTS = PEER_NKEYS * PEER_NKEYS
PEER_TOPK = 16
PEER_DKEY = 256
PEER_HALF = PEER_DKEY // 2
RMS_EPS = 1e-6

LANES = 128
SUBLANES = 8
VMEM_LIMIT_BYTES = 56 * 2**20

S5_CB = 4
S5_CB_GROUPS = S5_GROUPS // S5_CB
S5_CB_IN = S5_CB_GROUPS * S5_GROUP
S5_CB_STATE = S5_CB_GROUPS * S5_STATE


def _const_spec(shape):
    nd = len(shape)
    return pl.BlockSpec(shape, lambda *_: (0,) * nd, pipeline_mode=pl.Buffered(1))


def _rmsnorm(x, w):
    ms = jnp.mean(x * x, axis=-1, keepdims=True)
    return x * lax.rsqrt(ms + RMS_EPS) * w


def _gelu(x):
    return 0.5 * x * (1.0 + lax.erf(x * (1.0 / math.sqrt(2.0))))


def _sigmoid(x):
    return 1.0 / (1.0 + jnp.exp(-x))


def _dot(a, b):
    return jnp.dot(a, b, preferred_element_type=F32)


def _dot_nt(a, b):
    return lax.dot_general(a, b, (((1,), (1,)), ((), ())), preferred_element_type=F32)


def _dot_tn(a, b):
    return lax.dot_general(a, b, (((0,), (0,)), ((), ())), preferred_element_type=F32)


def _s5_discretize_kernel(lre_ref, lim_ref, ldt_ref, are_ref, aim_ref, zre_ref, zim_ref):
    lr = jnp.minimum(lre_ref[...], -1e-4)
    li = lim_ref[...]
    dt = jnp.exp(ldt_ref[...])
    mag = jnp.exp(lr * dt)
    ang = li * dt
    ab_re = mag * jnp.cos(ang)
    ab_im = mag * jnp.sin(ang)
    den = lr * lr + li * li
    nr = ab_re - 1.0
    are_ref[...] = ab_re
    aim_ref[...] = ab_im
    zre_ref[...] = (nr * lr + ab_im * li) / den
    zim_ref[...] = (ab_im * lr - nr * li) / den


def _s5_scale_b_kernel(zre_ref, zim_ref, bre_ref, bim_ref, ore_ref, oim_ref):
    zr = zre_ref[...]
    zi = zim_ref[...]
    br = bre_ref[...]
    bi = bim_ref[...]
    ore_ref[...] = zr * br - zi * bi
    oim_ref[...] = zr * bi + zi * br


def _s5_params(lam_re, lam_im, log_dt, b_re, b_im, c_re, c_im):
    g, p, hh = S5_GROUPS, S5_STATE, S5_GROUP
    gp = jax.ShapeDtypeStruct((g, p), F32)
    a_re, a_im, z_re, z_im = pl.pallas_call(
        _s5_discretize_kernel, out_shape=(gp, gp, gp, gp), name="s5_discretize",
    )(lam_re.astype(F32), lam_im.astype(F32), log_dt.astype(F32).reshape(g, 1))
    gph = jax.ShapeDtypeStruct((g * p, hh), F32)
    bb_re, bb_im = pl.pallas_call(
        _s5_scale_b_kernel, out_shape=(gph, gph), name="s5_scale_b",
    )(z_re.reshape(g * p, 1), z_im.reshape(g * p, 1),
      b_re.astype(F32).reshape(g * p, hh), b_im.astype(F32).reshape(g * p, hh))
    eye = jnp.eye(S5_CB_GROUPS, dtype=F32)

    def blockdiag_in(w):
        w4 = w.reshape(S5_CB, S5_CB_GROUPS, p, hh)
        return jnp.einsum('cgph,gk->cghkp', w4, eye).reshape(S5_CB, S5_CB_IN, S5_CB_STATE)

    def blockdiag_out(w):
        w4 = w.astype(F32).reshape(S5_CB, S5_CB_GROUPS, hh, p)
        return jnp.einsum('cghp,gk->cgpkh', w4, eye).reshape(S5_CB, S5_CB_STATE, S5_CB_IN)

    bbd = jnp.concatenate([blockdiag_in(bb_re), blockdiag_in(bb_im)], axis=-1).astype(BF16)
    ccd = jnp.concatenate([blockdiag_out(c_re), -blockdiag_out(c_im)], axis=1).astype(BF16)
    a = jnp.stack([a_re.reshape(S5_CB, S5_CB_STATE), a_im.reshape(S5_CB, S5_CB_STATE)], axis=1)
    return a, bbd, ccd


def _s5_pack_state(s_re, s_im):
    nb = s_re.shape[0]
    return jnp.stack([s_re.reshape(nb, S5_CB, S5_CB_STATE), s_im.reshape(nb, S5_CB, S5_CB_STATE)],
                     axis=2).reshape(nb, 2 * S5_GROUPS * S5_STATE)


def _s5_unpack_state(s):
    nb = s.shape[0]
    s4 = s.reshape(nb, S5_CB, 2, S5_CB_STATE)
    return (s4[:, :, 0].reshape(nb, S5_GROUPS, S5_STATE), s4[:, :, 1].reshape(nb, S5_GROUPS, S5_STATE))


def _s5_kernel(x_ref, s0_ref, nw_ref, a_ref, bbd_ref, ccd_ref, d_ref, wglu_ref,
               out_ref, sfin_ref, bu_ref, st_ref, hs_ref, ys_ref, *, nb, lc):
    rows = nb * lc
    nk = D_MODEL // LANES
    lane = lambda k: slice(k * LANES, (k + 1) * LANES)

    @pl.when(pl.program_id(0) == 0)
    def _():
        st_ref[...] = s0_ref[...]

    x = x_ref[...].reshape(rows, D_MODEL)
    h = _rmsnorm(x, nw_ref[...])
    if nb <= lc:
        for b in range(nb):
            for k in range(nk):
                hs_ref[k, pl.ds(b, lc, stride=nb), :] = h[b * lc:(b + 1) * lc, lane(k)]
        hb = jnp.concatenate([hs_ref[k] for k in range(nk)], axis=-1).astype(BF16)
    else:
        for k in range(nk):
            hs_ref[k] = h[:, lane(k)]
        hb = jnp.concatenate(
            [jnp.concatenate([hs_ref[k, pl.ds(t, nb, stride=lc), :] for t in range(lc)], axis=0)
             for k in range(nk)], axis=-1).astype(BF16)
    nt = S5_CB_STATE // LANES
    for cb in range(S5_CB):
        bu = _dot(hb[:, cb * S5_CB_IN:(cb + 1) * S5_CB_IN], bbd_ref[cb])
        for k in range(2 * nt):
            bu_ref[k] = bu[:, lane(k)]
        a_re = [jnp.broadcast_to(a_ref[cb, 0:1, lane(k)], (SUBLANES, LANES)) for k in range(nt)]
        a_im = [jnp.broadcast_to(a_ref[cb, 1:2, lane(k)], (SUBLANES, LANES)) for k in range(nt)]
        col = cb * 2 * S5_CB_STATE
        for j in range(nb // SUBLANES):
            brow = j * SUBLANES

            def step(t, carry, brow=brow):
                idx = pl.ds(pl.multiple_of(t * nb + brow, SUBLANES), SUBLANES)
                new_re, new_im = [], []
                for k in range(nt):
                    s_re, s_im = carry[k], carry[nt + k]
                    n_re = a_re[k] * s_re - a_im[k] * s_im + bu_ref[k, idx, :]
                    n_im = a_re[k] * s_im + a_im[k] * s_re + bu_ref[nt + k, idx, :]
                    bu_ref[k, idx, :] = n_re
                    bu_ref[nt + k, idx, :] = n_im
                    new_re.append(n_re)
                    new_im.append(n_im)
                return tuple(new_re) + tuple(new_im)

            carry = tuple(st_ref[brow:brow + SUBLANES, col + k * LANES:col + (k + 1) * LANES]
                          for k in range(2 * nt))
            if lc <= 8:
                for t in range(lc):
                    carry = step(t, carry)
            else:
                carry = lax.fori_loop(0, lc, step, carry, unroll=4)
            for k in range(2 * nt):
                st_ref[brow:brow + SUBLANES, col + k * LANES:col + (k + 1) * LANES] = carry[k]
        st_all = jnp.concatenate([bu_ref[k] for k in range(2 * nt)], axis=-1)
        y_cb = _dot(st_all.astype(BF16), ccd_ref[cb])
        for k in range(S5_CB_IN // LANES):
            ys_ref[cb * (S5_CB_IN // LANES) + k] = y_cb[:, lane(k)]
    if nb <= lc:
        y = jnp.concatenate(
            [jnp.concatenate([ys_ref[k, pl.ds(b, lc, stride=nb), :] for b in range(nb)], axis=0)
             for k in range(nk)], axis=-1)
    else:
        for t in range(lc):
            for k in range(nk):
                hs_ref[k, pl.ds(t, nb, stride=lc), :] = ys_ref[k, t * nb:(t + 1) * nb, :]
        y = jnp.concatenate([hs_ref[k] for k in range(nk)], axis=-1)
    y = _gelu(y + d_ref[...] * h)
    z = _dot(y.astype(BF16), wglu_ref[...])
    out = x + z[:, :D_MODEL] * _sigmoid(z[:, D_MODEL:])
    out_ref[...] = out.reshape(out_ref.shape)
    sfin_ref[...] = st_ref[...]


def _s5_layer(x, s0_re, s0_im, norm_w, a, bbd, ccd, d, w_glu, *, lc):
    nb, seq, _ = x.shape
    nstate = 2 * S5_GROUPS * S5_STATE
    rows = nb * lc
    if seq == lc:
        xin = x.reshape(rows, D_MODEL)
        x_spec = pl.BlockSpec((rows, D_MODEL), lambda c: (0, 0))
    else:
        xin = x
        x_spec = pl.BlockSpec((nb, lc, D_MODEL), lambda c: (0, c, 0))
    out, sfin = pl.pallas_call(
        functools.partial(_s5_kernel, nb=nb, lc=lc),
        grid=(seq // lc,),
        in_specs=[
            x_spec,
            _const_spec((nb, nstate)),
            _const_spec((1, D_MODEL)),
            _const_spec((S5_CB, 2, S5_CB_STATE)),
            _const_spec((S5_CB, S5_CB_IN, 2 * S5_CB_STATE)),
            _const_spec((S5_CB, 2 * S5_CB_STATE, S5_CB_IN)),
            _const_spec((1, D_MODEL)),
            _const_spec((D_MODEL, 2 * D_MODEL)),
        ],
        out_specs=(x_spec, pl.BlockSpec((nb, nstate), lambda c: (0, 0))),
        out_shape=(jax.ShapeDtypeStruct(xin.shape, F32), jax.ShapeDtypeStruct((nb, nstate), F32)),
        scratch_shapes=[
            pltpu.VMEM((2 * S5_CB_STATE // LANES, rows, LANES), F32),
            pltpu.VMEM((nb, nstate), F32),
            pltpu.VMEM((D_MODEL // LANES, rows, LANES), F32),
            pltpu.VMEM((D_MODEL // LANES, rows, LANES), F32),
        ],
        compiler_params=pltpu.CompilerParams(
            dimension_semantics=("arbitrary",), vmem_limit_bytes=VMEM_LIMIT_BYTES),
        name="s5_layer",
    )(xin, _s5_pack_state(s0_re, s0_im), norm_w.reshape(1, D_MODEL), a, bbd, ccd,
      d.reshape(1, D_MODEL), w_glu)
    new_re, new_im = _s5_unpack_state(sfin)
    return out.reshape(x.shape), new_re, new_im


def _hgrn_gates(proj, lb):
    q = proj[:, 0:HG_FDIM]
    q = q * _sigmoid(q)
    fz = proj[:, HG_FDIM:2 * HG_FDIM]
    v = proj[:, 2 * HG_FDIM:2 * HG_FDIM + D_MODEL]
    g = proj[:, 2 * HG_FDIM + D_MODEL:]
    log_sig = jnp.minimum(fz, 0.0) - jnp.log1p(jnp.exp(-jnp.abs(fz)))
    t1 = jnp.log(lb)
    t2 = jnp.log1p(-lb) + log_sig
    hi = jnp.maximum(t1, t2)
    lo = jnp.minimum(t1, t2)
    logf = hi + jnp.log1p(jnp.exp(lo - hi))
    k = (1.0 - lb) * _sigmoid(-fz)
    return q, k, v, g, logf


def _hgrn_tile(q, k, v, g, logf, st_refs, nw, *, chunk):
    rows = q.shape[0]
    nseg = rows // chunk
    ri = lax.broadcasted_iota(jnp.int32, (rows, rows), 0)
    ci = lax.broadcasted_iota(jnp.int32, (rows, rows), 1)
    if nseg == 1:
        same = ci <= ri
    else:
        same = (ci <= ri) & ((ri // chunk) == (ci // chunk))
    tri = jnp.where(same, 1.0, 0.0).astype(F32)
    b = jnp.dot(tri, logf, precision=lax.Precision.HIGHEST, preferred_element_type=F32)
    rowid = lax.broadcasted_iota(jnp.int32, (rows, 1), 0)
    b_mid = b[chunk // 2:chunk // 2 + 1, :]
    b_last = b[chunk - 1:chunk, :]
    for s in range(1, nseg):
        in_s = rowid >= s * chunk
        b_mid = jnp.where(in_s, b[s * chunk + chunk // 2:s * chunk + chunk // 2 + 1, :], b_mid)
        b_last = jnp.where(in_s, b[s * chunk + chunk - 1:s * chunk + chunk, :], b_last)
    qs = (q * jnp.exp(b - b_mid)).astype(BF16)
    ks = (k * jnp.exp(b_mid - b)).astype(BF16)
    qi = (q * jnp.exp(b)).astype(BF16)
    kd = k * jnp.exp(b_last - b)
    vb = v.astype(BF16)
    outs = []
    for hd in range(HG_HEADS):
        sl = slice(hd * HG_DK, (hd + 1) * HG_DK)
        scores = jnp.where(same, _dot_nt(qs[:, sl], ks[:, sl]), 0.0)
        o = _dot(scores.astype(BF16), vb[:, sl])
        v_t = v[:, sl].T
        for s in range(nseg):
            st = st_refs[s][hd]
            o_s = _dot_nt(qi[:, sl], st.astype(BF16))
            kd_s = kd[:, sl]
            if nseg > 1:
                in_s = (rowid >= s * chunk) & (rowid < (s + 1) * chunk)
                o_s = jnp.where(in_s, o_s, 0.0)
                kd_s = jnp.where(in_s, kd_s, 0.0)
            o = o + o_s
            dec = jnp.exp(b[s * chunk + chunk - 1:s * chunk + chunk, sl])
            st_refs[s][hd] = dec * st + _dot(v_t.astype(BF16), kd_s.astype(BF16))
        o = o * lax.rsqrt(jnp.mean(o * o, axis=-1, keepdims=True) + RMS_EPS)
        gh = g[:, sl]
        outs.append(o * nw * (gh * _sigmoid(gh)))
    return jnp.concatenate(outs, axis=-1)


def _hgrn_prompt_kernel(x_ref, s0_ref, nw_ref, win_ref, lb_ref, hnw_ref, wout_ref,
                        out_ref, sout_ref, st_ref, y_ref, *, lc, chunk):
    c = pl.program_id(1)

    @pl.when(c == 0)
    def _():
        for hd in range(HG_HEADS):
            st_ref[hd] = s0_ref[0, hd].T

    x = x_ref[0]
    hb = _rmsnorm(x, nw_ref[...]).astype(BF16)
    proj = _dot(hb, win_ref[...])
    q, k, v, g, logf = _hgrn_gates(proj, lb_ref[...])
    for sub in range(lc // chunk):
        r = slice(sub * chunk, (sub + 1) * chunk)
        y_ref[r, :] = _hgrn_tile(q[r], k[r], v[r], g[r], logf[r], [st_ref], hnw_ref[...], chunk=chunk)
    out_ref[0] = x + _dot(y_ref[...].astype(BF16), wout_ref[...])

    @pl.when(c == pl.num_programs(1) - 1)
    def _():
        for hd in range(HG_HEADS):
            sout_ref[0, hd] = st_ref[hd].T


def _hgrn_sample_kernel(x_ref, s0_ref, nw_ref, win_ref, lb_ref, hnw_ref, wout_ref,
                        out_ref, sout_ref, st_ref, proj_ref, y_ref, *, seq):
    p = pl.program_id(0)
    nseg = SUBLANES // seq

    @pl.when(p == 0)
    def _():
        hb = _rmsnorm(x_ref[...], nw_ref[...]).astype(BF16)
        proj_ref[...] = _dot(hb, win_ref[...])

    for s in range(nseg):
        for hd in range(HG_HEADS):
            st_ref[s, hd] = s0_ref[s, hd].T
    r = pl.ds(pl.multiple_of(p * SUBLANES, SUBLANES), SUBLANES)
    q, k, v, g, logf = _hgrn_gates(proj_ref[r, :], lb_ref[...])
    y_ref[r, :] = _hgrn_tile(q, k, v, g, logf, [st_ref.at[s] for s in range(nseg)], hnw_ref[...],
                             chunk=seq)
    for s in range(nseg):
        for hd in range(HG_HEADS):
            sout_ref[s, hd] = st_ref[s, hd].T

    @pl.when(p == pl.num_programs(0) - 1)
    def _():
        out_ref[...] = x_ref[...] + _dot(y_ref[...].astype(BF16), wout_ref[...])


def _hgrn_layer(x, s0, norm_w, w_in, lb, hg_norm_w, w_out):
    nb, seq, _ = x.shape
    weights = (norm_w.reshape(1, D_MODEL), w_in.astype(BF16), lb.reshape(1, HG_FDIM),
               hg_norm_w.reshape(1, HG_DV), w_out.astype(BF16))
    w_specs = [
        _const_spec((1, D_MODEL)),
        _const_spec((D_MODEL, 2 * HG_FDIM + 2 * D_MODEL)),
        _const_spec((1, HG_FDIM)),
        _const_spec((1, HG_DV)),
        _const_spec((D_MODEL, D_MODEL)),
    ]
    s_shape = jax.ShapeDtypeStruct(s0.shape, F32)
    if seq >= HG_CHUNK:
        chunk = HG_CHUNK
        lc = 4 * chunk
        x_spec = pl.BlockSpec((1, lc, D_MODEL), lambda b, c: (b, c, 0))
        s_spec = pl.BlockSpec((1, HG_HEADS, HG_DK, HG_DV), lambda b, c: (b, 0, 0, 0))
        out, s_new = pl.pallas_call(
            functools.partial(_hgrn_prompt_kernel, lc=lc, chunk=chunk),
            grid=(nb, seq // lc),
            in_specs=[x_spec, s_spec] + w_specs,
            out_specs=(x_spec, s_spec),
            out_shape=(jax.ShapeDtypeStruct(x.shape, F32), s_shape),
            scratch_shapes=[
                pltpu.VMEM((HG_HEADS, HG_DV, HG_DK), F32),
                pltpu.VMEM((lc, D_MODEL), F32),
            ],
            compiler_params=pltpu.CompilerParams(
                dimension_semantics=("arbitrary", "arbitrary"), vmem_limit_bytes=VMEM_LIMIT_BYTES),
            name="hgrn_prompt",
        )(x, s0, *weights)
        return out, s_new
    assert SUBLANES % seq == 0 and nb % (SUBLANES // seq) == 0
    nseg = SUBLANES // seq
    rows = nb * seq
    x_spec = _const_spec((rows, D_MODEL))
    s_spec = pl.BlockSpec((nseg, HG_HEADS, HG_DK, HG_DV), lambda p: (p, 0, 0, 0))
    out, s_new = pl.pallas_call(
        functools.partial(_hgrn_sample_kernel, seq=seq),
        grid=(nb // nseg,),
        in_specs=[x_spec, s_spec] + w_specs,
        out_specs=(pl.BlockSpec((rows, D_MODEL), lambda p: (0, 0)), s_spec),
        out_shape=(jax.ShapeDtypeStruct((rows, D_MODEL), F32), s_shape),
        scratch_shapes=[
            pltpu.VMEM((nseg, HG_HEADS, HG_DV, HG_DK), F32),
            pltpu.VMEM((rows, 2 * HG_FDIM + 2 * D_MODEL), F32),
            pltpu.VMEM((rows, D_MODEL), F32),
        ],
        compiler_params=pltpu.CompilerParams(
            dimension_semantics=("arbitrary",), vmem_limit_bytes=VMEM_LIMIT_BYTES),
        name="hgrn_sample",
    )(x.reshape(rows, D_MODEL), s0, *weights)
    return out.reshape(x.shape), s_new


def _compare_exchange(v, i, j):
    hi = jnp.maximum(v[i], v[j])
    lo = jnp.minimum(v[i], v[j])
    v[i] = hi
    v[j] = lo


def _bitonic_merge_desc(v):
    n = len(v)
    j = n // 2
    while j >= 1:
        for i in range(n):
            if i & j == 0:
                _compare_exchange(v, i, i + j)
        j //= 2


def _sort_desc(v):
    n = len(v)
    k = 2
    while k <= n:
        j = k // 2
        while j >= 1:
            for i in range(n):
                l = i ^ j
                if l > i:
                    if i & k == 0:
                        _compare_exchange(v, i, l)
                    else:
                        _compare_exchange(v, l, i)
            j //= 2
        k *= 2
    return v


def _merge_top(a, b):
    n = len(a)
    v = [jnp.maximum(a[r], b[n - 1 - r]) for r in range(n)]
    _bitonic_merge_desc(v)
    return v


def _top16_rows(s):
    v = [s[SUBLANES * j:SUBLANES * (j + 1), :] for j in range(PEER_NKEYS // SUBLANES)]
    v = _sort_desc(v)
    for shift in (1, 2, 4):
        v = _merge_top(v, [pltpu.roll(x, shift, axis=0) for x in v])
    return v


def _candidate_top(a, b):
    n = PEER_TOPK
    low = jnp.full(a[0].shape, -jnp.inf, F32)
    runs = [
        [a[i] + b[0] for i in range(1, n)],
        [a[1] + b[j] for j in range(1, 8)],
        [a[2] + b[j] for j in range(1, 5)],
        [a[3] + b[j] for j in range(1, 4)],
        [a[4] + b[j] for j in range(1, 3)],
        [a[i] + b[1] for i in range(5, 8)],
    ]
    assert all((i + 1) * (j + 1) > n for i in range(1, n) for j in range(1, n)
               if not ((i == 1 and j < 8) or (i == 2 and j < 5) or (i == 3 and j < 4)
                       or (i == 4 and j < 3) or (5 <= i < 8 and j == 1)))
    top = [a[0] + b[j] for j in range(n)]
    for run in runs:
        top = _merge_top(top, run + [low] * (n - len(run)))
    return top


def _peer_select_kernel(x_ref, nw_ref, wq_ref, k1_ref, k2_ref,
                        hb_ref, tau_ref, e1_ref, s2_ref, e2_ref, t1_ref, t2_ref, s1_ref):
    hb = _rmsnorm(x_ref[...], nw_ref[...]).astype(BF16)
    hb_ref[...] = hb
    qb = _dot(hb, wq_ref[...]).astype(BF16)
    k1 = k1_ref[...]
    k2 = k2_ref[...]
    log2e = 1.0 / math.log(2.0)
    for hd in range(PEER_HEADS):
        base = hd * PEER_DKEY
        s1 = _dot_nt(k1, qb[:, base:base + PEER_HALF]) * log2e
        s2 = _dot_nt(k2, qb[:, base + PEER_HALF:base + PEER_DKEY]) * log2e
        s1_ref[hd] = s1
        s2_ref[hd] = s2
        top1 = _top16_rows(s1)
        top2 = _top16_rows(s2)
        for r in range(PEER_TOPK):
            t1_ref[r, hd:hd + 1, :] = top1[r][0:1, :]
            t2_ref[r, hd:hd + 1, :] = top2[r][0:1, :]
        e2_ref[hd] = jnp.exp2(s2 - top2[0][0:1, :])
    a = [t1_ref[r] for r in range(PEER_TOPK)]
    b = [t2_ref[r] for r in range(PEER_TOPK)]
    top = _candidate_top(a, b)
    z = jnp.exp2(top[0] - top[0])
    for r in range(1, PEER_TOPK):
        z = z + jnp.exp2(top[r] - top[0])
    thr = top[PEER_TOPK - 1]
    scale = 0.5 / z
    inf = jnp.full(thr.shape, jnp.inf, F32)
    theta = []
    for p in range(PEER_TOPK):
        th = inf
        for j in range(PEER_TOPK // (p + 1)):
            th = jnp.where(a[p] + b[j] >= thr, b[j], th)
        theta.append(th)
    for hd in range(PEER_HEADS):
        s1 = s1_ref[hd]
        row = slice(hd, hd + 1)
        tau = jnp.full(s1.shape, jnp.inf, F32)
        for p in reversed(range(PEER_TOPK)):
            tau = jnp.where(s1 >= t1_ref[p, row, :], theta[p][row, :], tau)
        tau_ref[hd] = tau
        e1_ref[hd] = jnp.exp2(s1 - t1_ref[0, row, :]) * scale[row, :]


def _peer_expert_kernel(hb_ref, x_ref, tau_ref, e1_ref, s2_ref, e2_ref, u_ref, vp_ref, v_ref, fw_ref,
                        out_ref, acc_ref, *tile_refs, tb, eb, tw, final_norm):
    e = pl.program_id(1)
    ntile = tb // tw
    ht_refs, at_refs = tile_refs[:ntile], tile_refs[ntile:]

    @pl.when(e == 0)
    def _():
        acc_ref[...] = jnp.zeros_like(acc_ref)
        for at_ref in at_refs:
            at_ref[...] = jnp.zeros_like(at_ref)

    u = u_ref[...]
    vp = vp_ref[...]
    for i in range(ntile):
        ht_refs[i][...] = _dot_nt(u, hb_ref[i * tw:(i + 1) * tw, :])
        acc_ref[:, i * tw:(i + 1) * tw] += _dot_tn(vp, at_refs[i][...])
    for i in range(ntile):
        for lt in range(tw // LANES):
            lcol = slice(lt * LANES, (lt + 1) * LANES)
            col = slice(i * tw + lt * LANES, i * tw + (lt + 1) * LANES)
            for r in range(eb // PEER_NKEYS):
                gate = None
                for hd in range(PEER_HEADS):
                    g = jnp.where(s2_ref[hd, :, col] >= tau_ref[hd, r:r + 1, col],
                                  e2_ref[hd, :, col] * e1_ref[hd, r:r + 1, col], 0.0)
                    gate = g if gate is None else gate + g
                rr = slice(r * PEER_NKEYS, (r + 1) * PEER_NKEYS)
                h = ht_refs[i][rr, lcol]
                at_refs[i][rr, lcol] = (h * (1.0 + lax.erf(h * (1.0 / math.sqrt(2.0)))) * gate).astype(BF16)

    @pl.when(e == pl.num_programs(1) - 1)
    def _():
        v = v_ref[...]
        for i in range(ntile):
            acc_ref[:, i * tw:(i + 1) * tw] += _dot_tn(v, at_refs[i][...])
        y = x_ref[...] + acc_ref[...].T
        if final_norm:
            y = _rmsnorm(y, fw_ref[...])
        out_ref[...] = y


def _peer_layer(x, norm_w, w_q, keys1, keys2, u_tab, v_tab, final_w, *, layer, final_norm):
    t_tot = x.shape[0]
    tbs = min(512, t_tot)
    tb = min(1024, t_tot)
    eb = 8 * PEER_NKEYS
    key_spec = pl.BlockSpec((PEER_HEADS, PEER_NKEYS, tbs), lambda i: (0, 0, i))
    key_f32 = jax.ShapeDtypeStruct((PEER_HEADS, PEER_NKEYS, t_tot), F32)
    hb, tau, e1, s2, e2 = pl.pallas_call(
        _peer_select_kernel,
        grid=(t_tot // tbs,),
        in_specs=[
            pl.BlockSpec((tbs, D_MODEL), lambda i: (i, 0)),
            _const_spec((1, D_MODEL)),
            _const_spec((D_MODEL, PEER_HEADS * PEER_DKEY)),
            _const_spec((PEER_NKEYS, PEER_HALF)),
            _const_spec((PEER_NKEYS, PEER_HALF)),
        ],
        out_specs=(pl.BlockSpec((tbs, D_MODEL), lambda i: (i, 0)), key_spec, key_spec, key_spec, key_spec),
        out_shape=(jax.ShapeDtypeStruct((t_tot, D_MODEL), BF16), key_f32, key_f32, key_f32, key_f32),
        scratch_shapes=[
            pltpu.VMEM((PEER_TOPK, PEER_HEADS, tbs), F32),
            pltpu.VMEM((PEER_TOPK, PEER_HEADS, tbs), F32),
            pltpu.VMEM((PEER_HEADS, PEER_NKEYS, tbs), F32),
        ],
        compiler_params=pltpu.CompilerParams(
            dimension_semantics=("arbitrary",), vmem_limit_bytes=VMEM_LIMIT_BYTES),
        name="peer_select",
    )(x, norm_w.reshape(1, D_MODEL), w_q.astype(BF16), keys1.astype(BF16), keys2.astype(BF16))

    tok = lambda i, e: (i, 0)
    once = pl.Buffered(1)
    tw = 2 * LANES
    out = pl.pallas_call(
        functools.partial(_peer_expert_kernel, tb=tb, eb=eb, tw=tw, final_norm=final_norm),
        grid=(t_tot // tb, PEER_EXPERTS // eb),
        in_specs=[
            pl.BlockSpec((tb, D_MODEL), tok, pipeline_mode=once),
            pl.BlockSpec((tb, D_MODEL), tok, pipeline_mode=once),
            pl.BlockSpec((PEER_HEADS, eb // PEER_NKEYS, tb), lambda i, e: (0, e, i)),
            pl.BlockSpec((PEER_HEADS, eb // PEER_NKEYS, tb), lambda i, e: (0, e, i)),
            pl.BlockSpec((PEER_HEADS, PEER_NKEYS, tb), lambda i, e: (0, 0, i), pipeline_mode=once),
            pl.BlockSpec((PEER_HEADS, PEER_NKEYS, tb), lambda i, e: (0, 0, i), pipeline_mode=once),
            pl.BlockSpec((None, eb, D_MODEL), lambda i, e: (layer, e, 0)),
            pl.BlockSpec((None, eb, D_MODEL), lambda i, e: (layer, jnp.maximum(e - 1, 0), 0)),
            pl.BlockSpec((None, eb, D_MODEL), lambda i, e: (layer, PEER_EXPERTS // eb - 1, 0),
                         pipeline_mode=once),
            _const_spec((1, D_MODEL)),
        ],
        out_specs=pl.BlockSpec((tb, D_MODEL), tok),
        out_shape=jax.ShapeDtypeStruct((t_tot, D_MODEL), F32),
        scratch_shapes=(
            [pltpu.VMEM((D_MODEL, tb), F32)]
            + [pltpu.VMEM((eb, tw), F32) for _ in range(tb // tw)]
            + [pltpu.VMEM((eb, tw), BF16) for _ in range(tb // tw)]
        ),
        compiler_params=pltpu.CompilerParams(
            dimension_semantics=("arbitrary", "arbitrary"), vmem_limit_bytes=VMEM_LIMIT_BYTES),
        name="peer_experts",
    )(hb, x, tau, e1, s2, e2, u_tab, v_tab, v_tab, final_w.reshape(1, D_MODEL))
    return out


def _trunk(x, s5_re, s5_im, s_hg, p, s5_prm, lb_all):
    nb, seq, _ = x.shape
    new_re, new_im, new_hg = [], [], []
    for i in range(DEPTH):
        j = i // N_MIXERS
        if i % N_MIXERS == 0:
            a, bbd, ccd = s5_prm[j]
            x, sr, si = _s5_layer(x, s5_re[j], s5_im[j], p["norm_mix"][i], a, bbd, ccd,
                                  p["s5_d"][j], p["s5_w_glu"][j].astype(BF16), lc=min(seq, 64))
            new_re.append(sr)
            new_im.append(si)
        else:
            x, s_new = _hgrn_layer(x, s_hg[j], p["norm_mix"][i], p["hg_w_in"][j], lb_all[i],
                                   p["hg_norm_w"][j], p["hg_w_out"][j])
            new_hg.append(s_new)
        x = _peer_layer(x.reshape(nb * seq, D_MODEL), p["norm_ffn"][i], p["peer_w_q"][i],
                        p["peer_keys1"][i], p["peer_keys2"][i], p["peer_u"], p["peer_v"],
                        p["norm_final"], layer=i, final_norm=(i == DEPTH - 1)).reshape(nb, seq, D_MODEL)
    return x, jnp.stack(new_re), jnp.stack(new_im), jnp.stack(new_hg)


def kernel(x_prompt, x_sample, state_s5_re, state_s5_im, state_hgrn, norm_mix, norm_ffn, norm_final,
           s5_lambda_re, s5_lambda_im, s5_log_dt, s5_b_re, s5_b_im, s5_c_re, s5_c_im, s5_d, s5_w_glu,
           hg_w_in, hg_lower_bounds, hg_norm_w, hg_w_out, peer_w_q, peer_keys1, peer_keys2, peer_u, peer_v):
    p = dict(norm_mix=norm_mix, norm_ffn=norm_ffn, norm_final=norm_final, s5_d=s5_d, s5_w_glu=s5_w_glu,
             hg_w_in=hg_w_in, hg_norm_w=hg_norm_w, hg_w_out=hg_w_out, peer_w_q=peer_w_q,
             peer_keys1=peer_keys1, peer_keys2=peer_keys2,
             peer_u=peer_u.astype(BF16), peer_v=peer_v.astype(BF16))
    n_a = s5_lambda_re.shape[0]
    s5_prm = [_s5_params(s5_lambda_re[j], s5_lambda_im[j], s5_log_dt[j], s5_b_re[j], s5_b_im[j],
                         s5_c_re[j], s5_c_im[j]) for j in range(n_a)]
    lb_all = jnp.cumsum(jax.nn.softmax(hg_lower_bounds.astype(F32), axis=0), axis=0)
    lb_all = lb_all - lb_all[0:1]
    nbp = x_prompt.shape[0]
    z_s5 = jnp.zeros((n_a, nbp, S5_GROUPS, S5_STATE), state_s5_re.dtype)
    z_hg = jnp.zeros((state_hgrn.shape[0], nbp, HG_HEADS, HG_DK, HG_DV), state_hgrn.dtype)
    y_p, re_p, im_p, hg_p = _trunk(x_prompt, z_s5, z_s5, z_hg, p, s5_prm, lb_all)
    y_s, re_s, im_s, hg_s = _trunk(x_sample, state_s5_re, state_s5_im, state_hgrn, p, s5_prm, lb_all)
    return (y_p, y_s, re_p, im_p, hg_p, re_s, im_s, hg_s)
```

```python
import functools
import math

import jax
import jax.numpy as jnp
from jax import lax
from jax.experimental import pallas as pl
from jax.experimental.pallas import tpu as pltpu

F32 = jnp.float32
BF16 = jnp.bfloat16

D_MODEL = 1024
DEPTH = 2
N_MIXERS = 2
S5_GROUP = 16
S5_GROUPS = D_MODEL // S5_GROUP
S5_STATE = 64
HG_HEADS = 8
HG_DK = 128
HG_DV = 128
HG_FDIM = HG_HEADS * HG_DK
HG_CHUNK = 64
PEER_HEADS = 8
PEER_NKEYS = 128
PEER_EXPERTS = PEER_NKEYS * PEER_NKEYS
PEER_TOPK = 16
PEER_DKEY = 256
PEER_HALF = PEER_DKEY // 2
RMS_EPS = 1e-6

LANES = 128
SUBLANES = 8
MXU_WIDTH = 256
VMEM_LIMIT_BYTES = 56 * 2**20

S5_TIME_CHUNK = 64
HG_CHUNKS_PER_STEP = 8
PEER_SELECT_TOKENS = 512
PEER_EXPERT_TOKENS = 1024
PEER_EXPERT_BLOCK = 8 * PEER_NKEYS
PEER_TOKEN_TILE = MXU_WIDTH

S5_CB = 4
S5_CB_GROUPS = S5_GROUPS // S5_CB
S5_CB_IN = S5_CB_GROUPS * S5_GROUP
S5_CB_STATE = S5_CB_GROUPS * S5_STATE


def _const_spec(shape):
    nd = len(shape)
    return pl.BlockSpec(shape, lambda *_: (0,) * nd, pipeline_mode=pl.Buffered(1))


def _rmsnorm(x, w):
    ms = jnp.mean(x * x, axis=-1, keepdims=True)
    return x * lax.rsqrt(ms + RMS_EPS) * w


def _gelu(x):
    return 0.5 * x * (1.0 + lax.erf(x * (1.0 / math.sqrt(2.0))))


def _sigmoid(x):
    return 1.0 / (1.0 + jnp.exp(-x))


def _dot(a, b):
    return jnp.dot(a, b, preferred_element_type=F32)


def _dot_nt(a, b):
    return lax.dot_general(a, b, (((1,), (1,)), ((), ())), preferred_element_type=F32)


def _dot_tn(a, b):
    return lax.dot_general(a, b, (((0,), (0,)), ((), ())), preferred_element_type=F32)


def _s5_discretize_kernel(lre_ref, lim_ref, ldt_ref, are_ref, aim_ref, zre_ref, zim_ref):
    lr = jnp.minimum(lre_ref[...], -1e-4)
    li = lim_ref[...]
    dt = jnp.exp(ldt_ref[...])
    mag = jnp.exp(lr * dt)
    ang = li * dt
    ab_re = mag * jnp.cos(ang)
    ab_im = mag * jnp.sin(ang)
    den = lr * lr + li * li
    nr = ab_re - 1.0
    are_ref[...] = ab_re
    aim_ref[...] = ab_im
    zre_ref[...] = (nr * lr + ab_im * li) / den
    zim_ref[...] = (ab_im * lr - nr * li) / den


def _s5_scale_b_kernel(zre_ref, zim_ref, bre_ref, bim_ref, ore_ref, oim_ref):
    zr = zre_ref[...]
    zi = zim_ref[...]
    br = bre_ref[...]
    bi = bim_ref[...]
    ore_ref[...] = zr * br - zi * bi
    oim_ref[...] = zr * bi + zi * br


def _s5_params(lam_re, lam_im, log_dt, b_re, b_im, c_re, c_im):
    g, p, hh = S5_GROUPS, S5_STATE, S5_GROUP
    gp = jax.ShapeDtypeStruct((g, p), F32)
    a_re, a_im, z_re, z_im = pl.pallas_call(
        _s5_discretize_kernel, out_shape=(gp, gp, gp, gp), name="s5_discretize",
    )(lam_re.astype(F32), lam_im.astype(F32), log_dt.astype(F32).reshape(g, 1))
    gph = jax.ShapeDtypeStruct((g * p, hh), F32)
    bb_re, bb_im = pl.pallas_call(
        _s5_scale_b_kernel, out_shape=(gph, gph), name="s5_scale_b",
    )(z_re.reshape(g * p, 1), z_im.reshape(g * p, 1),
      b_re.astype(F32).reshape(g * p, hh), b_im.astype(F32).reshape(g * p, hh))
    eye = jnp.eye(S5_CB_GROUPS, dtype=F32)

    def blockdiag_in(w):
        w4 = w.reshape(S5_CB, S5_CB_GROUPS, p, hh)
        return jnp.einsum('cgph,gk->cghkp', w4, eye).reshape(S5_CB, S5_CB_IN, S5_CB_STATE)

    def blockdiag_out(w):
        w4 = w.astype(F32).reshape(S5_CB, S5_CB_GROUPS, hh, p)
        return jnp.einsum('cghp,gk->cgpkh', w4, eye).reshape(S5_CB, S5_CB_STATE, S5_CB_IN)

    bbd = jnp.concatenate([blockdiag_in(bb_re), blockdiag_in(bb_im)], axis=-1).astype(BF16)
    ccd = jnp.concatenate([blockdiag_out(c_re), -blockdiag_out(c_im)], axis=1).astype(BF16)
    a = jnp.stack([a_re.reshape(S5_CB, S5_CB_STATE), a_im.reshape(S5_CB, S5_CB_STATE)], axis=1)
    return a, bbd, ccd


def _s5_pack_state(s_re, s_im):
    nb = s_re.shape[0]
    return jnp.stack([s_re.reshape(nb, S5_CB, S5_CB_STATE), s_im.reshape(nb, S5_CB, S5_CB_STATE)],
                     axis=2).reshape(nb, 2 * S5_GROUPS * S5_STATE)


def _s5_unpack_state(s):
    nb = s.shape[0]
    s4 = s.reshape(nb, S5_CB, 2, S5_CB_STATE)
    return (s4[:, :, 0].reshape(nb, S5_GROUPS, S5_STATE), s4[:, :, 1].reshape(nb, S5_GROUPS, S5_STATE))


def _s5_kernel(x_ref, s0_ref, nw_ref, a_ref, bbd_ref, ccd_ref, d_ref, wglu_ref,
               out_ref, sfin_ref, bu_ref, st_ref, hs_ref, ys_ref, *, nb, lc):
    rows = nb * lc
    nk = D_MODEL // LANES
    lane = lambda k: slice(k * LANES, (k + 1) * LANES)

    @pl.when(pl.program_id(0) == 0)
    def _():
        st_ref[...] = s0_ref[...]

    x = x_ref[...].reshape(rows, D_MODEL)
    h = _rmsnorm(x, nw_ref[...])
    if nb <= lc:
        for b in range(nb):
            for k in range(nk):
                hs_ref[k, pl.ds(b, lc, stride=nb), :] = h[b * lc:(b + 1) * lc, lane(k)]
        hb = jnp.concatenate([hs_ref[k] for k in range(nk)], axis=-1).astype(BF16)
    else:
        for k in range(nk):
            hs_ref[k] = h[:, lane(k)]
        hb = jnp.concatenate(
            [jnp.concatenate([hs_ref[k, pl.ds(t, nb, stride=lc), :] for t in range(lc)], axis=0)
             for k in range(nk)], axis=-1).astype(BF16)
    nt = S5_CB_STATE // LANES
    for cb in range(S5_CB):
        bu = _dot(hb[:, cb * S5_CB_IN:(cb + 1) * S5_CB_IN], bbd_ref[cb])
        for k in range(2 * nt):
            bu_ref[k] = bu[:, lane(k)]
        a_re = [jnp.broadcast_to(a_ref[cb, 0:1, lane(k)], (SUBLANES, LANES)) for k in range(nt)]
        a_im = [jnp.broadcast_to(a_ref[cb, 1:2, lane(k)], (SUBLANES, LANES)) for k in range(nt)]
        col = cb * 2 * S5_CB_STATE
        for j in range(nb // SUBLANES):
            brow = j * SUBLANES

            def step(t, carry, brow=brow):
                idx = pl.ds(pl.multiple_of(t * nb + brow, SUBLANES), SUBLANES)
                new_re, new_im = [], []
                for k in range(nt):
                    s_re, s_im = carry[k], carry[nt + k]
                    n_re = a_re[k] * s_re - a_im[k] * s_im + bu_ref[k, idx, :]
                    n_im = a_re[k] * s_im + a_im[k] * s_re + bu_ref[nt + k, idx, :]
                    bu_ref[k, idx, :] = n_re
                    bu_ref[nt + k, idx, :] = n_im
                    new_re.append(n_re)
                    new_im.append(n_im)
                return tuple(new_re) + tuple(new_im)

            carry = tuple(st_ref[brow:brow + SUBLANES, col + k * LANES:col + (k + 1) * LANES]
                          for k in range(2 * nt))
            if lc <= 8:
                for t in range(lc):
                    carry = step(t, carry)
            else:
                carry = lax.fori_loop(0, lc, step, carry, unroll=4)
            for k in range(2 * nt):
                st_ref[brow:brow + SUBLANES, col + k * LANES:col + (k + 1) * LANES] = carry[k]
        st_all = jnp.concatenate([bu_ref[k] for k in range(2 * nt)], axis=-1)
        y_cb = _dot(st_all.astype(BF16), ccd_ref[cb])
        for k in range(S5_CB_IN // LANES):
            ys_ref[cb * (S5_CB_IN // LANES) + k] = y_cb[:, lane(k)]
    if nb <= lc:
        y = jnp.concatenate(
            [jnp.concatenate([ys_ref[k, pl.ds(b, lc, stride=nb), :] for b in range(nb)], axis=0)
             for k in range(nk)], axis=-1)
    else:
        for t in range(lc):
            for k in range(nk):
                hs_ref[k, pl.ds(t, nb, stride=lc), :] = ys_ref[k, t * nb:(t + 1) * nb, :]
        y = jnp.concatenate([hs_ref[k] for k in range(nk)], axis=-1)
    y = _gelu(y + d_ref[...] * h)
    z = _dot(y.astype(BF16), wglu_ref[...])
    out = x + z[:, :D_MODEL] * _sigmoid(z[:, D_MODEL:])
    out_ref[...] = out.reshape(out_ref.shape)
    sfin_ref[...] = st_ref[...]


def _s5_layer(x, s0_re, s0_im, norm_w, a, bbd, ccd, d, w_glu, *, lc):
    nb, seq, _ = x.shape
    nstate = 2 * S5_GROUPS * S5_STATE
    rows = nb * lc
    if seq == lc:
        xin = x.reshape(rows, D_MODEL)
        x_spec = pl.BlockSpec((rows, D_MODEL), lambda c: (0, 0))
    else:
        xin = x
        x_spec = pl.BlockSpec((nb, lc, D_MODEL), lambda c: (0, c, 0))
    out, sfin = pl.pallas_call(
        functools.partial(_s5_kernel, nb=nb, lc=lc),
        grid=(seq // lc,),
        in_specs=[
            x_spec,
            _const_spec((nb, nstate)),
            _const_spec((1, D_MODEL)),
            _const_spec((S5_CB, 2, S5_CB_STATE)),
            _const_spec((S5_CB, S5_CB_IN, 2 * S5_CB_STATE)),
            _const_spec((S5_CB, 2 * S5_CB_STATE, S5_CB_IN)),
            _const_spec((1, D_MODEL)),
            _const_spec((D_MODEL, 2 * D_MODEL)),
        ],
        out_specs=(x_spec, pl.BlockSpec((nb, nstate), lambda c: (0, 0))),
        out_shape=(jax.ShapeDtypeStruct(xin.shape, F32), jax.ShapeDtypeStruct((nb, nstate), F32)),
        scratch_shapes=[
            pltpu.VMEM((2 * S5_CB_STATE // LANES, rows, LANES), F32),
            pltpu.VMEM((nb, nstate), F32),
            pltpu.VMEM((D_MODEL // LANES, rows, LANES), F32),
            pltpu.VMEM((D_MODEL // LANES, rows, LANES), F32),
        ],
        compiler_params=pltpu.CompilerParams(
            dimension_semantics=("arbitrary",), vmem_limit_bytes=VMEM_LIMIT_BYTES),
        name="s5_layer",
    )(xin, _s5_pack_state(s0_re, s0_im), norm_w.reshape(1, D_MODEL), a, bbd, ccd,
      d.reshape(1, D_MODEL), w_glu)
    new_re, new_im = _s5_unpack_state(sfin)
    return out.reshape(x.shape), new_re, new_im


def _hgrn_gates(proj, lb):
    q = proj[:, 0:HG_FDIM]
    q = q * _sigmoid(q)
    fz = proj[:, HG_FDIM:2 * HG_FDIM]
    v = proj[:, 2 * HG_FDIM:2 * HG_FDIM + D_MODEL]
    g = proj[:, 2 * HG_FDIM + D_MODEL:]
    log_sig = jnp.minimum(fz, 0.0) - jnp.log1p(jnp.exp(-jnp.abs(fz)))
    t1 = jnp.log(lb)
    t2 = jnp.log1p(-lb) + log_sig
    hi = jnp.maximum(t1, t2)
    lo = jnp.minimum(t1, t2)
    logf = hi + jnp.log1p(jnp.exp(lo - hi))
    k = (1.0 - lb) * _sigmoid(-fz)
    return q, k, v, g, logf


def _hgrn_tile(q, k, v, g, logf, st_refs, nw, *, chunk):
    rows = q.shape[0]
    nseg = rows // chunk
    ri = lax.broadcasted_iota(jnp.int32, (rows, rows), 0)
    ci = lax.broadcasted_iota(jnp.int32, (rows, rows), 1)
    if nseg == 1:
        same = ci <= ri
    else:
        same = (ci <= ri) & ((ri // chunk) == (ci // chunk))
    tri = jnp.where(same, 1.0, 0.0).astype(F32)
    b = jnp.dot(tri, logf, precision=lax.Precision.HIGHEST, preferred_element_type=F32)
    rowid = lax.broadcasted_iota(jnp.int32, (rows, 1), 0)
    b_mid = b[chunk // 2:chunk // 2 + 1, :]
    b_last = b[chunk - 1:chunk, :]
    for s in range(1, nseg):
        in_s = rowid >= s * chunk
        b_mid = jnp.where(in_s, b[s * chunk + chunk // 2:s * chunk + chunk // 2 + 1, :], b_mid)
        b_last = jnp.where(in_s, b[s * chunk + chunk - 1:s * chunk + chunk, :], b_last)
    qs = (q * jnp.exp(b - b_mid)).astype(BF16)
    ks = (k * jnp.exp(b_mid - b)).astype(BF16)
    qi = (q * jnp.exp(b)).astype(BF16)
    kd = k * jnp.exp(b_last - b)
    vb = v.astype(BF16)
    outs = []
    for hd in range(HG_HEADS):
        sl = slice(hd * HG_DK, (hd + 1) * HG_DK)
        scores = jnp.where(same, _dot_nt(qs[:, sl], ks[:, sl]), 0.0)
        o = _dot(scores.astype(BF16), vb[:, sl])
        v_t = v[:, sl].T
        for s in range(nseg):
            st = st_refs[s][hd]
            o_s = _dot_nt(qi[:, sl], st.astype(BF16))
            kd_s = kd[:, sl]
            if nseg > 1:
                in_s = (rowid >= s * chunk) & (rowid < (s + 1) * chunk)
                o_s = jnp.where(in_s, o_s, 0.0)
                kd_s = jnp.where(in_s, kd_s, 0.0)
            o = o + o_s
            dec = jnp.exp(b[s * chunk + chunk - 1:s * chunk + chunk, sl])
            st_refs[s][hd] = dec * st + _dot(v_t.astype(BF16), kd_s.astype(BF16))
        o = o * lax.rsqrt(jnp.mean(o * o, axis=-1, keepdims=True) + RMS_EPS)
        gh = g[:, sl]
        outs.append(o * nw * (gh * _sigmoid(gh)))
    return jnp.concatenate(outs, axis=-1)


def _hgrn_prompt_kernel(x_ref, s0_ref, nw_ref, win_ref, lb_ref, hnw_ref, wout_ref,
                        out_ref, sout_ref, st_ref, y_ref, *, lc, chunk):
    c = pl.program_id(1)

    @pl.when(c == 0)
    def _():
        for hd in range(HG_HEADS):
            st_ref[hd] = s0_ref[0, hd].T

    x = x_ref[0]
    hb = _rmsnorm(x, nw_ref[...]).astype(BF16)
    proj = _dot(hb, win_ref[...])
    q, k, v, g, logf = _hgrn_gates(proj, lb_ref[...])
    for sub in range(lc // chunk):
        r = slice(sub * chunk, (sub + 1) * chunk)
        y_ref[r, :] = _hgrn_tile(q[r], k[r], v[r], g[r], logf[r], [st_ref], hnw_ref[...], chunk=chunk)
    out_ref[0] = x + _dot(y_ref[...].astype(BF16), wout_ref[...])

    @pl.when(c == pl.num_programs(1) - 1)
    def _():
        for hd in range(HG_HEADS):
            sout_ref[0, hd] = st_ref[hd].T


def _hgrn_sample_kernel(x_ref, s0_ref, nw_ref, win_ref, lb_ref, hnw_ref, wout_ref,
                        out_ref, sout_ref, st_ref, proj_ref, y_ref, *, seq):
    p = pl.program_id(0)
    nseg = SUBLANES // seq

    @pl.when(p == 0)
    def _():
        hb = _rmsnorm(x_ref[...], nw_ref[...]).astype(BF16)
        proj_ref[...] = _dot(hb, win_ref[...])

    for s in range(nseg):
        for hd in range(HG_HEADS):
            st_ref[s, hd] = s0_ref[s, hd].T
    r = pl.ds(pl.multiple_of(p * SUBLANES, SUBLANES), SUBLANES)
    q, k, v, g, logf = _hgrn_gates(proj_ref[r, :], lb_ref[...])
    y_ref[r, :] = _hgrn_tile(q, k, v, g, logf, [st_ref.at[s] for s in range(nseg)], hnw_ref[...],
                             chunk=seq)
    for s in range(nseg):
        for hd in range(HG_HEADS):
            sout_ref[s, hd] = st_ref[s, hd].T

    @pl.when(p == pl.num_programs(0) - 1)
    def _():
        out_ref[...] = x_ref[...] + _dot(y_ref[...].astype(BF16), wout_ref[...])


def _hgrn_layer(x, s0, norm_w, w_in, lb, hg_norm_w, w_out):
    nb, seq, _ = x.shape
    weights = (norm_w.reshape(1, D_MODEL), w_in.astype(BF16), lb.reshape(1, HG_FDIM),
               hg_norm_w.reshape(1, HG_DV), w_out.astype(BF16))
    w_specs = [
        _const_spec((1, D_MODEL)),
        _const_spec((D_MODEL, 2 * HG_FDIM + 2 * D_MODEL)),
        _const_spec((1, HG_FDIM)),
        _const_spec((1, HG_DV)),
        _const_spec((D_MODEL, D_MODEL)),
    ]
    s_shape = jax.ShapeDtypeStruct(s0.shape, F32)
    if seq >= HG_CHUNK:
        chunk = HG_CHUNK
        lc = HG_CHUNKS_PER_STEP * chunk
        x_spec = pl.BlockSpec((1, lc, D_MODEL), lambda b, c: (b, c, 0))
        s_spec = pl.BlockSpec((1, HG_HEADS, HG_DK, HG_DV), lambda b, c: (b, 0, 0, 0))
        out, s_new = pl.pallas_call(
            functools.partial(_hgrn_prompt_kernel, lc=lc, chunk=chunk),
            grid=(nb, seq // lc),
            in_specs=[x_spec, s_spec] + w_specs,
            out_specs=(x_spec, s_spec),
            out_shape=(jax.ShapeDtypeStruct(x.shape, F32), s_shape),
            scratch_shapes=[
                pltpu.VMEM((HG_HEADS, HG_DV, HG_DK), F32),
                pltpu.VMEM((lc, D_MODEL), F32),
            ],
            compiler_params=pltpu.CompilerParams(
                dimension_semantics=("arbitrary", "arbitrary"), vmem_limit_bytes=VMEM_LIMIT_BYTES),
            name="hgrn_prompt",
        )(x, s0, *weights)
        return out, s_new
    assert SUBLANES % seq == 0 and nb % (SUBLANES // seq) == 0
    nseg = SUBLANES // seq
    rows = nb * seq
    x_spec = _const_spec((rows, D_MODEL))
    s_spec = pl.BlockSpec((nseg, HG_HEADS, HG_DK, HG_DV), lambda p: (p, 0, 0, 0))
    out, s_new = pl.pallas_call(
        functools.partial(_hgrn_sample_kernel, seq=seq),
        grid=(nb // nseg,),
        in_specs=[x_spec, s_spec] + w_specs,
        out_specs=(pl.BlockSpec((rows, D_MODEL), lambda p: (0, 0)), s_spec),
        out_shape=(jax.ShapeDtypeStruct((rows, D_MODEL), F32), s_shape),
        scratch_shapes=[
            pltpu.VMEM((nseg, HG_HEADS, HG_DV, HG_DK), F32),
            pltpu.VMEM((rows, 2 * HG_FDIM + 2 * D_MODEL), F32),
            pltpu.VMEM((rows, D_MODEL), F32),
        ],
        compiler_params=pltpu.CompilerParams(
            dimension_semantics=("arbitrary",), vmem_limit_bytes=VMEM_LIMIT_BYTES),
        name="hgrn_sample",
    )(x.reshape(rows, D_MODEL), s0, *weights)
    return out.reshape(x.shape), s_new


def _compare_exchange(v, i, j):
    hi = jnp.maximum(v[i], v[j])
    lo = jnp.minimum(v[i], v[j])
    v[i] = hi
    v[j] = lo


def _bitonic_merge_desc(v):
    n = len(v)
    j = n // 2
    while j >= 1:
        for i in range(n):
            if i & j == 0:
                _compare_exchange(v, i, i + j)
        j //= 2


_SORT16 = (
    (0, 13), (1, 12), (2, 15), (3, 14), (4, 8), (5, 6), (7, 11), (9, 10),
    (0, 5), (1, 7), (2, 9), (3, 4), (6, 13), (8, 14), (10, 15), (11, 12),
    (0, 1), (2, 3), (4, 5), (6, 8), (7, 9), (10, 11), (12, 13), (14, 15),
    (0, 2), (1, 3), (4, 10), (5, 11), (6, 7), (8, 9), (12, 14), (13, 15),
    (1, 2), (3, 12), (4, 6), (5, 7), (8, 10), (9, 11), (13, 14),
    (1, 4), (2, 6), (5, 8), (7, 10), (9, 13), (11, 14),
    (2, 4), (3, 6), (9, 12), (11, 13),
    (3, 5), (6, 8), (7, 9), (10, 12),
    (3, 4), (5, 6), (7, 8), (9, 10), (11, 12),
    (6, 7), (8, 9),
)


def _sort_desc(v):
    assert len(v) == PEER_TOPK
    for i, j in _SORT16:
        _compare_exchange(v, i, j)
    return v


def _merge_top(a, b):
    n = len(a)
    v = [jnp.maximum(a[r], b[n - 1 - r]) for r in range(n)]
    _bitonic_merge_desc(v)
    return v


def _top16_rows(s):
    v = [s[SUBLANES * j:SUBLANES * (j + 1), :] for j in range(PEER_NKEYS // SUBLANES)]
    v = _sort_desc(v)
    for shift in (1, 2, 4):
        v = _merge_top(v, [pltpu.roll(x, shift, axis=0) for x in v])
    return v


def _candidate_top(a, b):
    n = PEER_TOPK
    low = jnp.full(a[0].shape, -jnp.inf, F32)
    runs = [
        [a[i] + b[0] for i in range(1, n)],
        [a[1] + b[j] for j in range(1, 8)],
        [a[2] + b[j] for j in range(1, 5)],
        [a[3] + b[j] for j in range(1, 4)],
        [a[4] + b[j] for j in range(1, 3)],
        [a[i] + b[1] for i in range(5, 8)],
    ]
    assert all((i + 1) * (j + 1) > n for i in range(1, n) for j in range(1, n)
               if not ((i == 1 and j < 8) or (i == 2 and j < 5) or (i == 3 and j < 4)
                       or (i == 4 and j < 3) or (5 <= i < 8 and j == 1)))
    top = [a[0] + b[j] for j in range(n)]
    for run in runs:
        top = _merge_top(top, run + [low] * (n - len(run)))
    return top


def _peer_select_kernel(x_ref, nw_ref, wq_ref, k1_ref, k2_ref,
                        hb_ref, s1_ref, s2_ref, thr_ref, t1_ref, t2_ref):
    hb = _rmsnorm(x_ref[...], nw_ref[...]).astype(BF16)
    hb_ref[...] = hb
    qb = _dot(hb, wq_ref[...]).astype(BF16)
    k1 = k1_ref[...]
    k2 = k2_ref[...]
    log2e = 1.0 / math.log(2.0)
    for hd in range(PEER_HEADS):
        base = hd * PEER_DKEY
        s1 = _dot_nt(k1, qb[:, base:base + PEER_HALF]) * log2e
        s2 = _dot_nt(k2, qb[:, base + PEER_HALF:base + PEER_DKEY]) * log2e
        s1_ref[hd] = s1
        s2_ref[hd] = s2
        for s, t_ref in ((s1, t1_ref), (s2, t2_ref)):
            top = _top16_rows(s)
            for r in range(PEER_TOPK):
                t_ref[r, hd:hd + 1, :] = top[r][0:1, :]
    a = [t1_ref[r] for r in range(PEER_TOPK)]
    b = [t2_ref[r] for r in range(PEER_TOPK)]
    top = _candidate_top(a, b)
    z = jnp.exp2(top[0] - top[0])
    for r in range(1, PEER_TOPK):
        z = z + jnp.exp2(top[r] - top[0])
    shift = top[0] + jnp.log2(z) + 1.0
    for hd in range(PEER_HEADS):
        s1_ref[hd] = s1_ref[hd] - shift[hd:hd + 1, :]
    thr_ref[...] = _candidate_top([v - shift for v in a], b)[PEER_TOPK - 1]


def _peer_expert_kernel(hb_ref, x_ref, s1_ref, s2_ref, thr_ref, u_ref, vp_ref, v_ref, fw_ref,
                        out_ref, acc_ref, *tile_refs, tb, eb, tw, final_norm):
    e = pl.program_id(1)
    ntile = tb // tw
    ht_refs, at_refs = tile_refs[:ntile], tile_refs[ntile:]

    @pl.when(e == 0)
    def _():
        acc_ref[...] = jnp.zeros_like(acc_ref)
        for at_ref in at_refs:
            at_ref[...] = jnp.zeros_like(at_ref)

    u = u_ref[...]
    vp = vp_ref[...]
    for i in range(ntile):
        ht_refs[i][...] = _dot_nt(u, hb_ref[i * tw:(i + 1) * tw, :])
        acc_ref[:, i * tw:(i + 1) * tw] += _dot_tn(vp, at_refs[i][...])
    for i in range(ntile):
        for lt in range(tw // LANES):
            lcol = slice(lt * LANES, (lt + 1) * LANES)
            col = slice(i * tw + lt * LANES, i * tw + (lt + 1) * LANES)
            for r in range(eb // PEER_NKEYS):
                gate = None
                for hd in range(PEER_HEADS):
                    t = s2_ref[hd, :, col] + s1_ref[hd, r:r + 1, col]
                    g = jnp.where(t >= thr_ref[hd:hd + 1, col], jnp.exp2(t), 0.0)
                    gate = g if gate is None else gate + g
                rr = slice(r * PEER_NKEYS, (r + 1) * PEER_NKEYS)
                h = ht_refs[i][rr, lcol]
                at_refs[i][rr, lcol] = (h * (1.0 + lax.erf(h * (1.0 / math.sqrt(2.0)))) * gate).astype(BF16)

    @pl.when(e == pl.num_programs(1) - 1)
    def _():
        v = v_ref[...]
        for i in range(ntile):
            acc_ref[:, i * tw:(i + 1) * tw] += _dot_tn(v, at_refs[i][...])
        y = x_ref[...] + acc_ref[...].T
        if final_norm:
            y = _rmsnorm(y, fw_ref[...])
        out_ref[...] = y


def _peer_layer(x, norm_w, w_q, keys1, keys2, u_tab, v_tab, final_w, *, layer, final_norm):
    t_tot = x.shape[0]
    tbs = min(PEER_SELECT_TOKENS, t_tot)
    tb = min(PEER_EXPERT_TOKENS, t_tot)
    eb = PEER_EXPERT_BLOCK
    tw = PEER_TOKEN_TILE
    key_spec = pl.BlockSpec((PEER_HEADS, PEER_NKEYS, tbs), lambda i: (0, 0, i))
    key_f32 = jax.ShapeDtypeStruct((PEER_HEADS, PEER_NKEYS, t_tot), F32)
    hb, s1, s2, thr = pl.pallas_call(
        _peer_select_kernel,
        grid=(t_tot // tbs,),
        in_specs=[
            pl.BlockSpec((tbs, D_MODEL), lambda i: (i, 0)),
            _const_spec((1, D_MODEL)),
            _const_spec((D_MODEL, PEER_HEADS * PEER_DKEY)),
            _const_spec((PEER_NKEYS, PEER_HALF)),
            _const_spec((PEER_NKEYS, PEER_HALF)),
        ],
        out_specs=(pl.BlockSpec((tbs, D_MODEL), lambda i: (i, 0)), key_spec, key_spec,
                   pl.BlockSpec((PEER_HEADS, tbs), lambda i: (0, i))),
        out_shape=(jax.ShapeDtypeStruct((t_tot, D_MODEL), BF16), key_f32, key_f32,
                   jax.ShapeDtypeStruct((PEER_HEADS, t_tot), F32)),
        scratch_shapes=[
            pltpu.VMEM((PEER_TOPK, PEER_HEADS, tbs), F32),
            pltpu.VMEM((PEER_TOPK, PEER_HEADS, tbs), F32),
        ],
        compiler_params=pltpu.CompilerParams(
            dimension_semantics=("arbitrary",), vmem_limit_bytes=VMEM_LIMIT_BYTES),
        name="peer_select",
    )(x, norm_w.reshape(1, D_MODEL), w_q.astype(BF16), keys1.astype(BF16), keys2.astype(BF16))

    tok = lambda i, e: (i, 0)
    once = pl.Buffered(1)
    out = pl.pallas_call(
        functools.partial(_peer_expert_kernel, tb=tb, eb=eb, tw=tw, final_norm=final_norm),
        grid=(t_tot // tb, PEER_EXPERTS // eb),
        in_specs=[
            pl.BlockSpec((tb, D_MODEL), tok, pipeline_mode=once),
            pl.BlockSpec((tb, D_MODEL), tok, pipeline_mode=once),
            pl.BlockSpec((PEER_HEADS, eb // PEER_NKEYS, tb), lambda i, e: (0, e, i)),
            pl.BlockSpec((PEER_HEADS, PEER_NKEYS, tb), lambda i, e: (0, 0, i), pipeline_mode=once),
            pl.BlockSpec((PEER_HEADS, tb), lambda i, e: (0, i), pipeline_mode=once),
            pl.BlockSpec((None, eb, D_MODEL), lambda i, e: (layer, e, 0)),
            pl.BlockSpec((None, eb, D_MODEL), lambda i, e: (layer, jnp.maximum(e - 1, 0), 0)),
            pl.BlockSpec((None, eb, D_MODEL), lambda i, e: (layer, PEER_EXPERTS // eb - 1, 0),
                         pipeline_mode=once),
            _const_spec((1, D_MODEL)),
        ],
        out_specs=pl.BlockSpec((tb, D_MODEL), tok),
        out_shape=jax.ShapeDtypeStruct((t_tot, D_MODEL), F32),
        scratch_shapes=(
            [pltpu.VMEM((D_MODEL, tb), F32)]
            + [pltpu.VMEM((eb, tw), F32) for _ in range(tb // tw)]
            + [pltpu.VMEM((eb, tw), BF16) for _ in range(tb // tw)]
        ),
        compiler_params=pltpu.CompilerParams(
            dimension_semantics=("arbitrary", "arbitrary"), vmem_limit_bytes=VMEM_LIMIT_BYTES),
        name="peer_experts",
    )(hb, x, s1, s2, thr, u_tab, v_tab, v_tab, final_w.reshape(1, D_MODEL))
    return out


def _trunk(x, s5_re, s5_im, s_hg, p, s5_prm, lb_all):
    nb, seq, _ = x.shape
    new_re, new_im, new_hg = [], [], []
    for i in range(DEPTH):
        j = i // N_MIXERS
        if i % N_MIXERS == 0:
            a, bbd, ccd = s5_prm[j]
            x, sr, si = _s5_layer(x, s5_re[j], s5_im[j], p["norm_mix"][i], a, bbd, ccd,
                                  p["s5_d"][j], p["s5_w_glu"][j].astype(BF16), lc=min(seq, S5_TIME_CHUNK))
            new_re.append(sr)
            new_im.append(si)
        else:
            x, s_new = _hgrn_layer(x, s_hg[j], p["norm_mix"][i], p["hg_w_in"][j], lb_all[i],
                                   p["hg_norm_w"][j], p["hg_w_out"][j])
            new_hg.append(s_new)
        x = _peer_layer(x.reshape(nb * seq, D_MODEL), p["norm_ffn"][i], p["peer_w_q"][i],
                        p["peer_keys1"][i], p["peer_keys2"][i], p["peer_u"], p["peer_v"],
                        p["norm_final"], layer=i, final_norm=(i == DEPTH - 1)).reshape(nb, seq, D_MODEL)
    return x, jnp.stack(new_re), jnp.stack(new_im), jnp.stack(new_hg)


def kernel(x_prompt, x_sample, state_s5_re, state_s5_im, state_hgrn, norm_mix, norm_ffn, norm_final,
           s5_lambda_re, s5_lambda_im, s5_log_dt, s5_b_re, s5_b_im, s5_c_re, s5_c_im, s5_d, s5_w_glu,
           hg_w_in, hg_lower_bounds, hg_norm_w, hg_w_out, peer_w_q, peer_keys1, peer_keys2, peer_u, peer_v):
    p = dict(norm_mix=norm_mix, norm_ffn=norm_ffn, norm_final=norm_final, s5_d=s5_d, s5_w_glu=s5_w_glu,
             hg_w_in=hg_w_in, hg_norm_w=hg_norm_w, hg_w_out=hg_w_out, peer_w_q=peer_w_q,
             peer_keys1=peer_keys1, peer_keys2=peer_keys2,
             peer_u=peer_u.astype(BF16), peer_v=peer_v.astype(BF16))
    n_a = s5_lambda_re.shape[0]
    s5_prm = [_s5_params(s5_lambda_re[j], s5_lambda_im[j], s5_log_dt[j], s5_b_re[j], s5_b_im[j],
                         s5_c_re[j], s5_c_im[j]) for j in range(n_a)]
    lb_all = jnp.cumsum(jax.nn.softmax(hg_lower_bounds.astype(F32), axis=0), axis=0)
    lb_all = lb_all - lb_all[0:1]
    nbp = x_prompt.shape[0]
    z_s5 = jnp.zeros((n_a, nbp, S5_GROUPS, S5_STATE), state_s5_re.dtype)
    z_hg = jnp.zeros((state_hgrn.shape[0], nbp, HG_HEADS, HG_DK, HG_DV), state_hgrn.dtype)
    y_p, re_p, im_p, hg_p = _trunk(x_prompt, z_s5, z_s5, z_hg, p, s5_prm, lb_all)
    y_s, re_s, im_s, hg_s = _trunk(x_sample, state_s5_re, state_s5_im, state_hgrn, p, s5_prm, lb_all)
    return (y_p, y_s, re_p, im_p, hg_p, re_s, im_s, hg_s)
```

```python
import functools
import math

import jax
import jax.numpy as jnp
from jax import lax
from jax.experimental import pallas as pl
from jax.experimental.pallas import tpu as pltpu

F32 = jnp.float32
BF16 = jnp.bfloat16

D_MODEL = 1024
DEPTH = 2
N_MIXERS = 2
S5_GROUP = 16
S5_GROUPS = D_MODEL // S5_GROUP
S5_STATE = 64
HG_HEADS = 8
HG_DK = 128
HG_DV = 128
HG_FDIM = HG_HEADS * HG_DK
HG_CHUNK = 64
PEER_HEADS = 8
PEER_NKEYS = 128
PEER_EXPERTS = PEER_NKEYS * PEER_NKEYS
PEER_TOPK = 16
PEER_DKEY = 256
PEER_HALF = PEER_DKEY // 2
RMS_EPS = 1e-6

LANES = 128
SUBLANES = 8
MXU_WIDTH = 256
VMEM_LIMIT_BYTES = 56 * 2**20

S5_TIME_CHUNK = 64
HG_CHUNKS_PER_STEP = 8
PEER_SELECT_TOKENS = 512
PEER_EXPERT_TOKENS = 1024
PEER_EXPERT_BLOCK = 8 * PEER_NKEYS
PEER_TOKEN_TILE = MXU_WIDTH

S5_CB = 4
S5_CB_GROUPS = S5_GROUPS // S5_CB
S5_CB_IN = S5_CB_GROUPS * S5_GROUP
S5_CB_STATE = S5_CB_GROUPS * S5_STATE


def _const_spec(shape):
    nd = len(shape)
    return pl.BlockSpec(shape, lambda *_: (0,) * nd, pipeline_mode=pl.Buffered(1))


def _rmsnorm(x, w):
    ms = jnp.mean(x * x, axis=-1, keepdims=True)
    return x * lax.rsqrt(ms + RMS_EPS) * w


def _gelu(x):
    return 0.5 * x * (1.0 + lax.erf(x * (1.0 / math.sqrt(2.0))))


def _sigmoid(x):
    return 1.0 / (1.0 + jnp.exp(-x))


def _dot(a, b):
    return jnp.dot(a, b, preferred_element_type=F32)


def _dot_nt(a, b):
    return lax.dot_general(a, b, (((1,), (1,)), ((), ())), preferred_element_type=F32)


def _dot_tn(a, b):
    return lax.dot_general(a, b, (((0,), (0,)), ((), ())), preferred_element_type=F32)


def _s5_discretize_kernel(lre_ref, lim_ref, ldt_ref, are_ref, aim_ref, zre_ref, zim_ref):
    lr = jnp.minimum(lre_ref[...], -1e-4)
    li = lim_ref[...]
    dt = jnp.exp(ldt_ref[...])
    mag = jnp.exp(lr * dt)
    ang = li * dt
    ab_re = mag * jnp.cos(ang)
    ab_im = mag * jnp.sin(ang)
    den = lr * lr + li * li
    nr = ab_re - 1.0
    are_ref[...] = ab_re
    aim_ref[...] = ab_im
    zre_ref[...] = (nr * lr + ab_im * li) / den
    zim_ref[...] = (ab_im * lr - nr * li) / den


def _s5_scale_b_kernel(zre_ref, zim_ref, bre_ref, bim_ref, ore_ref, oim_ref):
    zr = zre_ref[...]
    zi = zim_ref[...]
    br = bre_ref[...]
    bi = bim_ref[...]
    ore_ref[...] = zr * br - zi * bi
    oim_ref[...] = zr * bi + zi * br


def _s5_params(lam_re, lam_im, log_dt, b_re, b_im, c_re, c_im):
    g, p, hh = S5_GROUPS, S5_STATE, S5_GROUP
    gp = jax.ShapeDtypeStruct((g, p), F32)
    a_re, a_im, z_re, z_im = pl.pallas_call(
        _s5_discretize_kernel, out_shape=(gp, gp, gp, gp), name="s5_discretize",
    )(lam_re.astype(F32), lam_im.astype(F32), log_dt.astype(F32).reshape(g, 1))
    gph = jax.ShapeDtypeStruct((g * p, hh), F32)
    bb_re, bb_im = pl.pallas_call(
        _s5_scale_b_kernel, out_shape=(gph, gph), name="s5_scale_b",
    )(z_re.reshape(g * p, 1), z_im.reshape(g * p, 1),
      b_re.astype(F32).reshape(g * p, hh), b_im.astype(F32).reshape(g * p, hh))
    eye = jnp.eye(S5_CB_GROUPS, dtype=F32)

    def blockdiag_in(w):
        w4 = w.reshape(S5_CB, S5_CB_GROUPS, p, hh)
        return jnp.einsum('cgph,gk->cghkp', w4, eye).reshape(S5_CB, S5_CB_IN, S5_CB_STATE)

    def blockdiag_out(w):
        w4 = w.astype(F32).reshape(S5_CB, S5_CB_GROUPS, hh, p)
        return jnp.einsum('cghp,gk->cgpkh', w4, eye).reshape(S5_CB, S5_CB_STATE, S5_CB_IN)

    bbd = jnp.concatenate([blockdiag_in(bb_re), blockdiag_in(bb_im)], axis=-1).astype(BF16)
    ccd = jnp.concatenate([blockdiag_out(c_re), -blockdiag_out(c_im)], axis=1).astype(BF16)
    a = jnp.stack([a_re.reshape(S5_CB, S5_CB_STATE), a_im.reshape(S5_CB, S5_CB_STATE)], axis=1)
    return a, bbd, ccd


def _s5_pack_state(s_re, s_im):
    nb = s_re.shape[0]
    return jnp.stack([s_re.reshape(nb, S5_CB, S5_CB_STATE), s_im.reshape(nb, S5_CB, S5_CB_STATE)],
                     axis=2).reshape(nb, 2 * S5_GROUPS * S5_STATE)


def _s5_unpack_state(s):
    nb = s.shape[0]
    s4 = s.reshape(nb, S5_CB, 2, S5_CB_STATE)
    return (s4[:, :, 0].reshape(nb, S5_GROUPS, S5_STATE), s4[:, :, 1].reshape(nb, S5_GROUPS, S5_STATE))


def _s5_kernel(x_ref, s0_ref, nw_ref, a_ref, bbd_ref, ccd_ref, d_ref, wglu_ref,
               out_ref, sfin_ref, bu_ref, st_ref, hs_ref, ys_ref, *, nb, lc):
    rows = nb * lc
    nk = D_MODEL // LANES
    lane = lambda k: slice(k * LANES, (k + 1) * LANES)

    @pl.when(pl.program_id(0) == 0)
    def _():
        st_ref[...] = s0_ref[...]

    x = x_ref[...].reshape(rows, D_MODEL)
    h = _rmsnorm(x, nw_ref[...])
    if nb <= lc:
        for b in range(nb):
            for k in range(nk):
                hs_ref[k, pl.ds(b, lc, stride=nb), :] = h[b * lc:(b + 1) * lc, lane(k)]
        hb = jnp.concatenate([hs_ref[k] for k in range(nk)], axis=-1).astype(BF16)
    else:
        for k in range(nk):
            hs_ref[k] = h[:, lane(k)]
        hb = jnp.concatenate(
            [jnp.concatenate([hs_ref[k, pl.ds(t, nb, stride=lc), :] for t in range(lc)], axis=0)
             for k in range(nk)], axis=-1).astype(BF16)
    nt = S5_CB_STATE // LANES
    for cb in range(S5_CB):
        bu = _dot(hb[:, cb * S5_CB_IN:(cb + 1) * S5_CB_IN], bbd_ref[cb])
        for k in range(2 * nt):
            bu_ref[k] = bu[:, lane(k)]
        a_re = [jnp.broadcast_to(a_ref[cb, 0:1, lane(k)], (SUBLANES, LANES)) for k in range(nt)]
        a_im = [jnp.broadcast_to(a_ref[cb, 1:2, lane(k)], (SUBLANES, LANES)) for k in range(nt)]
        col = cb * 2 * S5_CB_STATE
        for j in range(nb // SUBLANES):
            brow = j * SUBLANES

            def step(t, carry, brow=brow):
                idx = pl.ds(pl.multiple_of(t * nb + brow, SUBLANES), SUBLANES)
                new_re, new_im = [], []
                for k in range(nt):
                    s_re, s_im = carry[k], carry[nt + k]
                    n_re = a_re[k] * s_re - a_im[k] * s_im + bu_ref[k, idx, :]
                    n_im = a_re[k] * s_im + a_im[k] * s_re + bu_ref[nt + k, idx, :]
                    bu_ref[k, idx, :] = n_re
                    bu_ref[nt + k, idx, :] = n_im
                    new_re.append(n_re)
                    new_im.append(n_im)
                return tuple(new_re) + tuple(new_im)

            carry = tuple(st_ref[brow:brow + SUBLANES, col + k * LANES:col + (k + 1) * LANES]
                          for k in range(2 * nt))
            if lc <= 8:
                for t in range(lc):
                    carry = step(t, carry)
            else:
                carry = lax.fori_loop(0, lc, step, carry, unroll=4)
            for k in range(2 * nt):
                st_ref[brow:brow + SUBLANES, col + k * LANES:col + (k + 1) * LANES] = carry[k]
        st_all = jnp.concatenate([bu_ref[k] for k in range(2 * nt)], axis=-1)
        y_cb = _dot(st_all.astype(BF16), ccd_ref[cb])
        for k in range(S5_CB_IN // LANES):
            ys_ref[cb * (S5_CB_IN // LANES) + k] = y_cb[:, lane(k)]
    if nb <= lc:
        y = jnp.concatenate(
            [jnp.concatenate([ys_ref[k, pl.ds(b, lc, stride=nb), :] for b in range(nb)], axis=0)
             for k in range(nk)], axis=-1)
    else:
        for t in range(lc):
            for k in range(nk):
                hs_ref[k, pl.ds(t, nb, stride=lc), :] = ys_ref[k, t * nb:(t + 1) * nb, :]
        y = jnp.concatenate([hs_ref[k] for k in range(nk)], axis=-1)
    y = _gelu(y + d_ref[...] * h)
    z = _dot(y.astype(BF16), wglu_ref[...])
    out = x + z[:, :D_MODEL] * _sigmoid(z[:, D_MODEL:])
    out_ref[...] = out.reshape(out_ref.shape)
    sfin_ref[...] = st_ref[...]


def _s5_layer(x, s0_re, s0_im, norm_w, a, bbd, ccd, d, w_glu, *, lc):
    nb, seq, _ = x.shape
    nstate = 2 * S5_GROUPS * S5_STATE
    rows = nb * lc
    if seq == lc:
        xin = x.reshape(rows, D_MODEL)
        x_spec = pl.BlockSpec((rows, D_MODEL), lambda c: (0, 0))
    else:
        xin = x
        x_spec = pl.BlockSpec((nb, lc, D_MODEL), lambda c: (0, c, 0))
    out, sfin = pl.pallas_call(
        functools.partial(_s5_kernel, nb=nb, lc=lc),
        grid=(seq // lc,),
        in_specs=[
            x_spec,
            _const_spec((nb, nstate)),
            _const_spec((1, D_MODEL)),
            _const_spec((S5_CB, 2, S5_CB_STATE)),
            _const_spec((S5_CB, S5_CB_IN, 2 * S5_CB_STATE)),
            _const_spec((S5_CB, 2 * S5_CB_STATE, S5_CB_IN)),
            _const_spec((1, D_MODEL)),
            _const_spec((D_MODEL, 2 * D_MODEL)),
        ],
        out_specs=(x_spec, pl.BlockSpec((nb, nstate), lambda c: (0, 0))),
        out_shape=(jax.ShapeDtypeStruct(xin.shape, F32), jax.ShapeDtypeStruct((nb, nstate), F32)),
        scratch_shapes=[
            pltpu.VMEM((2 * S5_CB_STATE // LANES, rows, LANES), F32),
            pltpu.VMEM((nb, nstate), F32),
            pltpu.VMEM((D_MODEL // LANES, rows, LANES), F32),
            pltpu.VMEM((D_MODEL // LANES, rows, LANES), F32),
        ],
        compiler_params=pltpu.CompilerParams(
            dimension_semantics=("arbitrary",), vmem_limit_bytes=VMEM_LIMIT_BYTES),
        name="s5_layer",
    )(xin, _s5_pack_state(s0_re, s0_im), norm_w.reshape(1, D_MODEL), a, bbd, ccd,
      d.reshape(1, D_MODEL), w_glu)
    new_re, new_im = _s5_unpack_state(sfin)
    return out.reshape(x.shape), new_re, new_im


def _hgrn_gates(proj, lb):
    q = proj[:, 0:HG_FDIM]
    q = q * _sigmoid(q)
    fz = proj[:, HG_FDIM:2 * HG_FDIM]
    v = proj[:, 2 * HG_FDIM:2 * HG_FDIM + D_MODEL]
    g = proj[:, 2 * HG_FDIM + D_MODEL:]
    log_sig = jnp.minimum(fz, 0.0) - jnp.log1p(jnp.exp(-jnp.abs(fz)))
    t1 = jnp.log(lb)
    t2 = jnp.log1p(-lb) + log_sig
    hi = jnp.maximum(t1, t2)
    lo = jnp.minimum(t1, t2)
    logf = hi + jnp.log1p(jnp.exp(lo - hi))
    k = (1.0 - lb) * _sigmoid(-fz)
    return q, k, v, g, logf


def _hgrn_tile(q, k, v, g, logf, st_refs, nw, *, chunk):
    rows = q.shape[0]
    nseg = rows // chunk
    ri = lax.broadcasted_iota(jnp.int32, (rows, rows), 0)
    ci = lax.broadcasted_iota(jnp.int32, (rows, rows), 1)
    if nseg == 1:
        same = ci <= ri
    else:
        same = (ci <= ri) & ((ri // chunk) == (ci // chunk))
    tri = jnp.where(same, 1.0, 0.0).astype(F32)
    b = jnp.dot(tri, logf, precision=lax.Precision.HIGHEST, preferred_element_type=F32)
    rowid = lax.broadcasted_iota(jnp.int32, (rows, 1), 0)
    b_mid = b[chunk // 2:chunk // 2 + 1, :]
    b_last = b[chunk - 1:chunk, :]
    for s in range(1, nseg):
        in_s = rowid >= s * chunk
        b_mid = jnp.where(in_s, b[s * chunk + chunk // 2:s * chunk + chunk // 2 + 1, :], b_mid)
        b_last = jnp.where(in_s, b[s * chunk + chunk - 1:s * chunk + chunk, :], b_last)
    qs = (q * jnp.exp(b - b_mid)).astype(BF16)
    ks = (k * jnp.exp(b_mid - b)).astype(BF16)
    qi = (q * jnp.exp(b)).astype(BF16)
    kd = k * jnp.exp(b_last - b)
    vb = v.astype(BF16)
    outs = []
    for hd in range(HG_HEADS):
        sl = slice(hd * HG_DK, (hd + 1) * HG_DK)
        scores = jnp.where(same, _dot_nt(qs[:, sl], ks[:, sl]), 0.0)
        o = _dot(scores.astype(BF16), vb[:, sl])
        v_t = v[:, sl].T
        for s in range(nseg):
            st = st_refs[s][hd]
            o_s = _dot_nt(qi[:, sl], st.astype(BF16))
            kd_s = kd[:, sl]
            if nseg > 1:
                in_s = (rowid >= s * chunk) & (rowid < (s + 1) * chunk)
                o_s = jnp.where(in_s, o_s, 0.0)
                kd_s = jnp.where(in_s, kd_s, 0.0)
            o = o + o_s
            dec = jnp.exp(b[s * chunk + chunk - 1:s * chunk + chunk, sl])
            st_refs[s][hd] = dec * st + _dot(v_t.astype(BF16), kd_s.astype(BF16))
        o = o * lax.rsqrt(jnp.mean(o * o, axis=-1, keepdims=True) + RMS_EPS)
        gh = g[:, sl]
        outs.append(o * nw * (gh * _sigmoid(gh)))
    return jnp.concatenate(outs, axis=-1)


def _hgrn_prompt_kernel(x_ref, s0_ref, nw_ref, win_ref, lb_ref, hnw_ref, wout_ref,
                        out_ref, sout_ref, st_ref, y_ref, *, lc, chunk):
    c = pl.program_id(1)

    @pl.when(c == 0)
    def _():
        for hd in range(HG_HEADS):
            st_ref[hd] = s0_ref[0, hd].T

    x = x_ref[0]
    hb = _rmsnorm(x, nw_ref[...]).astype(BF16)
    proj = _dot(hb, win_ref[...])
    q, k, v, g, logf = _hgrn_gates(proj, lb_ref[...])
    for sub in range(lc // chunk):
        r = slice(sub * chunk, (sub + 1) * chunk)
        y_ref[r, :] = _hgrn_tile(q[r], k[r], v[r], g[r], logf[r], [st_ref], hnw_ref[...], chunk=chunk)
    out_ref[0] = x + _dot(y_ref[...].astype(BF16), wout_ref[...])

    @pl.when(c == pl.num_programs(1) - 1)
    def _():
        for hd in range(HG_HEADS):
            sout_ref[0, hd] = st_ref[hd].T


def _hgrn_sample_kernel(x_ref, s0_ref, nw_ref, win_ref, lb_ref, hnw_ref, wout_ref,
                        out_ref, sout_ref, st_ref, proj_ref, y_ref, *, seq):
    p = pl.program_id(0)
    nseg = SUBLANES // seq

    @pl.when(p == 0)
    def _():
        hb = _rmsnorm(x_ref[...], nw_ref[...]).astype(BF16)
        proj_ref[...] = _dot(hb, win_ref[...])

    for s in range(nseg):
        for hd in range(HG_HEADS):
            st_ref[s, hd] = s0_ref[s, hd].T
    r = pl.ds(pl.multiple_of(p * SUBLANES, SUBLANES), SUBLANES)
    q, k, v, g, logf = _hgrn_gates(proj_ref[r, :], lb_ref[...])
    y_ref[r, :] = _hgrn_tile(q, k, v, g, logf, [st_ref.at[s] for s in range(nseg)], hnw_ref[...],
                             chunk=seq)
    for s in range(nseg):
        for hd in range(HG_HEADS):
            sout_ref[s, hd] = st_ref[s, hd].T

    @pl.when(p == pl.num_programs(0) - 1)
    def _():
        out_ref[...] = x_ref[...] + _dot(y_ref[...].astype(BF16), wout_ref[...])


def _hgrn_layer(x, s0, norm_w, w_in, lb, hg_norm_w, w_out):
    nb, seq, _ = x.shape
    weights = (norm_w.reshape(1, D_MODEL), w_in.astype(BF16), lb.reshape(1, HG_FDIM),
               hg_norm_w.reshape(1, HG_DV), w_out.astype(BF16))
    w_specs = [
        _const_spec((1, D_MODEL)),
        _const_spec((D_MODEL, 2 * HG_FDIM + 2 * D_MODEL)),
        _const_spec((1, HG_FDIM)),
        _const_spec((1, HG_DV)),
        _const_spec((D_MODEL, D_MODEL)),
    ]
    s_shape = jax.ShapeDtypeStruct(s0.shape, F32)
    if seq >= HG_CHUNK:
        chunk = HG_CHUNK
        lc = HG_CHUNKS_PER_STEP * chunk
        x_spec = pl.BlockSpec((1, lc, D_MODEL), lambda b, c: (b, c, 0))
        s_spec = pl.BlockSpec((1, HG_HEADS, HG_DK, HG_DV), lambda b, c: (b, 0, 0, 0))
        out, s_new = pl.pallas_call(
            functools.partial(_hgrn_prompt_kernel, lc=lc, chunk=chunk),
            grid=(nb, seq // lc),
            in_specs=[x_spec, s_spec] + w_specs,
            out_specs=(x_spec, s_spec),
            out_shape=(jax.ShapeDtypeStruct(x.shape, F32), s_shape),
            scratch_shapes=[
                pltpu.VMEM((HG_HEADS, HG_DV, HG_DK), F32),
                pltpu.VMEM((lc, D_MODEL), F32),
            ],
            compiler_params=pltpu.CompilerParams(
                dimension_semantics=("arbitrary", "arbitrary"), vmem_limit_bytes=VMEM_LIMIT_BYTES),
            name="hgrn_prompt",
        )(x, s0, *weights)
        return out, s_new
    assert SUBLANES % seq == 0 and nb % (SUBLANES // seq) == 0
    nseg = SUBLANES // seq
    rows = nb * seq
    x_spec = _const_spec((rows, D_MODEL))
    s_spec = pl.BlockSpec((nseg, HG_HEADS, HG_DK, HG_DV), lambda p: (p, 0, 0, 0))
    out, s_new = pl.pallas_call(
        functools.partial(_hgrn_sample_kernel, seq=seq),
        grid=(nb // nseg,),
        in_specs=[x_spec, s_spec] + w_specs,
        out_specs=(pl.BlockSpec((rows, D_MODEL), lambda p: (0, 0)), s_spec),
        out_shape=(jax.ShapeDtypeStruct((rows, D_MODEL), F32), s_shape),
        scratch_shapes=[
            pltpu.VMEM((nseg, HG_HEADS, HG_DV, HG_DK), F32),
            pltpu.VMEM((rows, 2 * HG_FDIM + 2 * D_MODEL), F32),
            pltpu.VMEM((rows, D_MODEL), F32),
        ],
        compiler_params=pltpu.CompilerParams(
            dimension_semantics=("arbitrary",), vmem_limit_bytes=VMEM_LIMIT_BYTES),
        name="hgrn_sample",
    )(x.reshape(rows, D_MODEL), s0, *weights)
    return out.reshape(x.shape), s_new


def _compare_exchange(v, i, j):
    hi = jnp.maximum(v[i], v[j])
    lo = jnp.minimum(v[i], v[j])
    v[i] = hi
    v[j] = lo


def _bitonic_merge_desc(v):
    n = len(v)
    j = n // 2
    while j >= 1:
        for i in range(n):
            if i & j == 0:
                _compare_exchange(v, i, i + j)
        j //= 2


_SORT16 = (
    (0, 13), (1, 12), (2, 15), (3, 14), (4, 8), (5, 6), (7, 11), (9, 10),
    (0, 5), (1, 7), (2, 9), (3, 4), (6, 13), (8, 14), (10, 15), (11, 12),
    (0, 1), (2, 3), (4, 5), (6, 8), (7, 9), (10, 11), (12, 13), (14, 15),
    (0, 2), (1, 3), (4, 10), (5, 11), (6, 7), (8, 9), (12, 14), (13, 15),
    (1, 2), (3, 12), (4, 6), (5, 7), (8, 10), (9, 11), (13, 14),
    (1, 4), (2, 6), (5, 8), (7, 10), (9, 13), (11, 14),
    (2, 4), (3, 6), (9, 12), (11, 13),
    (3, 5), (6, 8), (7, 9), (10, 12),
    (3, 4), (5, 6), (7, 8), (9, 10), (11, 12),
    (6, 7), (8, 9),
)


def _sort_desc(v):
    assert len(v) == PEER_TOPK
    for i, j in _SORT16:
        _compare_exchange(v, i, j)
    return v


def _merge_top(a, b):
    n = len(a)
    v = [jnp.maximum(a[r], b[n - 1 - r]) for r in range(n)]
    _bitonic_merge_desc(v)
    return v


def _top16_rows(s):
    v = [s[SUBLANES * j:SUBLANES * (j + 1), :] for j in range(PEER_NKEYS // SUBLANES)]
    v = _sort_desc(v)
    for shift in (1, 2, 4):
        v = _merge_top(v, [pltpu.roll(x, shift, axis=0) for x in v])
    return v


def _candidate_top(a, b):
    n = PEER_TOPK
    low = jnp.full(a[0].shape, -jnp.inf, F32)
    runs = [
        [a[i] + b[0] for i in range(1, n)],
        [a[1] + b[j] for j in range(1, 8)],
        [a[2] + b[j] for j in range(1, 5)],
        [a[3] + b[j] for j in range(1, 4)],
        [a[4] + b[j] for j in range(1, 3)],
        [a[i] + b[1] for i in range(5, 8)],
    ]
    assert all((i + 1) * (j + 1) > n for i in range(1, n) for j in range(1, n)
               if not ((i == 1 and j < 8) or (i == 2 and j < 5) or (i == 3 and j < 4)
                       or (i == 4 and j < 3) or (5 <= i < 8 and j == 1)))
    top = [a[0] + b[j] for j in range(n)]
    for run in runs:
        top = _merge_top(top, run + [low] * (n - len(run)))
    return top


def _staircase_cells():
    return [(i, j) for i in range(PEER_TOPK) for j in range(PEER_TOPK) if (i + 1) * (j + 1) <= PEER_TOPK]


def _staircase_count(a, b, thr):
    n = None
    for i, j in _staircase_cells():
        hit = jnp.where(a[i] + b[j] >= thr, 1.0, 0.0)
        n = hit if n is None else n + hit
    return n


def _peer_select_kernel(x_ref, nw_ref, wq_ref, k1_ref, k2_ref,
                        hb_ref, s1_ref, s2_ref, thr_ref, tie_ref, t1_ref, t2_ref):
    hb = _rmsnorm(x_ref[...], nw_ref[...]).astype(BF16)
    hb_ref[...] = hb
    qb = _dot(hb, wq_ref[...]).astype(BF16)
    k1 = k1_ref[...]
    k2 = k2_ref[...]
    log2e = 1.0 / math.log(2.0)
    tied = None
    for hd in range(PEER_HEADS):
        base = hd * PEER_DKEY
        s1 = _dot_nt(k1, qb[:, base:base + PEER_HALF]) * log2e
        s2 = _dot_nt(k2, qb[:, base + PEER_HALF:base + PEER_DKEY]) * log2e
        s1_ref[hd] = s1
        s2_ref[hd] = s2
        for s, t_ref in ((s1, t1_ref), (s2, t2_ref)):
            top = _top16_rows(s)
            for r in range(PEER_TOPK):
                t_ref[r, hd:hd + 1, :] = top[r][0:1, :]
            n_top = jnp.sum(jnp.where(s >= top[PEER_TOPK - 1][0:1, :], 1.0, 0.0), axis=0, keepdims=True)
            over = jnp.where(n_top > PEER_TOPK, 1.0, 0.0)
            tied = over if tied is None else jnp.maximum(tied, over)
    a = [t1_ref[r] for r in range(PEER_TOPK)]
    b = [t2_ref[r] for r in range(PEER_TOPK)]
    top = _candidate_top(a, b)
    z = jnp.exp2(top[0] - top[0])
    for r in range(1, PEER_TOPK):
        z = z + jnp.exp2(top[r] - top[0])
    shift = top[0] + jnp.log2(z) + 1.0
    for hd in range(PEER_HEADS):
        s1_ref[hd] = s1_ref[hd] - shift[hd:hd + 1, :]
    a_shifted = [v - shift for v in a]
    thr = _candidate_top(a_shifted, b)[PEER_TOPK - 1]
    thr_ref[...] = thr
    n_sel = jnp.maximum(_staircase_count(a, b, top[PEER_TOPK - 1]), _staircase_count(a_shifted, b, thr))
    over = jnp.max(jnp.where(n_sel > PEER_TOPK, 1.0, 0.0), axis=0, keepdims=True)
    flag = jnp.max(jnp.maximum(tied, over), axis=1, keepdims=True)
    tie_ref[0] = jnp.broadcast_to(flag, tie_ref.shape[1:])


def _ordered_rank(s_ref):
    s = s_ref[...]
    rowid = lax.broadcasted_iota(jnp.int32, (s.shape[0], 1), 0)

    def body(j, rank):
        sj = s_ref[pl.ds(j, 1), :]
        before = (sj > s) | ((sj == s) & (rowid > j))
        return rank + jnp.where(before, 1.0, 0.0)

    return lax.fori_loop(0, s.shape[0], body, jnp.zeros(s.shape, F32))


def _peer_select_ordered_kernel(x_ref, nw_ref, wq_ref, k1_ref, k2_ref,
                                s1_ref, c1_ref, s2_ref, r2_ref, t1_ref, t2_ref, r1_ref, s_ref):
    hb = _rmsnorm(x_ref[...], nw_ref[...]).astype(BF16)
    qb = _dot(hb, wq_ref[...]).astype(BF16)
    k1 = k1_ref[...]
    k2 = k2_ref[...]
    log2e = 1.0 / math.log(2.0)
    for hd in range(PEER_HEADS):
        base = hd * PEER_DKEY
        s1 = _dot_nt(k1, qb[:, base:base + PEER_HALF]) * log2e
        s2 = _dot_nt(k2, qb[:, base + PEER_HALF:base + PEER_DKEY]) * log2e
        s1_ref[hd] = s1
        s2_ref[hd] = s2
        for s, t_ref, rank_ref in ((s1, t1_ref, r1_ref), (s2, t2_ref, r2_ref)):
            s_ref[...] = s
            rank = _ordered_rank(s_ref)
            rank_ref[hd] = jnp.minimum(rank, float(PEER_TOPK))
            for r in range(PEER_TOPK):
                t_ref[r, hd:hd + 1, :] = jnp.sum(jnp.where(rank == r, s, 0.0), axis=0, keepdims=True)
    a = [t1_ref[r] for r in range(PEER_TOPK)]
    b = [t2_ref[r] for r in range(PEER_TOPK)]
    cells = _staircase_cells()
    sums = [a[i] + b[j] for i, j in cells]
    ahead = [None] * len(cells)

    def bump(k, hit):
        ahead[k] = hit if ahead[k] is None else ahead[k] + hit

    for p in range(len(cells)):
        for q in range(p + 1, len(cells)):
            bump(q, jnp.where(sums[p] >= sums[q], 1.0, 0.0))
            bump(p, jnp.where(sums[q] > sums[p], 1.0, 0.0))
    chosen = [jnp.zeros_like(sums[0]) + 1.0] + [jnp.where(n < PEER_TOPK, 1.0, 0.0) for n in ahead[1:]]
    z = None
    count = [None] * PEER_TOPK
    for (i, j), sm, ch in zip(cells, sums, chosen):
        term = ch * jnp.exp2(sm - sums[0])
        z = term if z is None else z + term
        count[i] = ch if count[i] is None else count[i] + ch
    shift = sums[0] + jnp.log2(z) + 1.0
    for hd in range(PEER_HEADS):
        row = slice(hd, hd + 1)
        s1_ref[hd] = s1_ref[hd] - shift[row, :]
        rank1 = r1_ref[hd]
        c1 = jnp.zeros(rank1.shape, F32)
        for r in range(PEER_TOPK):
            c1 = jnp.where(rank1 == r, count[r][row, :], c1)
        c1_ref[hd] = c1


def _peer_expert_kernel(hb_ref, x_ref, *refs, tb, eb, tw, final_norm, ordered):
    if ordered:
        s1_ref, c1_ref, s2_ref, r2_ref = refs[:4]
        refs = refs[4:]
    else:
        s1_ref, s2_ref, thr_ref = refs[:3]
        refs = refs[3:]
    u_ref, vp_ref, v_ref, fw_ref, out_ref, acc_ref = refs[:6]
    tile_refs = refs[6:]
    e = pl.program_id(1)
    ntile = tb // tw
    ht_refs, at_refs = tile_refs[:ntile], tile_refs[ntile:]

    @pl.when(e == 0)
    def _():
        acc_ref[...] = jnp.zeros_like(acc_ref)
        for at_ref in at_refs:
            at_ref[...] = jnp.zeros_like(at_ref)

    u = u_ref[...]
    vp = vp_ref[...]
    for i in range(ntile):
        ht_refs[i][...] = _dot_nt(u, hb_ref[i * tw:(i + 1) * tw, :])
        acc_ref[:, i * tw:(i + 1) * tw] += _dot_tn(vp, at_refs[i][...])
    for i in range(ntile):
        for lt in range(tw // LANES):
            lcol = slice(lt * LANES, (lt + 1) * LANES)
            col = slice(i * tw + lt * LANES, i * tw + (lt + 1) * LANES)
            for r in range(eb // PEER_NKEYS):
                gate = None
                for hd in range(PEER_HEADS):
                    t = s2_ref[hd, :, col] + s1_ref[hd, r:r + 1, col]
                    if ordered:
                        picked = r2_ref[hd, :, col] < c1_ref[hd, r:r + 1, col]
                    else:
                        picked = t >= thr_ref[hd:hd + 1, col]
                    g = jnp.where(picked, jnp.exp2(t), 0.0)
                    gate = g if gate is None else gate + g
                rr = slice(r * PEER_NKEYS, (r + 1) * PEER_NKEYS)
                h = ht_refs[i][rr, lcol]
                at_refs[i][rr, lcol] = (h * (1.0 + lax.erf(h * (1.0 / math.sqrt(2.0)))) * gate).astype(BF16)

    @pl.when(e == pl.num_programs(1) - 1)
    def _():
        v = v_ref[...]
        for i in range(ntile):
            acc_ref[:, i * tw:(i + 1) * tw] += _dot_tn(v, at_refs[i][...])
        y = x_ref[...] + acc_ref[...].T
        if final_norm:
            y = _rmsnorm(y, fw_ref[...])
        out_ref[...] = y


def _peer_layer(x, norm_w, w_q, keys1, keys2, u_tab, v_tab, final_w, *, layer, final_norm):
    t_tot = x.shape[0]
    tbs = min(PEER_SELECT_TOKENS, t_tot)
    tb = min(PEER_EXPERT_TOKENS, t_tot)
    eb = PEER_EXPERT_BLOCK
    tw = PEER_TOKEN_TILE
    nsel = t_tot // tbs
    select_in = (x, norm_w.reshape(1, D_MODEL), w_q.astype(BF16), keys1.astype(BF16), keys2.astype(BF16))
    select_specs = [
        pl.BlockSpec((tbs, D_MODEL), lambda i: (i, 0)),
        _const_spec((1, D_MODEL)),
        _const_spec((D_MODEL, PEER_HEADS * PEER_DKEY)),
        _const_spec((PEER_NKEYS, PEER_HALF)),
        _const_spec((PEER_NKEYS, PEER_HALF)),
    ]
    select_params = pltpu.CompilerParams(dimension_semantics=("arbitrary",), vmem_limit_bytes=VMEM_LIMIT_BYTES)
    key_spec = pl.BlockSpec((PEER_HEADS, PEER_NKEYS, tbs), lambda i: (0, 0, i))
    key_f32 = jax.ShapeDtypeStruct((PEER_HEADS, PEER_NKEYS, t_tot), F32)
    top_scratch = pltpu.VMEM((PEER_TOPK, PEER_HEADS, tbs), F32)
    hb, s1, s2, thr, tie = pl.pallas_call(
        _peer_select_kernel,
        grid=(nsel,),
        in_specs=select_specs,
        out_specs=(pl.BlockSpec((tbs, D_MODEL), lambda i: (i, 0)), key_spec, key_spec,
                   pl.BlockSpec((PEER_HEADS, tbs), lambda i: (0, i)),
                   pl.BlockSpec((1, SUBLANES, LANES), lambda i: (i, 0, 0))),
        out_shape=(jax.ShapeDtypeStruct((t_tot, D_MODEL), BF16), key_f32, key_f32,
                   jax.ShapeDtypeStruct((PEER_HEADS, t_tot), F32),
                   jax.ShapeDtypeStruct((nsel, SUBLANES, LANES), F32)),
        scratch_shapes=[top_scratch, top_scratch],
        compiler_params=select_params,
        name="peer_select",
    )(*select_in)

    tok = lambda i, e: (i, 0)
    once = pl.Buffered(1)
    row_spec = pl.BlockSpec((PEER_HEADS, eb // PEER_NKEYS, tb), lambda i, e: (0, e, i))
    all_spec = pl.BlockSpec((PEER_HEADS, PEER_NKEYS, tb), lambda i, e: (0, 0, i), pipeline_mode=once)

    def experts(ordered, *selection):
        if ordered:
            sel_specs = [row_spec, row_spec, all_spec, all_spec]
        else:
            sel_specs = [row_spec, all_spec,
                         pl.BlockSpec((PEER_HEADS, tb), lambda i, e: (0, i), pipeline_mode=once)]
        return pl.pallas_call(
            functools.partial(_peer_expert_kernel, tb=tb, eb=eb, tw=tw, final_norm=final_norm,
                              ordered=ordered),
            grid=(t_tot // tb, PEER_EXPERTS // eb),
            in_specs=[
                pl.BlockSpec((tb, D_MODEL), tok, pipeline_mode=once),
                pl.BlockSpec((tb, D_MODEL), tok, pipeline_mode=once),
                *sel_specs,
                pl.BlockSpec((None, eb, D_MODEL), lambda i, e: (layer, e, 0)),
                pl.BlockSpec((None, eb, D_MODEL), lambda i, e: (layer, jnp.maximum(e - 1, 0), 0)),
                pl.BlockSpec((None, eb, D_MODEL), lambda i, e: (layer, PEER_EXPERTS // eb - 1, 0),
                             pipeline_mode=once),
                _const_spec((1, D_MODEL)),
            ],
            out_specs=pl.BlockSpec((tb, D_MODEL), tok),
            out_shape=jax.ShapeDtypeStruct((t_tot, D_MODEL), F32),
            scratch_shapes=(
                [pltpu.VMEM((D_MODEL, tb), F32)]
                + [pltpu.VMEM((eb, tw), F32) for _ in range(tb // tw)]
                + [pltpu.VMEM((eb, tw), BF16) for _ in range(tb // tw)]
            ),
            compiler_params=pltpu.CompilerParams(
                dimension_semantics=("arbitrary", "arbitrary"), vmem_limit_bytes=VMEM_LIMIT_BYTES),
            name="peer_experts_ordered" if ordered else "peer_experts",
        )(hb, x, *selection, u_tab, v_tab, v_tab, final_w.reshape(1, D_MODEL))

    def with_ties():
        s1o, c1, s2o, r2 = pl.pallas_call(
            _peer_select_ordered_kernel,
            grid=(nsel,),
            in_specs=select_specs,
            out_specs=(key_spec, key_spec, key_spec, key_spec),
            out_shape=(key_f32, key_f32, key_f32, key_f32),
            scratch_shapes=[top_scratch, top_scratch,
                            pltpu.VMEM((PEER_HEADS, PEER_NKEYS, tbs), F32),
                            pltpu.VMEM((PEER_NKEYS, tbs), F32)],
            compiler_params=select_params,
            name="peer_select_ordered",
        )(*select_in)
        return experts(True, s1o, c1, s2o, r2)

    return lax.cond(jnp.max(tie) > 0.0, with_ties, lambda: experts(False, s1, s2, thr))


def _trunk(x, s5_re, s5_im, s_hg, p, s5_prm, lb_all):
    nb, seq, _ = x.shape
    new_re, new_im, new_hg = [], [], []
    for i in range(DEPTH):
        j = i // N_MIXERS
        if i % N_MIXERS == 0:
            a, bbd, ccd = s5_prm[j]
            x, sr, si = _s5_layer(x, s5_re[j], s5_im[j], p["norm_mix"][i], a, bbd, ccd,
                                  p["s5_d"][j], p["s5_w_glu"][j].astype(BF16), lc=min(seq, S5_TIME_CHUNK))
            new_re.append(sr)
            new_im.append(si)
        else:
            x, s_new = _hgrn_layer(x, s_hg[j], p["norm_mix"][i], p["hg_w_in"][j], lb_all[i],
                                   p["hg_norm_w"][j], p["hg_w_out"][j])
            new_hg.append(s_new)
        x = _peer_layer(x.reshape(nb * seq, D_MODEL), p["norm_ffn"][i], p["peer_w_q"][i],
                        p["peer_keys1"][i], p["peer_keys2"][i], p["peer_u"], p["peer_v"],
                        p["norm_final"], layer=i, final_norm=(i == DEPTH - 1)).reshape(nb, seq, D_MODEL)
    return x, jnp.stack(new_re), jnp.stack(new_im), jnp.stack(new_hg)


def kernel(x_prompt, x_sample, state_s5_re, state_s5_im, state_hgrn, norm_mix, norm_ffn, norm_final,
           s5_lambda_re, s5_lambda_im, s5_log_dt, s5_b_re, s5_b_im, s5_c_re, s5_c_im, s5_d, s5_w_glu,
           hg_w_in, hg_lower_bounds, hg_norm_w, hg_w_out, peer_w_q, peer_keys1, peer_keys2, peer_u, peer_v):
    p = dict(norm_mix=norm_mix, norm_ffn=norm_ffn, norm_final=norm_final, s5_d=s5_d, s5_w_glu=s5_w_glu,
             hg_w_in=hg_w_in, hg_norm_w=hg_norm_w, hg_w_out=hg_w_out, peer_w_q=peer_w_q,
             peer_keys1=peer_keys1, peer_keys2=peer_keys2,
             peer_u=peer_u.astype(BF16), peer_v=peer_v.astype(BF16))
    n_a = s5_lambda_re.shape[0]
    s5_prm = [_s5_params(s5_lambda_re[j], s5_lambda_im[j], s5_log_dt[j], s5_b_re[j], s5_b_im[j],
                         s5_c_re[j], s5_c_im[j]) for j in range(n_a)]
    lb_all = jnp.cumsum(jax.nn.softmax(hg_lower_bounds.astype(F32), axis=0), axis=0)
    lb_all = lb_all - lb_all[0:1]
    nbp = x_prompt.shape[0]
    z_s5 = jnp.zeros((n_a, nbp, S5_GROUPS, S5_STATE), state_s5_re.dtype)
    z_hg = jnp.zeros((state_hgrn.shape[0], nbp, HG_HEADS, HG_DK, HG_DV), state_hgrn.dtype)
    y_p, re_p, im_p, hg_p = _trunk(x_prompt, z_s5, z_s5, z_hg, p, s5_prm, lb_all)
    y_s, re_s, im_s, hg_s = _trunk(x_sample, state_s5_re, state_s5_im, state_hgrn, p, s5_prm, lb_all)
    return (y_p, y_s, re_p, im_p, hg_p, re_s, im_s, hg_s)
```

```python
import functools
import math

import jax
import jax.numpy as jnp
from jax import lax
from jax.experimental import pallas as pl
from jax.experimental.pallas import tpu as pltpu

F32 = jnp.float32
BF16 = jnp.bfloat16

D_MODEL = 1024
DEPTH = 2
N_MIXERS = 2
S5_GROUP = 16
S5_GROUPS = D_MODEL // S5_GROUP
S5_STATE = 64
HG_HEADS = 8
HG_DK = 128
HG_DV = 128
HG_FDIM = HG_HEADS * HG_DK
HG_CHUNK = 64
PEER_HEADS = 8
PEER_NKEYS = 128
PEER_EXPERTS = PEER_NKEYS * PEER_NKEYS
PEER_TOPK = 16
PEER_DKEY = 256
PEER_HALF = PEER_DKEY // 2
RMS_EPS = 1e-6

LANES = 128
SUBLANES = 8
MXU_WIDTH = 256
VMEM_LIMIT_BYTES = 56 * 2**20

S5_TIME_CHUNK = 64
HG_CHUNKS_PER_STEP = 8
PEER_SELECT_TOKENS = 512
PEER_EXPERT_TOKENS = 1024
PEER_EXPERT_BLOCK = 8 * PEER_NKEYS
PEER_TOKEN_TILE = MXU_WIDTH

S5_CB = 4
S5_CB_GROUPS = S5_GROUPS // S5_CB
S5_CB_IN = S5_CB_GROUPS * S5_GROUP
S5_CB_STATE = S5_CB_GROUPS * S5_STATE


def _const_spec(shape):
    nd = len(shape)
    return pl.BlockSpec(shape, lambda *_: (0,) * nd, pipeline_mode=pl.Buffered(1))


def _rmsnorm(x, w):
    ms = jnp.mean(x * x, axis=-1, keepdims=True)
    return x * lax.rsqrt(ms + RMS_EPS) * w


def _gelu(x):
    return 0.5 * x * (1.0 + lax.erf(x * (1.0 / math.sqrt(2.0))))


def _sigmoid(x):
    return 1.0 / (1.0 + jnp.exp(-x))


def _dot(a, b):
    return jnp.dot(a, b, preferred_element_type=F32)


def _dot_nt(a, b):
    return lax.dot_general(a, b, (((1,), (1,)), ((), ())), preferred_element_type=F32)


def _dot_tn(a, b):
    return lax.dot_general(a, b, (((0,), (0,)), ((), ())), preferred_element_type=F32)


def _s5_discretize_kernel(lre_ref, lim_ref, ldt_ref, are_ref, aim_ref, zre_ref, zim_ref):
    lr = jnp.minimum(lre_ref[...], -1e-4)
    li = lim_ref[...]
    dt = jnp.exp(ldt_ref[...])
    mag = jnp.exp(lr * dt)
    ang = li * dt
    ab_re = mag * jnp.cos(ang)
    ab_im = mag * jnp.sin(ang)
    den = lr * lr + li * li
    nr = ab_re - 1.0
    are_ref[...] = ab_re
    aim_ref[...] = ab_im
    zre_ref[...] = (nr * lr + ab_im * li) / den
    zim_ref[...] = (ab_im * lr - nr * li) / den


def _s5_scale_b_kernel(zre_ref, zim_ref, bre_ref, bim_ref, ore_ref, oim_ref):
    zr = zre_ref[...]
    zi = zim_ref[...]
    br = bre_ref[...]
    bi = bim_ref[...]
    ore_ref[...] = zr * br - zi * bi
    oim_ref[...] = zr * bi + zi * br


def _s5_params(lam_re, lam_im, log_dt, b_re, b_im, c_re, c_im):
    g, p, hh = S5_GROUPS, S5_STATE, S5_GROUP
    gp = jax.ShapeDtypeStruct((g, p), F32)
    a_re, a_im, z_re, z_im = pl.pallas_call(
        _s5_discretize_kernel, out_shape=(gp, gp, gp, gp), name="s5_discretize",
    )(lam_re.astype(F32), lam_im.astype(F32), log_dt.astype(F32).reshape(g, 1))
    gph = jax.ShapeDtypeStruct((g * p, hh), F32)
    bb_re, bb_im = pl.pallas_call(
        _s5_scale_b_kernel, out_shape=(gph, gph), name="s5_scale_b",
    )(z_re.reshape(g * p, 1), z_im.reshape(g * p, 1),
      b_re.astype(F32).reshape(g * p, hh), b_im.astype(F32).reshape(g * p, hh))
    eye = jnp.eye(S5_CB_GROUPS, dtype=F32)

    def blockdiag_in(w):
        w4 = w.reshape(S5_CB, S5_CB_GROUPS, p, hh)
        return jnp.einsum('cgph,gk->cghkp', w4, eye).reshape(S5_CB, S5_CB_IN, S5_CB_STATE)

    def blockdiag_out(w):
        w4 = w.astype(F32).reshape(S5_CB, S5_CB_GROUPS, hh, p)
        return jnp.einsum('cghp,gk->cgpkh', w4, eye).reshape(S5_CB, S5_CB_STATE, S5_CB_IN)

    bbd = jnp.concatenate([blockdiag_in(bb_re), blockdiag_in(bb_im)], axis=-1).astype(BF16)
    ccd = jnp.concatenate([blockdiag_out(c_re), -blockdiag_out(c_im)], axis=1).astype(BF16)
    a = jnp.stack([a_re.reshape(S5_CB, S5_CB_STATE), a_im.reshape(S5_CB, S5_CB_STATE)], axis=1)
    return a, bbd, ccd


def _s5_pack_state(s_re, s_im):
    nb = s_re.shape[0]
    return jnp.stack([s_re.reshape(nb, S5_CB, S5_CB_STATE), s_im.reshape(nb, S5_CB, S5_CB_STATE)],
                     axis=2).reshape(nb, 2 * S5_GROUPS * S5_STATE)


def _s5_unpack_state(s):
    nb = s.shape[0]
    s4 = s.reshape(nb, S5_CB, 2, S5_CB_STATE)
    return (s4[:, :, 0].reshape(nb, S5_GROUPS, S5_STATE), s4[:, :, 1].reshape(nb, S5_GROUPS, S5_STATE))


def _s5_kernel(x_ref, s0_ref, nw_ref, a_ref, bbd_ref, ccd_ref, d_ref, wglu_ref,
               out_ref, sfin_ref, bu_ref, st_ref, hs_ref, ys_ref, *, nb, lc):
    rows = nb * lc
    nk = D_MODEL // LANES
    lane = lambda k: slice(k * LANES, (k + 1) * LANES)

    @pl.when(pl.program_id(0) == 0)
    def _():
        st_ref[...] = s0_ref[...]

    x = x_ref[...].reshape(rows, D_MODEL)
    h = _rmsnorm(x, nw_ref[...])
    if nb <= lc:
        for b in range(nb):
            for k in range(nk):
                hs_ref[k, pl.ds(b, lc, stride=nb), :] = h[b * lc:(b + 1) * lc, lane(k)]
        hb = jnp.concatenate([hs_ref[k] for k in range(nk)], axis=-1).astype(BF16)
    else:
        for k in range(nk):
            hs_ref[k] = h[:, lane(k)]
        hb = jnp.concatenate(
            [jnp.concatenate([hs_ref[k, pl.ds(t, nb, stride=lc), :] for t in range(lc)], axis=0)
             for k in range(nk)], axis=-1).astype(BF16)
    nt = S5_CB_STATE // LANES
    for cb in range(S5_CB):
        bu = _dot(hb[:, cb * S5_CB_IN:(cb + 1) * S5_CB_IN], bbd_ref[cb])
        for k in range(2 * nt):
            bu_ref[k] = bu[:, lane(k)]
        a_re = [jnp.broadcast_to(a_ref[cb, 0:1, lane(k)], (SUBLANES, LANES)) for k in range(nt)]
        a_im = [jnp.broadcast_to(a_ref[cb, 1:2, lane(k)], (SUBLANES, LANES)) for k in range(nt)]
        col = cb * 2 * S5_CB_STATE
        for j in range(nb // SUBLANES):
            brow = j * SUBLANES

            def step(t, carry, brow=brow):
                idx = pl.ds(pl.multiple_of(t * nb + brow, SUBLANES), SUBLANES)
                new_re, new_im = [], []
                for k in range(nt):
                    s_re, s_im = carry[k], carry[nt + k]
                    n_re = a_re[k] * s_re - a_im[k] * s_im + bu_ref[k, idx, :]
                    n_im = a_re[k] * s_im + a_im[k] * s_re + bu_ref[nt + k, idx, :]
                    bu_ref[k, idx, :] = n_re
                    bu_ref[nt + k, idx, :] = n_im
                    new_re.append(n_re)
                    new_im.append(n_im)
                return tuple(new_re) + tuple(new_im)

            carry = tuple(st_ref[brow:brow + SUBLANES, col + k * LANES:col + (k + 1) * LANES]
                          for k in range(2 * nt))
            if lc <= 8:
                for t in range(lc):
                    carry = step(t, carry)
            else:
                carry = lax.fori_loop(0, lc, step, carry, unroll=4)
            for k in range(2 * nt):
                st_ref[brow:brow + SUBLANES, col + k * LANES:col + (k + 1) * LANES] = carry[k]
        st_all = jnp.concatenate([bu_ref[k] for k in range(2 * nt)], axis=-1)
        y_cb = _dot(st_all.astype(BF16), ccd_ref[cb])
        for k in range(S5_CB_IN // LANES):
            ys_ref[cb * (S5_CB_IN // LANES) + k] = y_cb[:, lane(k)]
    if nb <= lc:
        y = jnp.concatenate(
            [jnp.concatenate([ys_ref[k, pl.ds(b, lc, stride=nb), :] for b in range(nb)], axis=0)
             for k in range(nk)], axis=-1)
    else:
        for t in range(lc):
            for k in range(nk):
                hs_ref[k, pl.ds(t, nb, stride=lc), :] = ys_ref[k, t * nb:(t + 1) * nb, :]
        y = jnp.concatenate([hs_ref[k] for k in range(nk)], axis=-1)
    y = _gelu(y + d_ref[...] * h)
    z = _dot(y.astype(BF16), wglu_ref[...])
    out = x + z[:, :D_MODEL] * _sigmoid(z[:, D_MODEL:])
    out_ref[...] = out.reshape(out_ref.shape)
    sfin_ref[...] = st_ref[...]


def _s5_layer(x, s0_re, s0_im, norm_w, a, bbd, ccd, d, w_glu, *, lc):
    nb, seq, _ = x.shape
    nstate = 2 * S5_GROUPS * S5_STATE
    rows = nb * lc
    if seq == lc:
        xin = x.reshape(rows, D_MODEL)
        x_spec = pl.BlockSpec((rows, D_MODEL), lambda c: (0, 0))
    else:
        xin = x
        x_spec = pl.BlockSpec((nb, lc, D_MODEL), lambda c: (0, c, 0))
    out, sfin = pl.pallas_call(
        functools.partial(_s5_kernel, nb=nb, lc=lc),
        grid=(seq // lc,),
        in_specs=[
            x_spec,
            _const_spec((nb, nstate)),
            _const_spec((1, D_MODEL)),
            _const_spec((S5_CB, 2, S5_CB_STATE)),
            _const_spec((S5_CB, S5_CB_IN, 2 * S5_CB_STATE)),
            _const_spec((S5_CB, 2 * S5_CB_STATE, S5_CB_IN)),
            _const_spec((1, D_MODEL)),
            _const_spec((D_MODEL, 2 * D_MODEL)),
        ],
        out_specs=(x_spec, pl.BlockSpec((nb, nstate), lambda c: (0, 0))),
        out_shape=(jax.ShapeDtypeStruct(xin.shape, F32), jax.ShapeDtypeStruct((nb, nstate), F32)),
        scratch_shapes=[
            pltpu.VMEM((2 * S5_CB_STATE // LANES, rows, LANES), F32),
            pltpu.VMEM((nb, nstate), F32),
            pltpu.VMEM((D_MODEL // LANES, rows, LANES), F32),
            pltpu.VMEM((D_MODEL // LANES, rows, LANES), F32),
        ],
        compiler_params=pltpu.CompilerParams(
            dimension_semantics=("arbitrary",), vmem_limit_bytes=VMEM_LIMIT_BYTES),
        name="s5_layer",
    )(xin, _s5_pack_state(s0_re, s0_im), norm_w.reshape(1, D_MODEL), a, bbd, ccd,
      d.reshape(1, D_MODEL), w_glu)
    new_re, new_im = _s5_unpack_state(sfin)
    return out.reshape(x.shape), new_re, new_im


def _hgrn_gates(proj, lb):
    q = proj[:, 0:HG_FDIM]
    q = q * _sigmoid(q)
    fz = proj[:, HG_FDIM:2 * HG_FDIM]
    v = proj[:, 2 * HG_FDIM:2 * HG_FDIM + D_MODEL]
    g = proj[:, 2 * HG_FDIM + D_MODEL:]
    log_sig = jnp.minimum(fz, 0.0) - jnp.log1p(jnp.exp(-jnp.abs(fz)))
    t1 = jnp.log(lb)
    t2 = jnp.log1p(-lb) + log_sig
    hi = jnp.maximum(t1, t2)
    lo = jnp.minimum(t1, t2)
    logf = hi + jnp.log1p(jnp.exp(lo - hi))
    k = (1.0 - lb) * _sigmoid(-fz)
    return q, k, v, g, logf


def _hgrn_tile(q, k, v, g, logf, st_refs, nw, *, chunk):
    rows = q.shape[0]
    nseg = rows // chunk
    ri = lax.broadcasted_iota(jnp.int32, (rows, rows), 0)
    ci = lax.broadcasted_iota(jnp.int32, (rows, rows), 1)
    if nseg == 1:
        same = ci <= ri
    else:
        same = (ci <= ri) & ((ri // chunk) == (ci // chunk))
    tri = jnp.where(same, 1.0, 0.0).astype(F32)
    b = jnp.dot(tri, logf, precision=lax.Precision.HIGHEST, preferred_element_type=F32)
    rowid = lax.broadcasted_iota(jnp.int32, (rows, 1), 0)
    b_mid = b[chunk // 2:chunk // 2 + 1, :]
    b_last = b[chunk - 1:chunk, :]
    for s in range(1, nseg):
        in_s = rowid >= s * chunk
        b_mid = jnp.where(in_s, b[s * chunk + chunk // 2:s * chunk + chunk // 2 + 1, :], b_mid)
        b_last = jnp.where(in_s, b[s * chunk + chunk - 1:s * chunk + chunk, :], b_last)
    qs = (q * jnp.exp(b - b_mid)).astype(BF16)
    ks = (k * jnp.exp(b_mid - b)).astype(BF16)
    qi = (q * jnp.exp(b)).astype(BF16)
    kd = k * jnp.exp(b_last - b)
    vb = v.astype(BF16)
    outs = []
    for hd in range(HG_HEADS):
        sl = slice(hd * HG_DK, (hd + 1) * HG_DK)
        scores = jnp.where(same, _dot_nt(qs[:, sl], ks[:, sl]), 0.0)
        o = _dot(scores.astype(BF16), vb[:, sl])
        v_t = v[:, sl].T
        for s in range(nseg):
            st = st_refs[s][hd]
            o_s = _dot_nt(qi[:, sl], st.astype(BF16))
            kd_s = kd[:, sl]
            if nseg > 1:
                in_s = (rowid >= s * chunk) & (rowid < (s + 1) * chunk)
                o_s = jnp.where(in_s, o_s, 0.0)
                kd_s = jnp.where(in_s, kd_s, 0.0)
            o = o + o_s
            dec = jnp.exp(b[s * chunk + chunk - 1:s * chunk + chunk, sl])
            st_refs[s][hd] = dec * st + _dot(v_t.astype(BF16), kd_s.astype(BF16))
        o = o * lax.rsqrt(jnp.mean(o * o, axis=-1, keepdims=True) + RMS_EPS)
        gh = g[:, sl]
        outs.append(o * nw * (gh * _sigmoid(gh)))
    return jnp.concatenate(outs, axis=-1)


def _hgrn_prompt_kernel(x_ref, s0_ref, nw_ref, win_ref, lb_ref, hnw_ref, wout_ref,
                        out_ref, sout_ref, st_ref, y_ref, *, lc, chunk):
    c = pl.program_id(1)

    @pl.when(c == 0)
    def _():
        for hd in range(HG_HEADS):
            st_ref[hd] = s0_ref[0, hd].T

    x = x_ref[0]
    hb = _rmsnorm(x, nw_ref[...]).astype(BF16)
    proj = _dot(hb, win_ref[...])
    q, k, v, g, logf = _hgrn_gates(proj, lb_ref[...])
    for sub in range(lc // chunk):
        r = slice(sub * chunk, (sub + 1) * chunk)
        y_ref[r, :] = _hgrn_tile(q[r], k[r], v[r], g[r], logf[r], [st_ref], hnw_ref[...], chunk=chunk)
    out_ref[0] = x + _dot(y_ref[...].astype(BF16), wout_ref[...])

    @pl.when(c == pl.num_programs(1) - 1)
    def _():
        for hd in range(HG_HEADS):
            sout_ref[0, hd] = st_ref[hd].T


def _hgrn_sample_kernel(x_ref, s0_ref, nw_ref, win_ref, lb_ref, hnw_ref, wout_ref,
                        out_ref, sout_ref, st_ref, proj_ref, y_ref, *, seq):
    p = pl.program_id(0)
    nseg = SUBLANES // seq

    @pl.when(p == 0)
    def _():
        hb = _rmsnorm(x_ref[...], nw_ref[...]).astype(BF16)
        proj_ref[...] = _dot(hb, win_ref[...])

    for s in range(nseg):
        for hd in range(HG_HEADS):
            st_ref[s, hd] = s0_ref[s, hd].T
    r = pl.ds(pl.multiple_of(p * SUBLANES, SUBLANES), SUBLANES)
    q, k, v, g, logf = _hgrn_gates(proj_ref[r, :], lb_ref[...])
    y_ref[r, :] = _hgrn_tile(q, k, v, g, logf, [st_ref.at[s] for s in range(nseg)], hnw_ref[...],
                             chunk=seq)
    for s in range(nseg):
        for hd in range(HG_HEADS):
            sout_ref[s, hd] = st_ref[s, hd].T

    @pl.when(p == pl.num_programs(0) - 1)
    def _():
        out_ref[...] = x_ref[...] + _dot(y_ref[...].astype(BF16), wout_ref[...])


def _hgrn_layer(x, s0, norm_w, w_in, lb, hg_norm_w, w_out):
    nb, seq, _ = x.shape
    weights = (norm_w.reshape(1, D_MODEL), w_in.astype(BF16), lb.reshape(1, HG_FDIM),
               hg_norm_w.reshape(1, HG_DV), w_out.astype(BF16))
    w_specs = [
        _const_spec((1, D_MODEL)),
        _const_spec((D_MODEL, 2 * HG_FDIM + 2 * D_MODEL)),
        _const_spec((1, HG_FDIM)),
        _const_spec((1, HG_DV)),
        _const_spec((D_MODEL, D_MODEL)),
    ]
    s_shape = jax.ShapeDtypeStruct(s0.shape, F32)
    if seq >= HG_CHUNK:
        chunk = HG_CHUNK
        lc = HG_CHUNKS_PER_STEP * chunk
        x_spec = pl.BlockSpec((1, lc, D_MODEL), lambda b, c: (b, c, 0))
        s_spec = pl.BlockSpec((1, HG_HEADS, HG_DK, HG_DV), lambda b, c: (b, 0, 0, 0))
        out, s_new = pl.pallas_call(
            functools.partial(_hgrn_prompt_kernel, lc=lc, chunk=chunk),
            grid=(nb, seq // lc),
            in_specs=[x_spec, s_spec] + w_specs,
            out_specs=(x_spec, s_spec),
            out_shape=(jax.ShapeDtypeStruct(x.shape, F32), s_shape),
            scratch_shapes=[
                pltpu.VMEM((HG_HEADS, HG_DV, HG_DK), F32),
                pltpu.VMEM((lc, D_MODEL), F32),
            ],
            compiler_params=pltpu.CompilerParams(
                dimension_semantics=("arbitrary", "arbitrary"), vmem_limit_bytes=VMEM_LIMIT_BYTES),
            name="hgrn_prompt",
        )(x, s0, *weights)
        return out, s_new
    assert SUBLANES % seq == 0 and nb % (SUBLANES // seq) == 0
    nseg = SUBLANES // seq
    rows = nb * seq
    x_spec = _const_spec((rows, D_MODEL))
    s_spec = pl.BlockSpec((nseg, HG_HEADS, HG_DK, HG_DV), lambda p: (p, 0, 0, 0))
    out, s_new = pl.pallas_call(
        functools.partial(_hgrn_sample_kernel, seq=seq),
        grid=(nb // nseg,),
        in_specs=[x_spec, s_spec] + w_specs,
        out_specs=(pl.BlockSpec((rows, D_MODEL), lambda p: (0, 0)), s_spec),
        out_shape=(jax.ShapeDtypeStruct((rows, D_MODEL), F32), s_shape),
        scratch_shapes=[
            pltpu.VMEM((nseg, HG_HEADS, HG_DV, HG_DK), F32),
            pltpu.VMEM((rows, 2 * HG_FDIM + 2 * D_MODEL), F32),
            pltpu.VMEM((rows, D_MODEL), F32),
        ],
        compiler_params=pltpu.CompilerParams(
            dimension_semantics=("arbitrary",), vmem_limit_bytes=VMEM_LIMIT_BYTES),
        name="hgrn_sample",
    )(x.reshape(rows, D_MODEL), s0, *weights)
    return out.reshape(x.shape), s_new


def _compare_exchange(v, i, j):
    hi = jnp.maximum(v[i], v[j])
    lo = jnp.minimum(v[i], v[j])
    v[i] = hi
    v[j] = lo


def _bitonic_merge_desc(v):
    n = len(v)
    j = n // 2
    while j >= 1:
        for i in range(n):
            if i & j == 0:
                _compare_exchange(v, i, i + j)
        j //= 2


_SORT16 = (
    (0, 13), (1, 12), (2, 15), (3, 14), (4, 8), (5, 6), (7, 11), (9, 10),
    (0, 5), (1, 7), (2, 9), (3, 4), (6, 13), (8, 14), (10, 15), (11, 12),
    (0, 1), (2, 3), (4, 5), (6, 8), (7, 9), (10, 11), (12, 13), (14, 15),
    (0, 2), (1, 3), (4, 10), (5, 11), (6, 7), (8, 9), (12, 14), (13, 15),
    (1, 2), (3, 12), (4, 6), (5, 7), (8, 10), (9, 11), (13, 14),
    (1, 4), (2, 6), (5, 8), (7, 10), (9, 13), (11, 14),
    (2, 4), (3, 6), (9, 12), (11, 13),
    (3, 5), (6, 8), (7, 9), (10, 12),
    (3, 4), (5, 6), (7, 8), (9, 10), (11, 12),
    (6, 7), (8, 9),
)


def _sort_desc(v):
    assert len(v) == PEER_TOPK
    for i, j in _SORT16:
        _compare_exchange(v, i, j)
    return v


def _merge_top(a, b):
    n = len(a)
    v = [jnp.maximum(a[r], b[n - 1 - r]) for r in range(n)]
    _bitonic_merge_desc(v)
    return v


def _top16_rows(s):
    v = [s[SUBLANES * j:SUBLANES * (j + 1), :] for j in range(PEER_NKEYS // SUBLANES)]
    v = _sort_desc(v)
    for shift in (1, 2, 4):
        v = _merge_top(v, [pltpu.roll(x, shift, axis=0) for x in v])
    return v


def _candidate_top(a, b):
    n = PEER_TOPK
    low = jnp.full(a[0].shape, -jnp.inf, F32)
    runs = [
        [a[i] + b[0] for i in range(1, n)],
        [a[1] + b[j] for j in range(1, 8)],
        [a[2] + b[j] for j in range(1, 5)],
        [a[3] + b[j] for j in range(1, 4)],
        [a[4] + b[j] for j in range(1, 3)],
        [a[i] + b[1] for i in range(5, 8)],
    ]
    assert all((i + 1) * (j + 1) > n for i in range(1, n) for j in range(1, n)
               if not ((i == 1 and j < 8) or (i == 2 and j < 5) or (i == 3 and j < 4)
                       or (i == 4 and j < 3) or (5 <= i < 8 and j == 1)))
    top = [a[0] + b[j] for j in range(n)]
    for run in runs:
        top = _merge_top(top, run + [low] * (n - len(run)))
    return top


def _staircase_cells():
    return [(i, j) for i in range(PEER_TOPK) for j in range(PEER_TOPK) if (i + 1) * (j + 1) <= PEER_TOPK]


def _staircase_count(a, b, thr):
    n = None
    for i, j in _staircase_cells():
        hit = jnp.where(a[i] + b[j] >= thr, 1.0, 0.0)
        n = hit if n is None else n + hit
    return n


def _peer_select_kernel(x_ref, nw_ref, wq_ref, k1_ref, k2_ref,
                        hb_ref, s1_ref, s2_ref, thr_ref, tie_ref, t1_ref, t2_ref):
    hb = _rmsnorm(x_ref[...], nw_ref[...]).astype(BF16)
    hb_ref[...] = hb
    qb = _dot(hb, wq_ref[...]).astype(BF16)
    k1 = k1_ref[...]
    k2 = k2_ref[...]
    log2e = 1.0 / math.log(2.0)
    tied = None
    for hd in range(PEER_HEADS):
        base = hd * PEER_DKEY
        s1 = _dot_nt(k1, qb[:, base:base + PEER_HALF]) * log2e
        s2 = _dot_nt(k2, qb[:, base + PEER_HALF:base + PEER_DKEY]) * log2e
        s1_ref[hd] = s1
        s2_ref[hd] = s2
        for s, t_ref in ((s1, t1_ref), (s2, t2_ref)):
            top = _top16_rows(s)
            for r in range(PEER_TOPK):
                t_ref[r, hd:hd + 1, :] = top[r][0:1, :]
            n_top = jnp.sum(jnp.where(s >= top[PEER_TOPK - 1][0:1, :], 1.0, 0.0), axis=0, keepdims=True)
            over = jnp.where(n_top > PEER_TOPK, 1.0, 0.0)
            tied = over if tied is None else jnp.maximum(tied, over)
    a = [t1_ref[r] for r in range(PEER_TOPK)]
    b = [t2_ref[r] for r in range(PEER_TOPK)]
    top = _candidate_top(a, b)
    z = jnp.exp2(top[0] - top[0])
    for r in range(1, PEER_TOPK):
        z = z + jnp.exp2(top[r] - top[0])
    shift = top[0] + jnp.log2(z) + 1.0
    for hd in range(PEER_HEADS):
        s1_ref[hd] = s1_ref[hd] - shift[hd:hd + 1, :]
    a_shifted = [v - shift for v in a]
    thr = _candidate_top(a_shifted, b)[PEER_TOPK - 1]
    thr_ref[...] = thr
    n_sel = jnp.maximum(_staircase_count(a, b, top[PEER_TOPK - 1]), _staircase_count(a_shifted, b, thr))
    over = jnp.max(jnp.where(n_sel > PEER_TOPK, 1.0, 0.0), axis=0, keepdims=True)
    flag = jnp.max(jnp.maximum(tied, over), axis=1, keepdims=True)
    tie_ref[0] = jnp.broadcast_to(flag, tie_ref.shape[1:])


def _ordered_rank(s_ref):
    s = s_ref[...]
    rowid = lax.broadcasted_iota(jnp.int32, (s.shape[0], 1), 0)

    def body(j, rank):
        sj = s_ref[pl.ds(j, 1), :]
        before = (sj > s) | ((sj == s) & (rowid > j))
        return rank + jnp.where(before, 1.0, 0.0)

    return lax.fori_loop(0, s.shape[0], body, jnp.zeros(s.shape, F32))


def _peer_select_ordered_kernel(tied_ref, x_ref, nw_ref, wq_ref, k1_ref, k2_ref,
                                c1_ref, r2_ref, t1_ref, t2_ref, r1_ref, s_ref):
    tied = tied_ref[pl.program_id(0)] != 0

    @pl.when(jnp.logical_not(tied))
    def _():
        c1_ref[...] = jnp.zeros_like(c1_ref)
        r2_ref[...] = jnp.zeros_like(r2_ref)

    @pl.when(tied)
    def _():
        _ordered_selection(x_ref, nw_ref, wq_ref, k1_ref, k2_ref, c1_ref, r2_ref, t1_ref, t2_ref, r1_ref, s_ref)


def _ordered_selection(x_ref, nw_ref, wq_ref, k1_ref, k2_ref, c1_ref, r2_ref, t1_ref, t2_ref, r1_ref, s_ref):
    hb = _rmsnorm(x_ref[...], nw_ref[...]).astype(BF16)
    qb = _dot(hb, wq_ref[...]).astype(BF16)
    k1 = k1_ref[...]
    k2 = k2_ref[...]
    log2e = 1.0 / math.log(2.0)
    for hd in range(PEER_HEADS):
        base = hd * PEER_DKEY
        s1 = _dot_nt(k1, qb[:, base:base + PEER_HALF]) * log2e
        s2 = _dot_nt(k2, qb[:, base + PEER_HALF:base + PEER_DKEY]) * log2e
        for s, t_ref, rank_ref in ((s1, t1_ref, r1_ref), (s2, t2_ref, r2_ref)):
            s_ref[...] = s
            rank = _ordered_rank(s_ref)
            rank_ref[hd] = jnp.minimum(rank, float(PEER_TOPK))
            for r in range(PEER_TOPK):
                t_ref[r, hd:hd + 1, :] = jnp.sum(jnp.where(rank == r, s, 0.0), axis=0, keepdims=True)
    a = [t1_ref[r] for r in range(PEER_TOPK)]
    b = [t2_ref[r] for r in range(PEER_TOPK)]
    cells = _staircase_cells()
    sums = [a[i] + b[j] for i, j in cells]
    ahead = [None] * len(cells)

    def bump(k, hit):
        ahead[k] = hit if ahead[k] is None else ahead[k] + hit

    for p in range(len(cells)):
        for q in range(p + 1, len(cells)):
            bump(q, jnp.where(sums[p] >= sums[q], 1.0, 0.0))
            bump(p, jnp.where(sums[q] > sums[p], 1.0, 0.0))
    count = [None] * PEER_TOPK
    for (i, j), n in zip(cells, ahead):
        ch = jnp.where(n < PEER_TOPK, 1.0, 0.0)
        count[i] = ch if count[i] is None else count[i] + ch
    for hd in range(PEER_HEADS):
        row = slice(hd, hd + 1)
        rank1 = r1_ref[hd]
        c1 = jnp.zeros(rank1.shape, F32)
        for r in range(PEER_TOPK):
            c1 = jnp.where(rank1 == r, count[r][row, :], c1)
        c1_ref[hd] = c1


def _peer_expert_kernel(tied_ref, hb_ref, x_ref, s1_ref, s2_ref, thr_ref, c1_ref, r2_ref,
                        u_ref, vp_ref, v_ref, fw_ref, out_ref, acc_ref, *tile_refs, tb, eb, tw, final_norm):
    e = pl.program_id(1)
    tied = tied_ref[pl.program_id(0)] != 0
    ntile = tb // tw
    ht_refs, at_refs = tile_refs[:ntile], tile_refs[ntile:]

    @pl.when(e == 0)
    def _():
        acc_ref[...] = jnp.zeros_like(acc_ref)
        for at_ref in at_refs:
            at_ref[...] = jnp.zeros_like(at_ref)

    def step(ordered):
        u = u_ref[...]
        vp = vp_ref[...]
        for i in range(ntile):
            ht_refs[i][...] = _dot_nt(u, hb_ref[i * tw:(i + 1) * tw, :])
            acc_ref[:, i * tw:(i + 1) * tw] += _dot_tn(vp, at_refs[i][...])
        for i in range(ntile):
            for lt in range(tw // LANES):
                lcol = slice(lt * LANES, (lt + 1) * LANES)
                col = slice(i * tw + lt * LANES, i * tw + (lt + 1) * LANES)
                for r in range(eb // PEER_NKEYS):
                    gate = None
                    for hd in range(PEER_HEADS):
                        t = s2_ref[hd, :, col] + s1_ref[hd, r:r + 1, col]
                        if ordered:
                            picked = r2_ref[hd, :, col] < c1_ref[hd, r:r + 1, col]
                        else:
                            picked = t >= thr_ref[hd:hd + 1, col]
                        g = jnp.where(picked, jnp.exp2(t), 0.0)
                        gate = g if gate is None else gate + g
                    rr = slice(r * PEER_NKEYS, (r + 1) * PEER_NKEYS)
                    h = ht_refs[i][rr, lcol]
                    at_refs[i][rr, lcol] = (
                        h * (1.0 + lax.erf(h * (1.0 / math.sqrt(2.0)))) * gate).astype(BF16)

    pl.when(jnp.logical_not(tied))(functools.partial(step, False))
    pl.when(tied)(functools.partial(step, True))

    @pl.when(e == pl.num_programs(1) - 1)
    def _():
        v = v_ref[...]
        for i in range(ntile):
            acc_ref[:, i * tw:(i + 1) * tw] += _dot_tn(v, at_refs[i][...])
        y = x_ref[...] + acc_ref[...].T
        if final_norm:
            y = _rmsnorm(y, fw_ref[...])
        out_ref[...] = y


def _peer_layer(x, norm_w, w_q, keys1, keys2, u_tab, v_tab, final_w, *, layer, final_norm):
    t_tot = x.shape[0]
    tbs = min(PEER_SELECT_TOKENS, t_tot)
    tb = min(PEER_EXPERT_TOKENS, t_tot)
    eb = PEER_EXPERT_BLOCK
    tw = PEER_TOKEN_TILE
    nsel = t_tot // tbs
    select_in = (x, norm_w.reshape(1, D_MODEL), w_q.astype(BF16), keys1.astype(BF16), keys2.astype(BF16))
    select_specs = [
        pl.BlockSpec((tbs, D_MODEL), lambda i: (i, 0)),
        _const_spec((1, D_MODEL)),
        _const_spec((D_MODEL, PEER_HEADS * PEER_DKEY)),
        _const_spec((PEER_NKEYS, PEER_HALF)),
        _const_spec((PEER_NKEYS, PEER_HALF)),
    ]
    select_params = pltpu.CompilerParams(dimension_semantics=("arbitrary",), vmem_limit_bytes=VMEM_LIMIT_BYTES)
    key_spec = pl.BlockSpec((PEER_HEADS, PEER_NKEYS, tbs), lambda i: (0, 0, i))
    key_f32 = jax.ShapeDtypeStruct((PEER_HEADS, PEER_NKEYS, t_tot), F32)
    top_scratch = pltpu.VMEM((PEER_TOPK, PEER_HEADS, tbs), F32)
    hb, s1, s2, thr, tie = pl.pallas_call(
        _peer_select_kernel,
        grid=(nsel,),
        in_specs=select_specs,
        out_specs=(pl.BlockSpec((tbs, D_MODEL), lambda i: (i, 0)), key_spec, key_spec,
                   pl.BlockSpec((PEER_HEADS, tbs), lambda i: (0, i)),
                   pl.BlockSpec((1, SUBLANES, LANES), lambda i: (i, 0, 0))),
        out_shape=(jax.ShapeDtypeStruct((t_tot, D_MODEL), BF16), key_f32, key_f32,
                   jax.ShapeDtypeStruct((PEER_HEADS, t_tot), F32),
                   jax.ShapeDtypeStruct((nsel, SUBLANES, LANES), F32)),
        scratch_shapes=[top_scratch, top_scratch],
        compiler_params=select_params,
        name="peer_select",
    )(*select_in)

    tied_exp = jnp.max((tie[:, 0, 0] > 0.0).astype(jnp.int32).reshape(t_tot // tb, tb // tbs), axis=1)
    tied_sel = jnp.repeat(tied_exp, tb // tbs)
    c1, r2 = pl.pallas_call(
        _peer_select_ordered_kernel,
        grid_spec=pltpu.PrefetchScalarGridSpec(
            num_scalar_prefetch=1,
            grid=(nsel,),
            in_specs=[
                pl.BlockSpec((tbs, D_MODEL), lambda i, tied: (i, 0)),
                pl.BlockSpec((1, D_MODEL), lambda i, tied: (0, 0)),
                pl.BlockSpec((D_MODEL, PEER_HEADS * PEER_DKEY), lambda i, tied: (0, 0)),
                pl.BlockSpec((PEER_NKEYS, PEER_HALF), lambda i, tied: (0, 0)),
                pl.BlockSpec((PEER_NKEYS, PEER_HALF), lambda i, tied: (0, 0)),
            ],
            out_specs=(pl.BlockSpec((PEER_HEADS, PEER_NKEYS, tbs), lambda i, tied: (0, 0, i)),
                       pl.BlockSpec((PEER_HEADS, PEER_NKEYS, tbs), lambda i, tied: (0, 0, i))),
            scratch_shapes=[top_scratch, top_scratch,
                            pltpu.VMEM((PEER_HEADS, PEER_NKEYS, tbs), F32),
                            pltpu.VMEM((PEER_NKEYS, tbs), F32)],
        ),
        out_shape=(key_f32, key_f32),
        compiler_params=select_params,
        name="peer_select_ordered",
    )(tied_sel, *select_in)

    once = pl.Buffered(1)
    tok = lambda i, e, tied: (i, 0)
    row_spec = pl.BlockSpec((PEER_HEADS, eb // PEER_NKEYS, tb), lambda i, e, tied: (0, e, i))
    all_spec = pl.BlockSpec((PEER_HEADS, PEER_NKEYS, tb), lambda i, e, tied: (0, 0, i), pipeline_mode=once)
    return pl.pallas_call(
        functools.partial(_peer_expert_kernel, tb=tb, eb=eb, tw=tw, final_norm=final_norm),
        grid_spec=pltpu.PrefetchScalarGridSpec(
            num_scalar_prefetch=1,
            grid=(t_tot // tb, PEER_EXPERTS // eb),
            in_specs=[
                pl.BlockSpec((tb, D_MODEL), tok, pipeline_mode=once),
                pl.BlockSpec((tb, D_MODEL), tok, pipeline_mode=once),
                row_spec,
                all_spec,
                pl.BlockSpec((PEER_HEADS, tb), lambda i, e, tied: (0, i), pipeline_mode=once),
                row_spec,
                all_spec,
                pl.BlockSpec((None, eb, D_MODEL), lambda i, e, tied: (layer, e, 0)),
                pl.BlockSpec((None, eb, D_MODEL), lambda i, e, tied: (layer, jnp.maximum(e - 1, 0), 0)),
                pl.BlockSpec((None, eb, D_MODEL), lambda i, e, tied: (layer, PEER_EXPERTS // eb - 1, 0),
                             pipeline_mode=once),
                pl.BlockSpec((1, D_MODEL), lambda i, e, tied: (0, 0), pipeline_mode=once),
            ],
            out_specs=pl.BlockSpec((tb, D_MODEL), tok),
            scratch_shapes=(
                [pltpu.VMEM((D_MODEL, tb), F32)]
                + [pltpu.VMEM((eb, tw), F32) for _ in range(tb // tw)]
                + [pltpu.VMEM((eb, tw), BF16) for _ in range(tb // tw)]
            ),
        ),
        out_shape=jax.ShapeDtypeStruct((t_tot, D_MODEL), F32),
        compiler_params=pltpu.CompilerParams(
            dimension_semantics=("arbitrary", "arbitrary"), vmem_limit_bytes=VMEM_LIMIT_BYTES),
        name="peer_experts",
    )(tied_exp, hb, x, s1, s2, thr, c1, r2, u_tab, v_tab, v_tab, final_w.reshape(1, D_MODEL))


def _trunk(x, s5_re, s5_im, s_hg, p, s5_prm, lb_all):
    nb, seq, _ = x.shape
    new_re, new_im, new_hg = [], [], []
    for i in range(DEPTH):
        j = i // N_MIXERS
        if i % N_MIXERS == 0:
            a, bbd, ccd = s5_prm[j]
            x, sr, si = _s5_layer(x, s5_re[j], s5_im[j], p["norm_mix"][i], a, bbd, ccd,
                                  p["s5_d"][j], p["s5_w_glu"][j].astype(BF16), lc=min(seq, S5_TIME_CHUNK))
            new_re.append(sr)
            new_im.append(si)
        else:
            x, s_new = _hgrn_layer(x, s_hg[j], p["norm_mix"][i], p["hg_w_in"][j], lb_all[i],
                                   p["hg_norm_w"][j], p["hg_w_out"][j])
            new_hg.append(s_new)
        x = _peer_layer(x.reshape(nb * seq, D_MODEL), p["norm_ffn"][i], p["peer_w_q"][i],
                        p["peer_keys1"][i], p["peer_keys2"][i], p["peer_u"], p["peer_v"],
                        p["norm_final"], layer=i, final_norm=(i == DEPTH - 1)).reshape(nb, seq, D_MODEL)
    return x, jnp.stack(new_re), jnp.stack(new_im), jnp.stack(new_hg)


def kernel(x_prompt, x_sample, state_s5_re, state_s5_im, state_hgrn, norm_mix, norm_ffn, norm_final,
           s5_lambda_re, s5_lambda_im, s5_log_dt, s5_b_re, s5_b_im, s5_c_re, s5_c_im, s5_d, s5_w_glu,
           hg_w_in, hg_lower_bounds, hg_norm_w, hg_w_out, peer_w_q, peer_keys1, peer_keys2, peer_u, peer_v):
    p = dict(norm_mix=norm_mix, norm_ffn=norm_ffn, norm_final=norm_final, s5_d=s5_d, s5_w_glu=s5_w_glu,
             hg_w_in=hg_w_in, hg_norm_w=hg_norm_w, hg_w_out=hg_w_out, peer_w_q=peer_w_q,
             peer_keys1=peer_keys1, peer_keys2=peer_keys2,
             peer_u=peer_u.astype(BF16), peer_v=peer_v.astype(BF16))
    n_a = s5_lambda_re.shape[0]
    s5_prm = [_s5_params(s5_lambda_re[j], s5_lambda_im[j], s5_log_dt[j], s5_b_re[j], s5_b_im[j],
                         s5_c_re[j], s5_c_im[j]) for j in range(n_a)]
    lb_all = jnp.cumsum(jax.nn.softmax(hg_lower_bounds.astype(F32), axis=0), axis=0)
    lb_all = lb_all - lb_all[0:1]
    nbp = x_prompt.shape[0]
    z_s5 = jnp.zeros((n_a, nbp, S5_GROUPS, S5_STATE), state_s5_re.dtype)
    z_hg = jnp.zeros((state_hgrn.shape[0], nbp, HG_HEADS, HG_DK, HG_DV), state_hgrn.dtype)
    y_p, re_p, im_p, hg_p = _trunk(x_prompt, z_s5, z_s5, z_hg, p, s5_prm, lb_all)
    y_s, re_s, im_s, hg_s = _trunk(x_sample, state_s5_re, state_s5_im, state_hgrn, p, s5_prm, lb_all)
    return (y_p, y_s, re_p, im_p, hg_p, re_s, im_s, hg_s)
```

```python
import functools
import math

import jax
import jax.numpy as jnp
from jax import lax
from jax.experimental import pallas as pl
from jax.experimental.pallas import tpu as pltpu

F32 = jnp.float32
BF16 = jnp.bfloat16

D_MODEL = 1024
DEPTH = 2
N_MIXERS = 2
S5_GROUP = 16
S5_GROUPS = D_MODEL // S5_GROUP
S5_STATE = 64
HG_HEADS = 8
HG_DK = 128
HG_DV = 128
HG_FDIM = HG_HEADS * HG_DK
HG_CHUNK = 64
PEER_HEADS = 8
PEER_NKEYS = 128
PEER_EXPERTS = PEER_NKEYS * PEER_NKEYS
PEER_TOPK = 16
PEER_DKEY = 256
PEER_HALF = PEER_DKEY // 2
RMS_EPS = 1e-6

LANES = 128
SUBLANES = 8
MXU_WIDTH = 256
VMEM_LIMIT_BYTES = 56 * 2**20

S5_TIME_CHUNK = 64
HG_CHUNKS_PER_STEP = 8
PEER_SELECT_TOKENS = 512
PEER_EXPERT_TOKENS = 1024
PEER_EXPERT_BLOCK = 8 * PEER_NKEYS
PEER_TOKEN_TILE = MXU_WIDTH

S5_CB = 4
S5_CB_GROUPS = S5_GROUPS // S5_CB
S5_CB_IN = S5_CB_GROUPS * S5_GROUP
S5_CB_STATE = S5_CB_GROUPS * S5_STATE


def _const_spec(shape):
    nd = len(shape)
    return pl.BlockSpec(shape, lambda *_: (0,) * nd, pipeline_mode=pl.Buffered(1))


def _rmsnorm(x, w):
    ms = jnp.mean(x * x, axis=-1, keepdims=True)
    return x * lax.rsqrt(ms + RMS_EPS) * w


def _gelu(x):
    return 0.5 * x * (1.0 + lax.erf(x * (1.0 / math.sqrt(2.0))))


def _sigmoid(x):
    return 1.0 / (1.0 + jnp.exp(-x))


def _dot(a, b):
    return jnp.dot(a, b, preferred_element_type=F32)


def _dot_nt(a, b):
    return lax.dot_general(a, b, (((1,), (1,)), ((), ())), preferred_element_type=F32)


def _dot_tn(a, b):
    return lax.dot_general(a, b, (((0,), (0,)), ((), ())), preferred_element_type=F32)


def _s5_discretize_kernel(lre_ref, lim_ref, ldt_ref, are_ref, aim_ref, zre_ref, zim_ref):
    lr = jnp.minimum(lre_ref[...], -1e-4)
    li = lim_ref[...]
    dt = jnp.exp(ldt_ref[...])
    mag = jnp.exp(lr * dt)
    ang = li * dt
    ab_re = mag * jnp.cos(ang)
    ab_im = mag * jnp.sin(ang)
    den = lr * lr + li * li
    nr = ab_re - 1.0
    are_ref[...] = ab_re
    aim_ref[...] = ab_im
    zre_ref[...] = (nr * lr + ab_im * li) / den
    zim_ref[...] = (ab_im * lr - nr * li) / den


def _s5_scale_b_kernel(zre_ref, zim_ref, bre_ref, bim_ref, ore_ref, oim_ref):
    zr = zre_ref[...]
    zi = zim_ref[...]
    br = bre_ref[...]
    bi = bim_ref[...]
    ore_ref[...] = zr * br - zi * bi
    oim_ref[...] = zr * bi + zi * br


def _s5_params(lam_re, lam_im, log_dt, b_re, b_im, c_re, c_im):
    g, p, hh = S5_GROUPS, S5_STATE, S5_GROUP
    gp = jax.ShapeDtypeStruct((g, p), F32)
    a_re, a_im, z_re, z_im = pl.pallas_call(
        _s5_discretize_kernel, out_shape=(gp, gp, gp, gp), name="s5_discretize",
    )(lam_re.astype(F32), lam_im.astype(F32), log_dt.astype(F32).reshape(g, 1))
    gph = jax.ShapeDtypeStruct((g * p, hh), F32)
    bb_re, bb_im = pl.pallas_call(
        _s5_scale_b_kernel, out_shape=(gph, gph), name="s5_scale_b",
    )(z_re.reshape(g * p, 1), z_im.reshape(g * p, 1),
      b_re.astype(F32).reshape(g * p, hh), b_im.astype(F32).reshape(g * p, hh))
    eye = jnp.eye(S5_CB_GROUPS, dtype=F32)

    def blockdiag_in(w):
        w4 = w.reshape(S5_CB, S5_CB_GROUPS, p, hh)
        return jnp.einsum('cgph,gk->cghkp', w4, eye).reshape(S5_CB, S5_CB_IN, S5_CB_STATE)

    def blockdiag_out(w):
        w4 = w.astype(F32).reshape(S5_CB, S5_CB_GROUPS, hh, p)
        return jnp.einsum('cghp,gk->cgpkh', w4, eye).reshape(S5_CB, S5_CB_STATE, S5_CB_IN)

    bbd = jnp.concatenate([blockdiag_in(bb_re), blockdiag_in(bb_im)], axis=-1).astype(BF16)
    ccd = jnp.concatenate([blockdiag_out(c_re), -blockdiag_out(c_im)], axis=1).astype(BF16)
    a = jnp.stack([a_re.reshape(S5_CB, S5_CB_STATE), a_im.reshape(S5_CB, S5_CB_STATE)], axis=1)
    return a, bbd, ccd


def _s5_pack_state(s_re, s_im):
    nb = s_re.shape[0]
    return jnp.stack([s_re.reshape(nb, S5_CB, S5_CB_STATE), s_im.reshape(nb, S5_CB, S5_CB_STATE)],
                     axis=2).reshape(nb, 2 * S5_GROUPS * S5_STATE)


def _s5_unpack_state(s):
    nb = s.shape[0]
    s4 = s.reshape(nb, S5_CB, 2, S5_CB_STATE)
    return (s4[:, :, 0].reshape(nb, S5_GROUPS, S5_STATE), s4[:, :, 1].reshape(nb, S5_GROUPS, S5_STATE))


def _s5_kernel(x_ref, s0_ref, nw_ref, a_ref, bbd_ref, ccd_ref, d_ref, wglu_ref,
               out_ref, sfin_ref, bu_ref, st_ref, hs_ref, ys_ref, *, nb, lc):
    rows = nb * lc
    nk = D_MODEL // LANES
    lane = lambda k: slice(k * LANES, (k + 1) * LANES)

    @pl.when(pl.program_id(0) == 0)
    def _():
        st_ref[...] = s0_ref[...]

    x = x_ref[...].reshape(rows, D_MODEL)
    h = _rmsnorm(x, nw_ref[...])
    if nb <= lc:
        for b in range(nb):
            for k in range(nk):
                hs_ref[k, pl.ds(b, lc, stride=nb), :] = h[b * lc:(b + 1) * lc, lane(k)]
        hb = jnp.concatenate([hs_ref[k] for k in range(nk)], axis=-1).astype(BF16)
    else:
        for k in range(nk):
            hs_ref[k] = h[:, lane(k)]
        hb = jnp.concatenate(
            [jnp.concatenate([hs_ref[k, pl.ds(t, nb, stride=lc), :] for t in range(lc)], axis=0)
             for k in range(nk)], axis=-1).astype(BF16)
    nt = S5_CB_STATE // LANES
    for cb in range(S5_CB):
        bu = _dot(hb[:, cb * S5_CB_IN:(cb + 1) * S5_CB_IN], bbd_ref[cb])
        for k in range(2 * nt):
            bu_ref[k] = bu[:, lane(k)]
        a_re = [jnp.broadcast_to(a_ref[cb, 0:1, lane(k)], (SUBLANES, LANES)) for k in range(nt)]
        a_im = [jnp.broadcast_to(a_ref[cb, 1:2, lane(k)], (SUBLANES, LANES)) for k in range(nt)]
        col = cb * 2 * S5_CB_STATE
        for j in range(nb // SUBLANES):
            brow = j * SUBLANES

            def step(t, carry, brow=brow):
                idx = pl.ds(pl.multiple_of(t * nb + brow, SUBLANES), SUBLANES)
                new_re, new_im = [], []
                for k in range(nt):
                    s_re, s_im = carry[k], carry[nt + k]
                    n_re = a_re[k] * s_re - a_im[k] * s_im + bu_ref[k, idx, :]
                    n_im = a_re[k] * s_im + a_im[k] * s_re + bu_ref[nt + k, idx, :]
                    bu_ref[k, idx, :] = n_re
                    bu_ref[nt + k, idx, :] = n_im
                    new_re.append(n_re)
                    new_im.append(n_im)
                return tuple(new_re) + tuple(new_im)

            carry = tuple(st_ref[brow:brow + SUBLANES, col + k * LANES:col + (k + 1) * LANES]
                          for k in range(2 * nt))
            if lc <= 8:
                for t in range(lc):
                    carry = step(t, carry)
            else:
                carry = lax.fori_loop(0, lc, step, carry, unroll=4)
            for k in range(2 * nt):
                st_ref[brow:brow + SUBLANES, col + k * LANES:col + (k + 1) * LANES] = carry[k]
        st_all = jnp.concatenate([bu_ref[k] for k in range(2 * nt)], axis=-1)
        y_cb = _dot(st_all.astype(BF16), ccd_ref[cb])
        for k in range(S5_CB_IN // LANES):
            ys_ref[cb * (S5_CB_IN // LANES) + k] = y_cb[:, lane(k)]
    if nb <= lc:
        y = jnp.concatenate(
            [jnp.concatenate([ys_ref[k, pl.ds(b, lc, stride=nb), :] for b in range(nb)], axis=0)
             for k in range(nk)], axis=-1)
    else:
        for t in range(lc):
            for k in range(nk):
                hs_ref[k, pl.ds(t, nb, stride=lc), :] = ys_ref[k, t * nb:(t + 1) * nb, :]
        y = jnp.concatenate([hs_ref[k] for k in range(nk)], axis=-1)
    y = _gelu(y + d_ref[...] * h)
    z = _dot(y.astype(BF16), wglu_ref[...])
    out = x + z[:, :D_MODEL] * _sigmoid(z[:, D_MODEL:])
    out_ref[...] = out.reshape(out_ref.shape)
    sfin_ref[...] = st_ref[...]


def _s5_layer(x, s0_re, s0_im, norm_w, a, bbd, ccd, d, w_glu, *, lc):
    nb, seq, _ = x.shape
    nstate = 2 * S5_GROUPS * S5_STATE
    rows = nb * lc
    if seq == lc:
        xin = x.reshape(rows, D_MODEL)
        x_spec = pl.BlockSpec((rows, D_MODEL), lambda c: (0, 0))
    else:
        xin = x
        x_spec = pl.BlockSpec((nb, lc, D_MODEL), lambda c: (0, c, 0))
    out, sfin = pl.pallas_call(
        functools.partial(_s5_kernel, nb=nb, lc=lc),
        grid=(seq // lc,),
        in_specs=[
            x_spec,
            _const_spec((nb, nstate)),
            _const_spec((1, D_MODEL)),
            _const_spec((S5_CB, 2, S5_CB_STATE)),
            _const_spec((S5_CB, S5_CB_IN, 2 * S5_CB_STATE)),
            _const_spec((S5_CB, 2 * S5_CB_STATE, S5_CB_IN)),
            _const_spec((1, D_MODEL)),
            _const_spec((D_MODEL, 2 * D_MODEL)),
        ],
        out_specs=(x_spec, pl.BlockSpec((nb, nstate), lambda c: (0, 0))),
        out_shape=(jax.ShapeDtypeStruct(xin.shape, F32), jax.ShapeDtypeStruct((nb, nstate), F32)),
        scratch_shapes=[
            pltpu.VMEM((2 * S5_CB_STATE // LANES, rows, LANES), F32),
            pltpu.VMEM((nb, nstate), F32),
            pltpu.VMEM((D_MODEL // LANES, rows, LANES), F32),
            pltpu.VMEM((D_MODEL // LANES, rows, LANES), F32),
        ],
        compiler_params=pltpu.CompilerParams(
            dimension_semantics=("arbitrary",), vmem_limit_bytes=VMEM_LIMIT_BYTES),
        name="s5_layer",
    )(xin, _s5_pack_state(s0_re, s0_im), norm_w.reshape(1, D_MODEL), a, bbd, ccd,
      d.reshape(1, D_MODEL), w_glu)
    new_re, new_im = _s5_unpack_state(sfin)
    return out.reshape(x.shape), new_re, new_im


def _hgrn_gates(proj, lb):
    q = proj[:, 0:HG_FDIM]
    q = q * _sigmoid(q)
    fz = proj[:, HG_FDIM:2 * HG_FDIM]
    v = proj[:, 2 * HG_FDIM:2 * HG_FDIM + D_MODEL]
    g = proj[:, 2 * HG_FDIM + D_MODEL:]
    log_sig = jnp.minimum(fz, 0.0) - jnp.log1p(jnp.exp(-jnp.abs(fz)))
    t1 = jnp.log(lb)
    t2 = jnp.log1p(-lb) + log_sig
    hi = jnp.maximum(t1, t2)
    lo = jnp.minimum(t1, t2)
    logf = hi + jnp.log1p(jnp.exp(lo - hi))
    k = (1.0 - lb) * _sigmoid(-fz)
    return q, k, v, g, logf


def _hgrn_tile(q, k, v, g, logf, st_refs, nw, *, chunk):
    rows = q.shape[0]
    nseg = rows // chunk
    ri = lax.broadcasted_iota(jnp.int32, (rows, rows), 0)
    ci = lax.broadcasted_iota(jnp.int32, (rows, rows), 1)
    if nseg == 1:
        same = ci <= ri
    else:
        same = (ci <= ri) & ((ri // chunk) == (ci // chunk))
    tri = jnp.where(same, 1.0, 0.0).astype(F32)
    b = jnp.dot(tri, logf, precision=lax.Precision.HIGHEST, preferred_element_type=F32)
    rowid = lax.broadcasted_iota(jnp.int32, (rows, 1), 0)
    b_mid = b[chunk // 2:chunk // 2 + 1, :]
    b_last = b[chunk - 1:chunk, :]
    for s in range(1, nseg):
        in_s = rowid >= s * chunk
        b_mid = jnp.where(in_s, b[s * chunk + chunk // 2:s * chunk + chunk // 2 + 1, :], b_mid)
        b_last = jnp.where(in_s, b[s * chunk + chunk - 1:s * chunk + chunk, :], b_last)
    qs = (q * jnp.exp(b - b_mid)).astype(BF16)
    ks = (k * jnp.exp(b_mid - b)).astype(BF16)
    qi = (q * jnp.exp(b)).astype(BF16)
    kd = k * jnp.exp(b_last - b)
    vb = v.astype(BF16)
    outs = []
    for hd in range(HG_HEADS):
        sl = slice(hd * HG_DK, (hd + 1) * HG_DK)
        scores = jnp.where(same, _dot_nt(qs[:, sl], ks[:, sl]), 0.0)
        o = _dot(scores.astype(BF16), vb[:, sl])
        v_t = v[:, sl].T
        for s in range(nseg):
            st = st_refs[s][hd]
            o_s = _dot_nt(qi[:, sl], st.astype(BF16))
            kd_s = kd[:, sl]
            if nseg > 1:
                in_s = (rowid >= s * chunk) & (rowid < (s + 1) * chunk)
                o_s = jnp.where(in_s, o_s, 0.0)
                kd_s = jnp.where(in_s, kd_s, 0.0)
            o = o + o_s
            dec = jnp.exp(b[s * chunk + chunk - 1:s * chunk + chunk, sl])
            st_refs[s][hd] = dec * st + _dot(v_t.astype(BF16), kd_s.astype(BF16))
        o = o * lax.rsqrt(jnp.mean(o * o, axis=-1, keepdims=True) + RMS_EPS)
        gh = g[:, sl]
        outs.append(o * nw * (gh * _sigmoid(gh)))
    return jnp.concatenate(outs, axis=-1)


def _hgrn_prompt_kernel(x_ref, s0_ref, nw_ref, win_ref, lb_ref, hnw_ref, wout_ref,
                        out_ref, sout_ref, st_ref, y_ref, *, lc, chunk):
    c = pl.program_id(1)

    @pl.when(c == 0)
    def _():
        for hd in range(HG_HEADS):
            st_ref[hd] = s0_ref[0, hd].T

    x = x_ref[0]
    hb = _rmsnorm(x, nw_ref[...]).astype(BF16)
    proj = _dot(hb, win_ref[...])
    q, k, v, g, logf = _hgrn_gates(proj, lb_ref[...])
    for sub in range(lc // chunk):
        r = slice(sub * chunk, (sub + 1) * chunk)
        y_ref[r, :] = _hgrn_tile(q[r], k[r], v[r], g[r], logf[r], [st_ref], hnw_ref[...], chunk=chunk)
    out_ref[0] = x + _dot(y_ref[...].astype(BF16), wout_ref[...])

    @pl.when(c == pl.num_programs(1) - 1)
    def _():
        for hd in range(HG_HEADS):
            sout_ref[0, hd] = st_ref[hd].T


def _hgrn_sample_kernel(x_ref, s0_ref, nw_ref, win_ref, lb_ref, hnw_ref, wout_ref,
                        out_ref, sout_ref, st_ref, proj_ref, y_ref, *, seq):
    p = pl.program_id(0)
    nseg = SUBLANES // seq

    @pl.when(p == 0)
    def _():
        hb = _rmsnorm(x_ref[...], nw_ref[...]).astype(BF16)
        proj_ref[...] = _dot(hb, win_ref[...])

    for s in range(nseg):
        for hd in range(HG_HEADS):
            st_ref[s, hd] = s0_ref[s, hd].T
    r = pl.ds(pl.multiple_of(p * SUBLANES, SUBLANES), SUBLANES)
    q, k, v, g, logf = _hgrn_gates(proj_ref[r, :], lb_ref[...])
    y_ref[r, :] = _hgrn_tile(q, k, v, g, logf, [st_ref.at[s] for s in range(nseg)], hnw_ref[...],
                             chunk=seq)
    for s in range(nseg):
        for hd in range(HG_HEADS):
            sout_ref[s, hd] = st_ref[s, hd].T

    @pl.when(p == pl.num_programs(0) - 1)
    def _():
        out_ref[...] = x_ref[...] + _dot(y_ref[...].astype(BF16), wout_ref[...])


def _hgrn_layer(x, s0, norm_w, w_in, lb, hg_norm_w, w_out):
    nb, seq, _ = x.shape
    weights = (norm_w.reshape(1, D_MODEL), w_in.astype(BF16), lb.reshape(1, HG_FDIM),
               hg_norm_w.reshape(1, HG_DV), w_out.astype(BF16))
    w_specs = [
        _const_spec((1, D_MODEL)),
        _const_spec((D_MODEL, 2 * HG_FDIM + 2 * D_MODEL)),
        _const_spec((1, HG_FDIM)),
        _const_spec((1, HG_DV)),
        _const_spec((D_MODEL, D_MODEL)),
    ]
    s_shape = jax.ShapeDtypeStruct(s0.shape, F32)
    if seq >= HG_CHUNK:
        chunk = HG_CHUNK
        lc = HG_CHUNKS_PER_STEP * chunk
        x_spec = pl.BlockSpec((1, lc, D_MODEL), lambda b, c: (b, c, 0))
        s_spec = pl.BlockSpec((1, HG_HEADS, HG_DK, HG_DV), lambda b, c: (b, 0, 0, 0))
        out, s_new = pl.pallas_call(
            functools.partial(_hgrn_prompt_kernel, lc=lc, chunk=chunk),
            grid=(nb, seq // lc),
            in_specs=[x_spec, s_spec] + w_specs,
            out_specs=(x_spec, s_spec),
            out_shape=(jax.ShapeDtypeStruct(x.shape, F32), s_shape),
            scratch_shapes=[
                pltpu.VMEM((HG_HEADS, HG_DV, HG_DK), F32),
                pltpu.VMEM((lc, D_MODEL), F32),
            ],
            compiler_params=pltpu.CompilerParams(
                dimension_semantics=("arbitrary", "arbitrary"), vmem_limit_bytes=VMEM_LIMIT_BYTES),
            name="hgrn_prompt",
        )(x, s0, *weights)
        return out, s_new
    assert SUBLANES % seq == 0 and nb % (SUBLANES // seq) == 0
    nseg = SUBLANES // seq
    rows = nb * seq
    x_spec = _const_spec((rows, D_MODEL))
    s_spec = pl.BlockSpec((nseg, HG_HEADS, HG_DK, HG_DV), lambda p: (p, 0, 0, 0))
    out, s_new = pl.pallas_call(
        functools.partial(_hgrn_sample_kernel, seq=seq),
        grid=(nb // nseg,),
        in_specs=[x_spec, s_spec] + w_specs,
        out_specs=(pl.BlockSpec((rows, D_MODEL), lambda p: (0, 0)), s_spec),
        out_shape=(jax.ShapeDtypeStruct((rows, D_MODEL), F32), s_shape),
        scratch_shapes=[
            pltpu.VMEM((nseg, HG_HEADS, HG_DV, HG_DK), F32),
            pltpu.VMEM((rows, 2 * HG_FDIM + 2 * D_MODEL), F32),
            pltpu.VMEM((rows, D_MODEL), F32),
        ],
        compiler_params=pltpu.CompilerParams(
            dimension_semantics=("arbitrary",), vmem_limit_bytes=VMEM_LIMIT_BYTES),
        name="hgrn_sample",
    )(x.reshape(rows, D_MODEL), s0, *weights)
    return out.reshape(x.shape), s_new


def _compare_exchange(v, i, j):
    hi = jnp.maximum(v[i], v[j])
    lo = jnp.minimum(v[i], v[j])
    v[i] = hi
    v[j] = lo


def _bitonic_merge_desc(v):
    n = len(v)
    j = n // 2
    while j >= 1:
        for i in range(n):
            if i & j == 0:
                _compare_exchange(v, i, i + j)
        j //= 2


_SORT16 = (
    (0, 13), (1, 12), (2, 15), (3, 14), (4, 8), (5, 6), (7, 11), (9, 10),
    (0, 5), (1, 7), (2, 9), (3, 4), (6, 13), (8, 14), (10, 15), (11, 12),
    (0, 1), (2, 3), (4, 5), (6, 8), (7, 9), (10, 11), (12, 13), (14, 15),
    (0, 2), (1, 3), (4, 10), (5, 11), (6, 7), (8, 9), (12, 14), (13, 15),
    (1, 2), (3, 12), (4, 6), (5, 7), (8, 10), (9, 11), (13, 14),
    (1, 4), (2, 6), (5, 8), (7, 10), (9, 13), (11, 14),
    (2, 4), (3, 6), (9, 12), (11, 13),
    (3, 5), (6, 8), (7, 9), (10, 12),
    (3, 4), (5, 6), (7, 8), (9, 10), (11, 12),
    (6, 7), (8, 9),
)


def _sort_desc(v):
    assert len(v) == PEER_TOPK
    for i, j in _SORT16:
        _compare_exchange(v, i, j)
    return v


def _merge_top(a, b):
    n = len(a)
    v = [jnp.maximum(a[r], b[n - 1 - r]) for r in range(n)]
    _bitonic_merge_desc(v)
    return v


def _top16_rows(s):
    v = [s[SUBLANES * j:SUBLANES * (j + 1), :] for j in range(PEER_NKEYS // SUBLANES)]
    v = _sort_desc(v)
    for shift in (1, 2, 4):
        v = _merge_top(v, [pltpu.roll(x, shift, axis=0) for x in v])
    return v


def _candidate_top(a, b):
    n = PEER_TOPK
    low = jnp.full(a[0].shape, -jnp.inf, F32)
    runs = [
        [a[i] + b[0] for i in range(1, n)],
        [a[1] + b[j] for j in range(1, 8)],
        [a[2] + b[j] for j in range(1, 5)],
        [a[3] + b[j] for j in range(1, 4)],
        [a[4] + b[j] for j in range(1, 3)],
        [a[i] + b[1] for i in range(5, 8)],
    ]
    assert all((i + 1) * (j + 1) > n for i in range(1, n) for j in range(1, n)
               if not ((i == 1 and j < 8) or (i == 2 and j < 5) or (i == 3 and j < 4)
                       or (i == 4 and j < 3) or (5 <= i < 8 and j == 1)))
    top = [a[0] + b[j] for j in range(n)]
    for run in runs:
        top = _merge_top(top, run + [low] * (n - len(run)))
    return top


def _staircase_cells():
    return [(i, j) for i in range(PEER_TOPK) for j in range(PEER_TOPK) if (i + 1) * (j + 1) <= PEER_TOPK]


def _staircase_count(a, b, thr):
    n = None
    for i, j in _staircase_cells():
        hit = jnp.where(a[i] + b[j] >= thr, 1.0, 0.0)
        n = hit if n is None else n + hit
    return n


def _peer_select_kernel(x_ref, nw_ref, wq_ref, k1_ref, k2_ref,
                        hb_ref, s1_ref, s2_ref, thr_ref, tie_ref, t1_ref, t2_ref):
    hb = _rmsnorm(x_ref[...], nw_ref[...]).astype(BF16)
    hb_ref[...] = hb
    qb = _dot(hb, wq_ref[...]).astype(BF16)
    k1 = k1_ref[...]
    k2 = k2_ref[...]
    log2e = 1.0 / math.log(2.0)
    tied = None
    for hd in range(PEER_HEADS):
        base = hd * PEER_DKEY
        s1 = _dot_nt(k1, qb[:, base:base + PEER_HALF]) * log2e
        s2 = _dot_nt(k2, qb[:, base + PEER_HALF:base + PEER_DKEY]) * log2e
        s1_ref[hd] = s1
        s2_ref[hd] = s2
        for s, t_ref in ((s1, t1_ref), (s2, t2_ref)):
            top = _top16_rows(s)
            for r in range(PEER_TOPK):
                t_ref[r, hd:hd + 1, :] = top[r][0:1, :]
            n_top = jnp.sum(jnp.where(s >= top[PEER_TOPK - 1][0:1, :], 1.0, 0.0), axis=0, keepdims=True)
            over = jnp.where(n_top > PEER_TOPK, 1.0, 0.0)
            tied = over if tied is None else jnp.maximum(tied, over)
    a = [t1_ref[r] for r in range(PEER_TOPK)]
    b = [t2_ref[r] for r in range(PEER_TOPK)]
    top = _candidate_top(a, b)
    z = jnp.exp2(top[0] - top[0])
    for r in range(1, PEER_TOPK):
        z = z + jnp.exp2(top[r] - top[0])
    shift = top[0] + jnp.log2(z) + 1.0
    for hd in range(PEER_HEADS):
        s1_ref[hd] = s1_ref[hd] - shift[hd:hd + 1, :]
    a_shifted = [v - shift for v in a]
    thr = _candidate_top(a_shifted, b)[PEER_TOPK - 1]
    thr_ref[...] = thr
    n_sel = _staircase_count(a, b, top[PEER_TOPK - 1])
    over = jnp.max(jnp.where(n_sel > PEER_TOPK, 1.0, 0.0), axis=0, keepdims=True)
    flag = jnp.max(jnp.maximum(tied, over), axis=1, keepdims=True)
    tie_ref[0] = jnp.broadcast_to(flag, tie_ref.shape[1:])


def _ordered_rank(s_ref):
    s = s_ref[...]
    rowid = lax.broadcasted_iota(jnp.int32, (s.shape[0], 1), 0)

    def body(j, rank):
        sj = s_ref[pl.ds(j, 1), :]
        before = (sj > s) | ((sj == s) & (rowid > j))
        return rank + jnp.where(before, 1.0, 0.0)

    return lax.fori_loop(0, s.shape[0], body, jnp.zeros(s.shape, F32))


def _peer_select_ordered_kernel(tied_ref, x_ref, nw_ref, wq_ref, k1_ref, k2_ref,
                                c1_ref, r2_ref, t1_ref, t2_ref, r1_ref, s_ref):
    tied = tied_ref[pl.program_id(0)] != 0

    @pl.when(jnp.logical_not(tied))
    def _():
        c1_ref[...] = jnp.zeros_like(c1_ref)
        r2_ref[...] = jnp.zeros_like(r2_ref)

    @pl.when(tied)
    def _():
        _ordered_selection(x_ref, nw_ref, wq_ref, k1_ref, k2_ref, c1_ref, r2_ref, t1_ref, t2_ref, r1_ref, s_ref)


def _ordered_selection(x_ref, nw_ref, wq_ref, k1_ref, k2_ref, c1_ref, r2_ref, t1_ref, t2_ref, r1_ref, s_ref):
    hb = _rmsnorm(x_ref[...], nw_ref[...]).astype(BF16)
    qb = _dot(hb, wq_ref[...]).astype(BF16)
    k1 = k1_ref[...]
    k2 = k2_ref[...]
    log2e = 1.0 / math.log(2.0)
    for hd in range(PEER_HEADS):
        base = hd * PEER_DKEY
        s1 = _dot_nt(k1, qb[:, base:base + PEER_HALF]) * log2e
        s2 = _dot_nt(k2, qb[:, base + PEER_HALF:base + PEER_DKEY]) * log2e
        for s, t_ref, rank_ref in ((s1, t1_ref, r1_ref), (s2, t2_ref, r2_ref)):
            s_ref[...] = s
            rank = _ordered_rank(s_ref)
            rank_ref[hd] = jnp.minimum(rank, float(PEER_TOPK))
            for r in range(PEER_TOPK):
                t_ref[r, hd:hd + 1, :] = jnp.sum(jnp.where(rank == r, s, 0.0), axis=0, keepdims=True)
    a = [t1_ref[r] for r in range(PEER_TOPK)]
    b = [t2_ref[r] for r in range(PEER_TOPK)]
    cells = _staircase_cells()
    sums = [a[i] + b[j] for i, j in cells]
    ahead = [None] * len(cells)

    def bump(k, hit):
        ahead[k] = hit if ahead[k] is None else ahead[k] + hit

    for p in range(len(cells)):
        for q in range(p + 1, len(cells)):
            bump(q, jnp.where(sums[p] >= sums[q], 1.0, 0.0))
            bump(p, jnp.where(sums[q] > sums[p], 1.0, 0.0))
    count = [None] * PEER_TOPK
    for (i, j), n in zip(cells, ahead):
        ch = jnp.where(n < PEER_TOPK, 1.0, 0.0)
        count[i] = ch if count[i] is None else count[i] + ch
    for hd in range(PEER_HEADS):
        row = slice(hd, hd + 1)
        rank1 = r1_ref[hd]
        c1 = jnp.zeros(rank1.shape, F32)
        for r in range(PEER_TOPK):
            c1 = jnp.where(rank1 == r, count[r][row, :], c1)
        c1_ref[hd] = c1


def _peer_expert_kernel(tied_ref, hb_ref, x_ref, s1_ref, s2_ref, thr_ref, c1_ref, r2_ref,
                        u_ref, vp_ref, v_ref, fw_ref, out_ref, acc_ref, *tile_refs, tb, eb, tw, final_norm):
    e = pl.program_id(1)
    tied = tied_ref[pl.program_id(0)] != 0
    ntile = tb // tw
    ht_refs, at_refs = tile_refs[:ntile], tile_refs[ntile:]

    @pl.when(e == 0)
    def _():
        acc_ref[...] = jnp.zeros_like(acc_ref)
        for at_ref in at_refs:
            at_ref[...] = jnp.zeros_like(at_ref)

    def step(ordered):
        u = u_ref[...]
        vp = vp_ref[...]
        for i in range(ntile):
            ht_refs[i][...] = _dot_nt(u, hb_ref[i * tw:(i + 1) * tw, :])
            acc_ref[:, i * tw:(i + 1) * tw] += _dot_tn(vp, at_refs[i][...])
        for i in range(ntile):
            for lt in range(tw // LANES):
                lcol = slice(lt * LANES, (lt + 1) * LANES)
                col = slice(i * tw + lt * LANES, i * tw + (lt + 1) * LANES)
                for r in range(eb // PEER_NKEYS):
                    gate = None
                    for hd in range(PEER_HEADS):
                        t = s2_ref[hd, :, col] + s1_ref[hd, r:r + 1, col]
                        if ordered:
                            picked = r2_ref[hd, :, col] < c1_ref[hd, r:r + 1, col]
                        else:
                            picked = t >= thr_ref[hd:hd + 1, col]
                        g = jnp.where(picked, jnp.exp2(t), 0.0)
                        gate = g if gate is None else gate + g
                    rr = slice(r * PEER_NKEYS, (r + 1) * PEER_NKEYS)
                    h = ht_refs[i][rr, lcol]
                    at_refs[i][rr, lcol] = (
                        h * (1.0 + lax.erf(h * (1.0 / math.sqrt(2.0)))) * gate).astype(BF16)

    pl.when(jnp.logical_not(tied))(functools.partial(step, False))
    pl.when(tied)(functools.partial(step, True))

    @pl.when(e == pl.num_programs(1) - 1)
    def _():
        v = v_ref[...]
        for i in range(ntile):
            acc_ref[:, i * tw:(i + 1) * tw] += _dot_tn(v, at_refs[i][...])
        y = x_ref[...] + acc_ref[...].T
        if final_norm:
            y = _rmsnorm(y, fw_ref[...])
        out_ref[...] = y


def _peer_layer(x, norm_w, w_q, keys1, keys2, u_tab, v_tab, final_w, *, layer, final_norm):
    t_tot = x.shape[0]
    tbs = min(PEER_SELECT_TOKENS, t_tot)
    tb = min(PEER_EXPERT_TOKENS, t_tot)
    eb = PEER_EXPERT_BLOCK
    tw = PEER_TOKEN_TILE
    nsel = t_tot // tbs
    select_in = (x, norm_w.reshape(1, D_MODEL), w_q.astype(BF16), keys1.astype(BF16), keys2.astype(BF16))
    select_specs = [
        pl.BlockSpec((tbs, D_MODEL), lambda i: (i, 0)),
        _const_spec((1, D_MODEL)),
        _const_spec((D_MODEL, PEER_HEADS * PEER_DKEY)),
        _const_spec((PEER_NKEYS, PEER_HALF)),
        _const_spec((PEER_NKEYS, PEER_HALF)),
    ]
    select_params = pltpu.CompilerParams(dimension_semantics=("arbitrary",), vmem_limit_bytes=VMEM_LIMIT_BYTES)
    key_spec = pl.BlockSpec((PEER_HEADS, PEER_NKEYS, tbs), lambda i: (0, 0, i))
    key_f32 = jax.ShapeDtypeStruct((PEER_HEADS, PEER_NKEYS, t_tot), F32)
    top_scratch = pltpu.VMEM((PEER_TOPK, PEER_HEADS, tbs), F32)
    hb, s1, s2, thr, tie = pl.pallas_call(
        _peer_select_kernel,
        grid=(nsel,),
        in_specs=select_specs,
        out_specs=(pl.BlockSpec((tbs, D_MODEL), lambda i: (i, 0)), key_spec, key_spec,
                   pl.BlockSpec((PEER_HEADS, tbs), lambda i: (0, i)),
                   pl.BlockSpec((1, SUBLANES, LANES), lambda i: (i, 0, 0))),
        out_shape=(jax.ShapeDtypeStruct((t_tot, D_MODEL), BF16), key_f32, key_f32,
                   jax.ShapeDtypeStruct((PEER_HEADS, t_tot), F32),
                   jax.ShapeDtypeStruct((nsel, SUBLANES, LANES), F32)),
        scratch_shapes=[top_scratch, top_scratch],
        compiler_params=select_params,
        name="peer_select",
    )(*select_in)

    tied_exp = jnp.max((tie[:, 0, 0] > 0.0).astype(jnp.int32).reshape(t_tot // tb, tb // tbs), axis=1)
    tied_sel = jnp.repeat(tied_exp, tb // tbs)
    c1, r2 = pl.pallas_call(
        _peer_select_ordered_kernel,
        grid_spec=pltpu.PrefetchScalarGridSpec(
            num_scalar_prefetch=1,
            grid=(nsel,),
            in_specs=[
                pl.BlockSpec((tbs, D_MODEL), lambda i, tied: (i, 0)),
                pl.BlockSpec((1, D_MODEL), lambda i, tied: (0, 0)),
                pl.BlockSpec((D_MODEL, PEER_HEADS * PEER_DKEY), lambda i, tied: (0, 0)),
                pl.BlockSpec((PEER_NKEYS, PEER_HALF), lambda i, tied: (0, 0)),
                pl.BlockSpec((PEER_NKEYS, PEER_HALF), lambda i, tied: (0, 0)),
            ],
            out_specs=(pl.BlockSpec((PEER_HEADS, PEER_NKEYS, tbs), lambda i, tied: (0, 0, i)),
                       pl.BlockSpec((PEER_HEADS, PEER_NKEYS, tbs), lambda i, tied: (0, 0, i))),
            scratch_shapes=[top_scratch, top_scratch,
                            pltpu.VMEM((PEER_HEADS, PEER_NKEYS, tbs), F32),
                            pltpu.VMEM((PEER_NKEYS, tbs), F32)],
        ),
        out_shape=(key_f32, key_f32),
        compiler_params=select_params,
        name="peer_select_ordered",
    )(tied_sel, *select_in)

    once = pl.Buffered(1)
    tok = lambda i, e, tied: (i, 0)
    row_spec = pl.BlockSpec((PEER_HEADS, eb // PEER_NKEYS, tb), lambda i, e, tied: (0, e, i))
    all_spec = pl.BlockSpec((PEER_HEADS, PEER_NKEYS, tb), lambda i, e, tied: (0, 0, i), pipeline_mode=once)
    return pl.pallas_call(
        functools.partial(_peer_expert_kernel, tb=tb, eb=eb, tw=tw, final_norm=final_norm),
        grid_spec=pltpu.PrefetchScalarGridSpec(
            num_scalar_prefetch=1,
            grid=(t_tot // tb, PEER_EXPERTS // eb),
            in_specs=[
                pl.BlockSpec((tb, D_MODEL), tok, pipeline_mode=once),
                pl.BlockSpec((tb, D_MODEL), tok, pipeline_mode=once),
                row_spec,
                all_spec,
                pl.BlockSpec((PEER_HEADS, tb), lambda i, e, tied: (0, i), pipeline_mode=once),
                row_spec,
                all_spec,
                pl.BlockSpec((None, eb, D_MODEL), lambda i, e, tied: (layer, e, 0)),
                pl.BlockSpec((None, eb, D_MODEL), lambda i, e, tied: (layer, jnp.maximum(e - 1, 0), 0)),
                pl.BlockSpec((None, eb, D_MODEL), lambda i, e, tied: (layer, PEER_EXPERTS // eb - 1, 0),
                             pipeline_mode=once),
                pl.BlockSpec((1, D_MODEL), lambda i, e, tied: (0, 0), pipeline_mode=once),
            ],
            out_specs=pl.BlockSpec((tb, D_MODEL), tok),
            scratch_shapes=(
                [pltpu.VMEM((D_MODEL, tb), F32)]
                + [pltpu.VMEM((eb, tw), F32) for _ in range(tb // tw)]
                + [pltpu.VMEM((eb, tw), BF16) for _ in range(tb // tw)]
            ),
        ),
        out_shape=jax.ShapeDtypeStruct((t_tot, D_MODEL), F32),
        compiler_params=pltpu.CompilerParams(
            dimension_semantics=("arbitrary", "arbitrary"), vmem_limit_bytes=VMEM_LIMIT_BYTES),
        name="peer_experts",
    )(tied_exp, hb, x, s1, s2, thr, c1, r2, u_tab, v_tab, v_tab, final_w.reshape(1, D_MODEL))


def _trunk(x, s5_re, s5_im, s_hg, p, s5_prm, lb_all):
    nb, seq, _ = x.shape
    new_re, new_im, new_hg = [], [], []
    for i in range(DEPTH):
        j = i // N_MIXERS
        if i % N_MIXERS == 0:
            a, bbd, ccd = s5_prm[j]
            x, sr, si = _s5_layer(x, s5_re[j], s5_im[j], p["norm_mix"][i], a, bbd, ccd,
                                  p["s5_d"][j], p["s5_w_glu"][j].astype(BF16), lc=min(seq, S5_TIME_CHUNK))
            new_re.append(sr)
            new_im.append(si)
        else:
            x, s_new = _hgrn_layer(x, s_hg[j], p["norm_mix"][i], p["hg_w_in"][j], lb_all[i],
                                   p["hg_norm_w"][j], p["hg_w_out"][j])
            new_hg.append(s_new)
        x = _peer_layer(x.reshape(nb * seq, D_MODEL), p["norm_ffn"][i], p["peer_w_q"][i],
                        p["peer_keys1"][i], p["peer_keys2"][i], p["peer_u"], p["peer_v"],
                        p["norm_final"], layer=i, final_norm=(i == DEPTH - 1)).reshape(nb, seq, D_MODEL)
    return x, jnp.stack(new_re), jnp.stack(new_im), jnp.stack(new_hg)


def kernel(x_prompt, x_sample, state_s5_re, state_s5_im, state_hgrn, norm_mix, norm_ffn, norm_final,
           s5_lambda_re, s5_lambda_im, s5_log_dt, s5_b_re, s5_b_im, s5_c_re, s5_c_im, s5_d, s5_w_glu,
           hg_w_in, hg_lower_bounds, hg_norm_w, hg_w_out, peer_w_q, peer_keys1, peer_keys2, peer_u, peer_v):
    p = dict(norm_mix=norm_mix, norm_ffn=norm_ffn, norm_final=norm_final, s5_d=s5_d, s5_w_glu=s5_w_glu,
             hg_w_in=hg_w_in, hg_norm_w=hg_norm_w, hg_w_out=hg_w_out, peer_w_q=peer_w_q,
             peer_keys1=peer_keys1, peer_keys2=peer_keys2,
             peer_u=peer_u.astype(BF16), peer_v=peer_v.astype(BF16))
    n_a = s5_lambda_re.shape[0]
    s5_prm = [_s5_params(s5_lambda_re[j], s5_lambda_im[j], s5_log_dt[j], s5_b_re[j], s5_b_im[j],
                         s5_c_re[j], s5_c_im[j]) for j in range(n_a)]
    lb_all = jnp.cumsum(jax.nn.softmax(hg_lower_bounds.astype(F32), axis=0), axis=0)
    lb_all = lb_all - lb_all[0:1]
    nbp = x_prompt.shape[0]
    z_s5 = jnp.zeros((n_a, nbp, S5_GROUPS, S5_STATE), state_s5_re.dtype)
    z_hg = jnp.zeros((state_hgrn.shape[0], nbp, HG_HEADS, HG_DK, HG_DV), state_hgrn.dtype)
    y_p, re_p, im_p, hg_p = _trunk(x_prompt, z_s5, z_s5, z_hg, p, s5_prm, lb_all)
    y_s, re_s, im_s, hg_s = _trunk(x_sample, state_s5_re, state_s5_im, state_hgrn, p, s5_prm, lb_all)
    return (y_p, y_s, re_p, im_p, hg_p, re_s, im_s, hg_s)
```

```python
import functools
import math

import jax
import jax.numpy as jnp
from jax import lax
from jax.experimental import pallas as pl
from jax.experimental.pallas import tpu as pltpu

F32 = jnp.float32
BF16 = jnp.bfloat16

D_MODEL = 1024
DEPTH = 2
N_MIXERS = 2
S5_GROUP = 16
S5_GROUPS = D_MODEL // S5_GROUP
S5_STATE = 64
HG_HEADS = 8
HG_DK = 128
HG_DV = 128
HG_FDIM = HG_HEADS * HG_DK
HG_CHUNK = 64
PEER_HEADS = 8
PEER_NKEYS = 128
PEER_EXPERTS = PEER_NKEYS * PEER_NKEYS
PEER_TOPK = 16
PEER_DKEY = 256
PEER_HALF = PEER_DKEY // 2
RMS_EPS = 1e-6

LANES = 128
SUBLANES = 8
MXU_WIDTH = 256
VMEM_LIMIT_BYTES = 56 * 2**20

S5_TIME_CHUNK = 64
HG_CHUNKS_PER_STEP = 8
PEER_SELECT_TOKENS = 512
PEER_EXPERT_TOKENS = 1024
PEER_EXPERT_BLOCK = 8 * PEER_NKEYS
PEER_TOKEN_TILE = MXU_WIDTH

S5_CB = 4
S5_CB_GROUPS = S5_GROUPS // S5_CB
S5_CB_IN = S5_CB_GROUPS * S5_GROUP
S5_CB_STATE = S5_CB_GROUPS * S5_STATE


def _const_spec(shape):
    nd = len(shape)
    return pl.BlockSpec(shape, lambda *_: (0,) * nd, pipeline_mode=pl.Buffered(1))


def _rmsnorm(x, w):
    ms = jnp.mean(x * x, axis=-1, keepdims=True)
    return x * lax.rsqrt(ms + RMS_EPS) * w


def _gelu(x):
    return 0.5 * x * (1.0 + lax.erf(x * (1.0 / math.sqrt(2.0))))


def _sigmoid(x):
    return 1.0 / (1.0 + jnp.exp(-x))


def _dot(a, b):
    return jnp.dot(a, b, preferred_element_type=F32)


def _dot_nt(a, b):
    return lax.dot_general(a, b, (((1,), (1,)), ((), ())), preferred_element_type=F32)


def _dot_tn(a, b):
    return lax.dot_general(a, b, (((0,), (0,)), ((), ())), preferred_element_type=F32)


def _s5_discretize_kernel(lre_ref, lim_ref, ldt_ref, are_ref, aim_ref, zre_ref, zim_ref):
    lr = jnp.minimum(lre_ref[...], -1e-4)
    li = lim_ref[...]
    dt = jnp.exp(ldt_ref[...])
    mag = jnp.exp(lr * dt)
    ang = li * dt
    ab_re = mag * jnp.cos(ang)
    ab_im = mag * jnp.sin(ang)
    den = lr * lr + li * li
    nr = ab_re - 1.0
    are_ref[...] = ab_re
    aim_ref[...] = ab_im
    zre_ref[...] = (nr * lr + ab_im * li) / den
    zim_ref[...] = (ab_im * lr - nr * li) / den


def _s5_scale_b_kernel(zre_ref, zim_ref, bre_ref, bim_ref, ore_ref, oim_ref):
    zr = zre_ref[...]
    zi = zim_ref[...]
    br = bre_ref[...]
    bi = bim_ref[...]
    ore_ref[...] = zr * br - zi * bi
    oim_ref[...] = zr * bi + zi * br


def _s5_params(lam_re, lam_im, log_dt, b_re, b_im, c_re, c_im):
    g, p, hh = S5_GROUPS, S5_STATE, S5_GROUP
    gp = jax.ShapeDtypeStruct((g, p), F32)
    a_re, a_im, z_re, z_im = pl.pallas_call(
        _s5_discretize_kernel, out_shape=(gp, gp, gp, gp), name="s5_discretize",
    )(lam_re.astype(F32), lam_im.astype(F32), log_dt.astype(F32).reshape(g, 1))
    gph = jax.ShapeDtypeStruct((g * p, hh), F32)
    bb_re, bb_im = pl.pallas_call(
        _s5_scale_b_kernel, out_shape=(gph, gph), name="s5_scale_b",
    )(z_re.reshape(g * p, 1), z_im.reshape(g * p, 1),
      b_re.astype(F32).reshape(g * p, hh), b_im.astype(F32).reshape(g * p, hh))
    eye = jnp.eye(S5_CB_GROUPS, dtype=F32)

    def blockdiag_in(w):
        w4 = w.reshape(S5_CB, S5_CB_GROUPS, p, hh)
        return jnp.einsum('cgph,gk->cghkp', w4, eye).reshape(S5_CB, S5_CB_IN, S5_CB_STATE)

    def blockdiag_out(w):
        w4 = w.astype(F32).reshape(S5_CB, S5_CB_GROUPS, hh, p)
        return jnp.einsum('cghp,gk->cgpkh', w4, eye).reshape(S5_CB, S5_CB_STATE, S5_CB_IN)

    bbd = jnp.concatenate([blockdiag_in(bb_re), blockdiag_in(bb_im)], axis=-1).astype(BF16)
    ccd = jnp.concatenate([blockdiag_out(c_re), -blockdiag_out(c_im)], axis=1).astype(BF16)
    a = jnp.stack([a_re.reshape(S5_CB, S5_CB_STATE), a_im.reshape(S5_CB, S5_CB_STATE)], axis=1)
    return a, bbd, ccd


def _s5_pack_state(s_re, s_im):
    nb = s_re.shape[0]
    return jnp.stack([s_re.reshape(nb, S5_CB, S5_CB_STATE), s_im.reshape(nb, S5_CB, S5_CB_STATE)],
                     axis=2).reshape(nb, 2 * S5_GROUPS * S5_STATE)


def _s5_unpack_state(s):
    nb = s.shape[0]
    s4 = s.reshape(nb, S5_CB, 2, S5_CB_STATE)
    return (s4[:, :, 0].reshape(nb, S5_GROUPS, S5_STATE), s4[:, :, 1].reshape(nb, S5_GROUPS, S5_STATE))


def _s5_kernel(x_ref, s0_ref, nw_ref, a_ref, bbd_ref, ccd_ref, d_ref, wglu_ref,
               out_ref, sfin_ref, bu_ref, st_ref, hs_ref, ys_ref, *, nb, lc):
    rows = nb * lc
    nk = D_MODEL // LANES
    lane = lambda k: slice(k * LANES, (k + 1) * LANES)

    @pl.when(pl.program_id(0) == 0)
    def _():
        st_ref[...] = s0_ref[...]

    x = x_ref[...].reshape(rows, D_MODEL)
    h = _rmsnorm(x, nw_ref[...])
    if nb <= lc:
        for b in range(nb):
            for k in range(nk):
                hs_ref[k, pl.ds(b, lc, stride=nb), :] = h[b * lc:(b + 1) * lc, lane(k)]
        hb = jnp.concatenate([hs_ref[k] for k in range(nk)], axis=-1).astype(BF16)
    else:
        for k in range(nk):
            hs_ref[k] = h[:, lane(k)]
        hb = jnp.concatenate(
            [jnp.concatenate([hs_ref[k, pl.ds(t, nb, stride=lc), :] for t in range(lc)], axis=0)
             for k in range(nk)], axis=-1).astype(BF16)
    nt = S5_CB_STATE // LANES
    for cb in range(S5_CB):
        bu = _dot(hb[:, cb * S5_CB_IN:(cb + 1) * S5_CB_IN], bbd_ref[cb])
        for k in range(2 * nt):
            bu_ref[k] = bu[:, lane(k)]
        a_re = [jnp.broadcast_to(a_ref[cb, 0:1, lane(k)], (SUBLANES, LANES)) for k in range(nt)]
        a_im = [jnp.broadcast_to(a_ref[cb, 1:2, lane(k)], (SUBLANES, LANES)) for k in range(nt)]
        col = cb * 2 * S5_CB_STATE
        for j in range(nb // SUBLANES):
            brow = j * SUBLANES

            def step(t, carry, brow=brow):
                idx = pl.ds(pl.multiple_of(t * nb + brow, SUBLANES), SUBLANES)
                new_re, new_im = [], []
                for k in range(nt):
                    s_re, s_im = carry[k], carry[nt + k]
                    n_re = a_re[k] * s_re - a_im[k] * s_im + bu_ref[k, idx, :]
                    n_im = a_re[k] * s_im + a_im[k] * s_re + bu_ref[nt + k, idx, :]
                    bu_ref[k, idx, :] = n_re
                    bu_ref[nt + k, idx, :] = n_im
                    new_re.append(n_re)
                    new_im.append(n_im)
                return tuple(new_re) + tuple(new_im)

            carry = tuple(st_ref[brow:brow + SUBLANES, col + k * LANES:col + (k + 1) * LANES]
                          for k in range(2 * nt))
            if lc <= 8:
                for t in range(lc):
                    carry = step(t, carry)
            else:
                carry = lax.fori_loop(0, lc, step, carry, unroll=4)
            for k in range(2 * nt):
                st_ref[brow:brow + SUBLANES, col + k * LANES:col + (k + 1) * LANES] = carry[k]
        st_all = jnp.concatenate([bu_ref[k] for k in range(2 * nt)], axis=-1)
        y_cb = _dot(st_all.astype(BF16), ccd_ref[cb])
        for k in range(S5_CB_IN // LANES):
            ys_ref[cb * (S5_CB_IN // LANES) + k] = y_cb[:, lane(k)]
    if nb <= lc:
        y = jnp.concatenate(
            [jnp.concatenate([ys_ref[k, pl.ds(b, lc, stride=nb), :] for b in range(nb)], axis=0)
             for k in range(nk)], axis=-1)
    else:
        for t in range(lc):
            for k in range(nk):
                hs_ref[k, pl.ds(t, nb, stride=lc), :] = ys_ref[k, t * nb:(t + 1) * nb, :]
        y = jnp.concatenate([hs_ref[k] for k in range(nk)], axis=-1)
    y = _gelu(y + d_ref[...] * h)
    z = _dot(y.astype(BF16), wglu_ref[...])
    out = x + z[:, :D_MODEL] * _sigmoid(z[:, D_MODEL:])
    out_ref[...] = out.reshape(out_ref.shape)
    sfin_ref[...] = st_ref[...]


def _s5_layer(x, s0_re, s0_im, norm_w, a, bbd, ccd, d, w_glu, *, lc):
    nb, seq, _ = x.shape
    nstate = 2 * S5_GROUPS * S5_STATE
    rows = nb * lc
    if seq == lc:
        xin = x.reshape(rows, D_MODEL)
        x_spec = pl.BlockSpec((rows, D_MODEL), lambda c: (0, 0))
    else:
        xin = x
        x_spec = pl.BlockSpec((nb, lc, D_MODEL), lambda c: (0, c, 0))
    out, sfin = pl.pallas_call(
        functools.partial(_s5_kernel, nb=nb, lc=lc),
        grid=(seq // lc,),
        in_specs=[
            x_spec,
            _const_spec((nb, nstate)),
            _const_spec((1, D_MODEL)),
            _const_spec((S5_CB, 2, S5_CB_STATE)),
            _const_spec((S5_CB, S5_CB_IN, 2 * S5_CB_STATE)),
            _const_spec((S5_CB, 2 * S5_CB_STATE, S5_CB_IN)),
            _const_spec((1, D_MODEL)),
            _const_spec((D_MODEL, 2 * D_MODEL)),
        ],
        out_specs=(x_spec, pl.BlockSpec((nb, nstate), lambda c: (0, 0))),
        out_shape=(jax.ShapeDtypeStruct(xin.shape, F32), jax.ShapeDtypeStruct((nb, nstate), F32)),
        scratch_shapes=[
            pltpu.VMEM((2 * S5_CB_STATE // LANES, rows, LANES), F32),
            pltpu.VMEM((nb, nstate), F32),
            pltpu.VMEM((D_MODEL // LANES, rows, LANES), F32),
            pltpu.VMEM((D_MODEL // LANES, rows, LANES), F32),
        ],
        compiler_params=pltpu.CompilerParams(
            dimension_semantics=("arbitrary",), vmem_limit_bytes=VMEM_LIMIT_BYTES),
        name="s5_layer",
    )(xin, _s5_pack_state(s0_re, s0_im), norm_w.reshape(1, D_MODEL), a, bbd, ccd,
      d.reshape(1, D_MODEL), w_glu)
    new_re, new_im = _s5_unpack_state(sfin)
    return out.reshape(x.shape), new_re, new_im


def _hgrn_gates(proj, lb):
    q = proj[:, 0:HG_FDIM]
    q = q * _sigmoid(q)
    fz = proj[:, HG_FDIM:2 * HG_FDIM]
    v = proj[:, 2 * HG_FDIM:2 * HG_FDIM + D_MODEL]
    g = proj[:, 2 * HG_FDIM + D_MODEL:]
    log_sig = jnp.minimum(fz, 0.0) - jnp.log1p(jnp.exp(-jnp.abs(fz)))
    t1 = jnp.log(lb)
    t2 = jnp.log1p(-lb) + log_sig
    hi = jnp.maximum(t1, t2)
    lo = jnp.minimum(t1, t2)
    logf = hi + jnp.log1p(jnp.exp(lo - hi))
    k = (1.0 - lb) * _sigmoid(-fz)
    return q, k, v, g, logf


def _hgrn_tile(q, k, v, g, logf, st_refs, nw, *, chunk):
    rows = q.shape[0]
    nseg = rows // chunk
    ri = lax.broadcasted_iota(jnp.int32, (rows, rows), 0)
    ci = lax.broadcasted_iota(jnp.int32, (rows, rows), 1)
    if nseg == 1:
        same = ci <= ri
    else:
        same = (ci <= ri) & ((ri // chunk) == (ci // chunk))
    tri = jnp.where(same, 1.0, 0.0).astype(F32)
    b = jnp.dot(tri, logf, precision=lax.Precision.HIGHEST, preferred_element_type=F32)
    rowid = lax.broadcasted_iota(jnp.int32, (rows, 1), 0)
    b_mid = b[chunk // 2:chunk // 2 + 1, :]
    b_last = b[chunk - 1:chunk, :]
    for s in range(1, nseg):
        in_s = rowid >= s * chunk
        b_mid = jnp.where(in_s, b[s * chunk + chunk // 2:s * chunk + chunk // 2 + 1, :], b_mid)
        b_last = jnp.where(in_s, b[s * chunk + chunk - 1:s * chunk + chunk, :], b_last)
    qs = (q * jnp.exp(b - b_mid)).astype(BF16)
    ks = (k * jnp.exp(b_mid - b)).astype(BF16)
    qi = (q * jnp.exp(b)).astype(BF16)
    kd = k * jnp.exp(b_last - b)
    vb = v.astype(BF16)
    outs = []
    for hd in range(HG_HEADS):
        sl = slice(hd * HG_DK, (hd + 1) * HG_DK)
        scores = jnp.where(same, _dot_nt(qs[:, sl], ks[:, sl]), 0.0)
        o = _dot(scores.astype(BF16), vb[:, sl])
        v_t = v[:, sl].T
        for s in range(nseg):
            st = st_refs[s][hd]
            o_s = _dot_nt(qi[:, sl], st.astype(BF16))
            kd_s = kd[:, sl]
            if nseg > 1:
                in_s = (rowid >= s * chunk) & (rowid < (s + 1) * chunk)
                o_s = jnp.where(in_s, o_s, 0.0)
                kd_s = jnp.where(in_s, kd_s, 0.0)
            o = o + o_s
            dec = jnp.exp(b[s * chunk + chunk - 1:s * chunk + chunk, sl])
            st_refs[s][hd] = dec * st + _dot(v_t.astype(BF16), kd_s.astype(BF16))
        o = o * lax.rsqrt(jnp.mean(o * o, axis=-1, keepdims=True) + RMS_EPS)
        gh = g[:, sl]
        outs.append(o * nw * (gh * _sigmoid(gh)))
    return jnp.concatenate(outs, axis=-1)


def _hgrn_prompt_kernel(x_ref, s0_ref, nw_ref, win_ref, lb_ref, hnw_ref, wout_ref,
                        out_ref, sout_ref, st_ref, y_ref, *, lc, chunk):
    c = pl.program_id(1)

    @pl.when(c == 0)
    def _():
        for hd in range(HG_HEADS):
            st_ref[hd] = s0_ref[0, hd].T

    x = x_ref[0]
    hb = _rmsnorm(x, nw_ref[...]).astype(BF16)
    proj = _dot(hb, win_ref[...])
    q, k, v, g, logf = _hgrn_gates(proj, lb_ref[...])
    for sub in range(lc // chunk):
        r = slice(sub * chunk, (sub + 1) * chunk)
        y_ref[r, :] = _hgrn_tile(q[r], k[r], v[r], g[r], logf[r], [st_ref], hnw_ref[...], chunk=chunk)
    out_ref[0] = x + _dot(y_ref[...].astype(BF16), wout_ref[...])

    @pl.when(c == pl.num_programs(1) - 1)
    def _():
        for hd in range(HG_HEADS):
            sout_ref[0, hd] = st_ref[hd].T


def _hgrn_sample_kernel(x_ref, s0_ref, nw_ref, win_ref, lb_ref, hnw_ref, wout_ref,
                        out_ref, sout_ref, st_ref, proj_ref, y_ref, *, seq):
    p = pl.program_id(0)
    nseg = SUBLANES // seq

    @pl.when(p == 0)
    def _():
        hb = _rmsnorm(x_ref[...], nw_ref[...]).astype(BF16)
        proj_ref[...] = _dot(hb, win_ref[...])

    for s in range(nseg):
        for hd in range(HG_HEADS):
            st_ref[s, hd] = s0_ref[s, hd].T
    r = pl.ds(pl.multiple_of(p * SUBLANES, SUBLANES), SUBLANES)
    q, k, v, g, logf = _hgrn_gates(proj_ref[r, :], lb_ref[...])
    y_ref[r, :] = _hgrn_tile(q, k, v, g, logf, [st_ref.at[s] for s in range(nseg)], hnw_ref[...],
                             chunk=seq)
    for s in range(nseg):
        for hd in range(HG_HEADS):
            sout_ref[s, hd] = st_ref[s, hd].T

    @pl.when(p == pl.num_programs(0) - 1)
    def _():
        out_ref[...] = x_ref[...] + _dot(y_ref[...].astype(BF16), wout_ref[...])


def _hgrn_layer(x, s0, norm_w, w_in, lb, hg_norm_w, w_out):
    nb, seq, _ = x.shape
    weights = (norm_w.reshape(1, D_MODEL), w_in.astype(BF16), lb.reshape(1, HG_FDIM),
               hg_norm_w.reshape(1, HG_DV), w_out.astype(BF16))
    w_specs = [
        _const_spec((1, D_MODEL)),
        _const_spec((D_MODEL, 2 * HG_FDIM + 2 * D_MODEL)),
        _const_spec((1, HG_FDIM)),
        _const_spec((1, HG_DV)),
        _const_spec((D_MODEL, D_MODEL)),
    ]
    s_shape = jax.ShapeDtypeStruct(s0.shape, F32)
    if seq >= HG_CHUNK:
        chunk = HG_CHUNK
        lc = HG_CHUNKS_PER_STEP * chunk
        x_spec = pl.BlockSpec((1, lc, D_MODEL), lambda b, c: (b, c, 0))
        s_spec = pl.BlockSpec((1, HG_HEADS, HG_DK, HG_DV), lambda b, c: (b, 0, 0, 0))
        out, s_new = pl.pallas_call(
            functools.partial(_hgrn_prompt_kernel, lc=lc, chunk=chunk),
            grid=(nb, seq // lc),
            in_specs=[x_spec, s_spec] + w_specs,
            out_specs=(x_spec, s_spec),
            out_shape=(jax.ShapeDtypeStruct(x.shape, F32), s_shape),
            scratch_shapes=[
                pltpu.VMEM((HG_HEADS, HG_DV, HG_DK), F32),
                pltpu.VMEM((lc, D_MODEL), F32),
            ],
            compiler_params=pltpu.CompilerParams(
                dimension_semantics=("arbitrary", "arbitrary"), vmem_limit_bytes=VMEM_LIMIT_BYTES),
            name="hgrn_prompt",
        )(x, s0, *weights)
        return out, s_new
    assert SUBLANES % seq == 0 and nb % (SUBLANES // seq) == 0
    nseg = SUBLANES // seq
    rows = nb * seq
    x_spec = _const_spec((rows, D_MODEL))
    s_spec = pl.BlockSpec((nseg, HG_HEADS, HG_DK, HG_DV), lambda p: (p, 0, 0, 0))
    out, s_new = pl.pallas_call(
        functools.partial(_hgrn_sample_kernel, seq=seq),
        grid=(nb // nseg,),
        in_specs=[x_spec, s_spec] + w_specs,
        out_specs=(pl.BlockSpec((rows, D_MODEL), lambda p: (0, 0)), s_spec),
        out_shape=(jax.ShapeDtypeStruct((rows, D_MODEL), F32), s_shape),
        scratch_shapes=[
            pltpu.VMEM((nseg, HG_HEADS, HG_DV, HG_DK), F32),
            pltpu.VMEM((rows, 2 * HG_FDIM + 2 * D_MODEL), F32),
            pltpu.VMEM((rows, D_MODEL), F32),
        ],
        compiler_params=pltpu.CompilerParams(
            dimension_semantics=("arbitrary",), vmem_limit_bytes=VMEM_LIMIT_BYTES),
        name="hgrn_sample",
    )(x.reshape(rows, D_MODEL), s0, *weights)
    return out.reshape(x.shape), s_new


def _compare_exchange(v, i, j):
    hi = jnp.maximum(v[i], v[j])
    lo = jnp.minimum(v[i], v[j])
    v[i] = hi
    v[j] = lo


def _bitonic_merge_desc(v):
    n = len(v)
    j = n // 2
    while j >= 1:
        for i in range(n):
            if i & j == 0:
                _compare_exchange(v, i, i + j)
        j //= 2


_SORT16 = (
    (0, 13), (1, 12), (2, 15), (3, 14), (4, 8), (5, 6), (7, 11), (9, 10),
    (0, 5), (1, 7), (2, 9), (3, 4), (6, 13), (8, 14), (10, 15), (11, 12),
    (0, 1), (2, 3), (4, 5), (6, 8), (7, 9), (10, 11), (12, 13), (14, 15),
    (0, 2), (1, 3), (4, 10), (5, 11), (6, 7), (8, 9), (12, 14), (13, 15),
    (1, 2), (3, 12), (4, 6), (5, 7), (8, 10), (9, 11), (13, 14),
    (1, 4), (2, 6), (5, 8), (7, 10), (9, 13), (11, 14),
    (2, 4), (3, 6), (9, 12), (11, 13),
    (3, 5), (6, 8), (7, 9), (10, 12),
    (3, 4), (5, 6), (7, 8), (9, 10), (11, 12),
    (6, 7), (8, 9),
)


def _sort_desc(v):
    assert len(v) == PEER_TOPK
    for i, j in _SORT16:
        _compare_exchange(v, i, j)
    return v


def _merge_top(a, b):
    n = len(a)
    v = [jnp.maximum(a[r], b[n - 1 - r]) for r in range(n)]
    _bitonic_merge_desc(v)
    return v


def _top16_rows(s):
    v = [s[SUBLANES * j:SUBLANES * (j + 1), :] for j in range(PEER_NKEYS // SUBLANES)]
    v = _sort_desc(v)
    for shift in (1, 2, 4):
        v = _merge_top(v, [pltpu.roll(x, shift, axis=0) for x in v])
    return v


def _candidate_top(a, b):
    n = PEER_TOPK
    low = jnp.full(a[0].shape, -jnp.inf, F32)
    runs = [
        [a[i] + b[0] for i in range(1, n)],
        [a[1] + b[j] for j in range(1, 8)],
        [a[2] + b[j] for j in range(1, 5)],
        [a[3] + b[j] for j in range(1, 4)],
        [a[4] + b[j] for j in range(1, 3)],
        [a[i] + b[1] for i in range(5, 8)],
    ]
    assert all((i + 1) * (j + 1) > n for i in range(1, n) for j in range(1, n)
               if not ((i == 1 and j < 8) or (i == 2 and j < 5) or (i == 3 and j < 4)
                       or (i == 4 and j < 3) or (5 <= i < 8 and j == 1)))
    top = [a[0] + b[j] for j in range(n)]
    for run in runs:
        top = _merge_top(top, run + [low] * (n - len(run)))
    return top


def _staircase_cells():
    return [(i, j) for i in range(PEER_TOPK) for j in range(PEER_TOPK) if (i + 1) * (j + 1) <= PEER_TOPK]


def _staircase_count(a, b, thr):
    n = None
    for i, j in _staircase_cells():
        hit = jnp.where(a[i] + b[j] >= thr, 1.0, 0.0)
        n = hit if n is None else n + hit
    return n


def _peer_select_kernel(x_ref, nw_ref, wq_ref, k1_ref, k2_ref,
                        hb_ref, s1_ref, s2_ref, thr_ref, tie_ref, t1_ref, t2_ref):
    hb = _rmsnorm(x_ref[...], nw_ref[...]).astype(BF16)
    hb_ref[...] = hb
    qb = _dot(hb, wq_ref[...]).astype(BF16)
    k1 = k1_ref[...]
    k2 = k2_ref[...]
    log2e = 1.0 / math.log(2.0)
    tied = None
    for hd in range(PEER_HEADS):
        base = hd * PEER_DKEY
        s1 = _dot_nt(k1, qb[:, base:base + PEER_HALF]) * log2e
        s2 = _dot_nt(k2, qb[:, base + PEER_HALF:base + PEER_DKEY]) * log2e
        s1_ref[hd] = s1
        s2_ref[hd] = s2
        for s, t_ref in ((s1, t1_ref), (s2, t2_ref)):
            top = _top16_rows(s)
            for r in range(PEER_TOPK):
                t_ref[r, hd:hd + 1, :] = top[r][0:1, :]
            n_top = jnp.sum(jnp.where(s >= top[PEER_TOPK - 1][0:1, :], 1.0, 0.0), axis=0, keepdims=True)
            over = jnp.where(n_top > PEER_TOPK, 1.0, 0.0)
            tied = over if tied is None else jnp.maximum(tied, over)
    a = [t1_ref[r] for r in range(PEER_TOPK)]
    b = [t2_ref[r] for r in range(PEER_TOPK)]
    top = _candidate_top(a, b)
    z = jnp.exp2(top[0] - top[0])
    for r in range(1, PEER_TOPK):
        z = z + jnp.exp2(top[r] - top[0])
    shift = top[0] + jnp.log2(z) + 1.0
    for hd in range(PEER_HEADS):
        s1_ref[hd] = s1_ref[hd] - shift[hd:hd + 1, :]
    a_shifted = [v - shift for v in a]
    thr = _candidate_top(a_shifted, b)[PEER_TOPK - 1]
    thr_ref[...] = thr
    n_sel = _staircase_count(a, b, top[PEER_TOPK - 1])
    over = jnp.max(jnp.where(n_sel > PEER_TOPK, 1.0, 0.0), axis=0, keepdims=True)
    flag = jnp.max(jnp.maximum(tied, over), axis=1, keepdims=True)
    tie_ref[0] = jnp.broadcast_to(flag, tie_ref.shape[1:])


def _ordered_rank(s, top):
    n, w = s.shape
    ri = lax.broadcasted_iota(jnp.int32, (n, n), 0)
    ci = lax.broadcasted_iota(jnp.int32, (n, n), 1)
    lower = jnp.where(ci < ri, 1.0, 0.0).astype(BF16)
    larger = None
    equal = []
    for t in top:
        hit = jnp.where(t > s, 1.0, 0.0)
        larger = hit if larger is None else larger + hit
        equal.append(jnp.where(s == t, 1.0, 0.0))
    before = _dot(lower, jnp.concatenate(equal, axis=1).astype(BF16))
    same = None
    for r, eq in enumerate(equal):
        cnt = eq * before[:, r * w:(r + 1) * w]
        same = cnt if same is None else jnp.maximum(same, cnt)
    return jnp.minimum(larger + same, float(PEER_TOPK))


def _peer_select_ordered_kernel(tied_ref, x_ref, nw_ref, wq_ref, k1_ref, k2_ref,
                                c1_ref, r2_ref, t1_ref, t2_ref, r1_ref):
    tied = tied_ref[pl.program_id(0)] != 0

    @pl.when(jnp.logical_not(tied))
    def _():
        c1_ref[...] = jnp.zeros_like(c1_ref)
        r2_ref[...] = jnp.zeros_like(r2_ref)

    @pl.when(tied)
    def _():
        _ordered_selection(x_ref, nw_ref, wq_ref, k1_ref, k2_ref, c1_ref, r2_ref, t1_ref, t2_ref, r1_ref)


def _ordered_selection(x_ref, nw_ref, wq_ref, k1_ref, k2_ref, c1_ref, r2_ref, t1_ref, t2_ref, r1_ref):
    hb = _rmsnorm(x_ref[...], nw_ref[...]).astype(BF16)
    qb = _dot(hb, wq_ref[...]).astype(BF16)
    k1 = k1_ref[...]
    k2 = k2_ref[...]
    log2e = 1.0 / math.log(2.0)
    for hd in range(PEER_HEADS):
        base = hd * PEER_DKEY
        s1 = _dot_nt(k1, qb[:, base:base + PEER_HALF]) * log2e
        s2 = _dot_nt(k2, qb[:, base + PEER_HALF:base + PEER_DKEY]) * log2e
        for s, t_ref, rank_ref in ((s1, t1_ref, r1_ref), (s2, t2_ref, r2_ref)):
            top = [v[0:1, :] for v in _top16_rows(s)]
            rank_ref[hd] = _ordered_rank(s, top)
            for r in range(PEER_TOPK):
                t_ref[r, hd:hd + 1, :] = top[r]
    a = [t1_ref[r] for r in range(PEER_TOPK)]
    b = [t2_ref[r] for r in range(PEER_TOPK)]
    cells = _staircase_cells()
    sums = [a[i] + b[j] for i, j in cells]
    ahead = [None] * len(cells)

    def bump(k, hit):
        ahead[k] = hit if ahead[k] is None else ahead[k] + hit

    for p in range(len(cells)):
        for q in range(p + 1, len(cells)):
            bump(q, jnp.where(sums[p] >= sums[q], 1.0, 0.0))
            bump(p, jnp.where(sums[q] > sums[p], 1.0, 0.0))
    count = [None] * PEER_TOPK
    for (i, j), n in zip(cells, ahead):
        ch = jnp.where(n < PEER_TOPK, 1.0, 0.0)
        count[i] = ch if count[i] is None else count[i] + ch
    for hd in range(PEER_HEADS):
        row = slice(hd, hd + 1)
        rank1 = r1_ref[hd]
        c1 = jnp.zeros(rank1.shape, F32)
        for r in range(PEER_TOPK):
            c1 = jnp.where(rank1 == r, count[r][row, :], c1)
        c1_ref[hd] = c1


def _peer_expert_kernel(tied_ref, hb_ref, x_ref, s1_ref, s2_ref, thr_ref, c1_ref, r2_ref,
                        u_ref, vp_ref, v_ref, fw_ref, out_ref, acc_ref, *tile_refs, tb, eb, tw, final_norm):
    e = pl.program_id(1)
    tied = tied_ref[pl.program_id(0)] != 0
    ntile = tb // tw
    ht_refs, at_refs = tile_refs[:ntile], tile_refs[ntile:]

    @pl.when(e == 0)
    def _():
        acc_ref[...] = jnp.zeros_like(acc_ref)
        for at_ref in at_refs:
            at_ref[...] = jnp.zeros_like(at_ref)

    def step(ordered):
        u = u_ref[...]
        vp = vp_ref[...]
        for i in range(ntile):
            ht_refs[i][...] = _dot_nt(u, hb_ref[i * tw:(i + 1) * tw, :])
            acc_ref[:, i * tw:(i + 1) * tw] += _dot_tn(vp, at_refs[i][...])
        for i in range(ntile):
            for lt in range(tw // LANES):
                lcol = slice(lt * LANES, (lt + 1) * LANES)
                col = slice(i * tw + lt * LANES, i * tw + (lt + 1) * LANES)
                for r in range(eb // PEER_NKEYS):
                    gate = None
                    for hd in range(PEER_HEADS):
                        t = s2_ref[hd, :, col] + s1_ref[hd, r:r + 1, col]
                        if ordered:
                            picked = r2_ref[hd, :, col] < c1_ref[hd, r:r + 1, col]
                        else:
                            picked = t >= thr_ref[hd:hd + 1, col]
                        g = jnp.where(picked, jnp.exp2(t), 0.0)
                        gate = g if gate is None else gate + g
                    rr = slice(r * PEER_NKEYS, (r + 1) * PEER_NKEYS)
                    h = ht_refs[i][rr, lcol]
                    at_refs[i][rr, lcol] = (
                        h * (1.0 + lax.erf(h * (1.0 / math.sqrt(2.0)))) * gate).astype(BF16)

    pl.when(jnp.logical_not(tied))(functools.partial(step, False))
    pl.when(tied)(functools.partial(step, True))

    @pl.when(e == pl.num_programs(1) - 1)
    def _():
        v = v_ref[...]
        for i in range(ntile):
            acc_ref[:, i * tw:(i + 1) * tw] += _dot_tn(v, at_refs[i][...])
        y = x_ref[...] + acc_ref[...].T
        if final_norm:
            y = _rmsnorm(y, fw_ref[...])
        out_ref[...] = y


def _peer_layer(x, norm_w, w_q, keys1, keys2, u_tab, v_tab, final_w, *, layer, final_norm):
    t_tot = x.shape[0]
    tbs = min(PEER_SELECT_TOKENS, t_tot)
    tb = min(PEER_EXPERT_TOKENS, t_tot)
    eb = PEER_EXPERT_BLOCK
    tw = PEER_TOKEN_TILE
    nsel = t_tot // tbs
    select_in = (x, norm_w.reshape(1, D_MODEL), w_q.astype(BF16), keys1.astype(BF16), keys2.astype(BF16))
    select_specs = [
        pl.BlockSpec((tbs, D_MODEL), lambda i: (i, 0)),
        _const_spec((1, D_MODEL)),
        _const_spec((D_MODEL, PEER_HEADS * PEER_DKEY)),
        _const_spec((PEER_NKEYS, PEER_HALF)),
        _const_spec((PEER_NKEYS, PEER_HALF)),
    ]
    select_params = pltpu.CompilerParams(dimension_semantics=("arbitrary",), vmem_limit_bytes=VMEM_LIMIT_BYTES)
    key_spec = pl.BlockSpec((PEER_HEADS, PEER_NKEYS, tbs), lambda i: (0, 0, i))
    key_f32 = jax.ShapeDtypeStruct((PEER_HEADS, PEER_NKEYS, t_tot), F32)
    top_scratch = pltpu.VMEM((PEER_TOPK, PEER_HEADS, tbs), F32)
    hb, s1, s2, thr, tie = pl.pallas_call(
        _peer_select_kernel,
        grid=(nsel,),
        in_specs=select_specs,
        out_specs=(pl.BlockSpec((tbs, D_MODEL), lambda i: (i, 0)), key_spec, key_spec,
                   pl.BlockSpec((PEER_HEADS, tbs), lambda i: (0, i)),
                   pl.BlockSpec((1, SUBLANES, LANES), lambda i: (i, 0, 0))),
        out_shape=(jax.ShapeDtypeStruct((t_tot, D_MODEL), BF16), key_f32, key_f32,
                   jax.ShapeDtypeStruct((PEER_HEADS, t_tot), F32),
                   jax.ShapeDtypeStruct((nsel, SUBLANES, LANES), F32)),
        scratch_shapes=[top_scratch, top_scratch],
        compiler_params=select_params,
        name="peer_select",
    )(*select_in)

    tied_exp = jnp.max((tie[:, 0, 0] > 0.0).astype(jnp.int32).reshape(t_tot // tb, tb // tbs), axis=1)
    tied_sel = jnp.repeat(tied_exp, tb // tbs)
    c1, r2 = pl.pallas_call(
        _peer_select_ordered_kernel,
        grid_spec=pltpu.PrefetchScalarGridSpec(
            num_scalar_prefetch=1,
            grid=(nsel,),
            in_specs=[
                pl.BlockSpec((tbs, D_MODEL), lambda i, tied: (i, 0)),
                pl.BlockSpec((1, D_MODEL), lambda i, tied: (0, 0)),
                pl.BlockSpec((D_MODEL, PEER_HEADS * PEER_DKEY), lambda i, tied: (0, 0)),
                pl.BlockSpec((PEER_NKEYS, PEER_HALF), lambda i, tied: (0, 0)),
                pl.BlockSpec((PEER_NKEYS, PEER_HALF), lambda i, tied: (0, 0)),
            ],
            out_specs=(pl.BlockSpec((PEER_HEADS, PEER_NKEYS, tbs), lambda i, tied: (0, 0, i)),
                       pl.BlockSpec((PEER_HEADS, PEER_NKEYS, tbs), lambda i, tied: (0, 0, i))),
            scratch_shapes=[top_scratch, top_scratch,
                            pltpu.VMEM((PEER_HEADS, PEER_NKEYS, tbs), F32)],
        ),
        out_shape=(key_f32, key_f32),
        compiler_params=select_params,
        name="peer_select_ordered",
    )(tied_sel, *select_in)

    once = pl.Buffered(1)
    tok = lambda i, e, tied: (i, 0)
    row_spec = pl.BlockSpec((PEER_HEADS, eb // PEER_NKEYS, tb), lambda i, e, tied: (0, e, i))
    all_spec = pl.BlockSpec((PEER_HEADS, PEER_NKEYS, tb), lambda i, e, tied: (0, 0, i), pipeline_mode=once)
    return pl.pallas_call(
        functools.partial(_peer_expert_kernel, tb=tb, eb=eb, tw=tw, final_norm=final_norm),
        grid_spec=pltpu.PrefetchScalarGridSpec(
            num_scalar_prefetch=1,
            grid=(t_tot // tb, PEER_EXPERTS // eb),
            in_specs=[
                pl.BlockSpec((tb, D_MODEL), tok, pipeline_mode=once),
                pl.BlockSpec((tb, D_MODEL), tok, pipeline_mode=once),
                row_spec,
                all_spec,
                pl.BlockSpec((PEER_HEADS, tb), lambda i, e, tied: (0, i), pipeline_mode=once),
                row_spec,
                all_spec,
                pl.BlockSpec((None, eb, D_MODEL), lambda i, e, tied: (layer, e, 0)),
                pl.BlockSpec((None, eb, D_MODEL), lambda i, e, tied: (layer, jnp.maximum(e - 1, 0), 0)),
                pl.BlockSpec((None, eb, D_MODEL), lambda i, e, tied: (layer, PEER_EXPERTS // eb - 1, 0),
                             pipeline_mode=once),
                pl.BlockSpec((1, D_MODEL), lambda i, e, tied: (0, 0), pipeline_mode=once),
            ],
            out_specs=pl.BlockSpec((tb, D_MODEL), tok),
            scratch_shapes=(
                [pltpu.VMEM((D_MODEL, tb), F32)]
                + [pltpu.VMEM((eb, tw), F32) for _ in range(tb // tw)]
                + [pltpu.VMEM((eb, tw), BF16) for _ in range(tb // tw)]
            ),
        ),
        out_shape=jax.ShapeDtypeStruct((t_tot, D_MODEL), F32),
        compiler_params=pltpu.CompilerParams(
            dimension_semantics=("arbitrary", "arbitrary"), vmem_limit_bytes=VMEM_LIMIT_BYTES),
        name="peer_experts",
    )(tied_exp, hb, x, s1, s2, thr, c1, r2, u_tab, v_tab, v_tab, final_w.reshape(1, D_MODEL))


def _trunk(x, s5_re, s5_im, s_hg, p, s5_prm, lb_all):
    nb, seq, _ = x.shape
    new_re, new_im, new_hg = [], [], []
    for i in range(DEPTH):
        j = i // N_MIXERS
        if i % N_MIXERS == 0:
            a, bbd, ccd = s5_prm[j]
            x, sr, si = _s5_layer(x, s5_re[j], s5_im[j], p["norm_mix"][i], a, bbd, ccd,
                                  p["s5_d"][j], p["s5_w_glu"][j].astype(BF16), lc=min(seq, S5_TIME_CHUNK))
            new_re.append(sr)
            new_im.append(si)
        else:
            x, s_new = _hgrn_layer(x, s_hg[j], p["norm_mix"][i], p["hg_w_in"][j], lb_all[i],
                                   p["hg_norm_w"][j], p["hg_w_out"][j])
            new_hg.append(s_new)
        x = _peer_layer(x.reshape(nb * seq, D_MODEL), p["norm_ffn"][i], p["peer_w_q"][i],
                        p["peer_keys1"][i], p["peer_keys2"][i], p["peer_u"], p["peer_v"],
                        p["norm_final"], layer=i, final_norm=(i == DEPTH - 1)).reshape(nb, seq, D_MODEL)
    return x, jnp.stack(new_re), jnp.stack(new_im), jnp.stack(new_hg)


def kernel(x_prompt, x_sample, state_s5_re, state_s5_im, state_hgrn, norm_mix, norm_ffn, norm_final,
           s5_lambda_re, s5_lambda_im, s5_log_dt, s5_b_re, s5_b_im, s5_c_re, s5_c_im, s5_d, s5_w_glu,
           hg_w_in, hg_lower_bounds, hg_norm_w, hg_w_out, peer_w_q, peer_keys1, peer_keys2, peer_u, peer_v):
    p = dict(norm_mix=norm_mix, norm_ffn=norm_ffn, norm_final=norm_final, s5_d=s5_d, s5_w_glu=s5_w_glu,
             hg_w_in=hg_w_in, hg_norm_w=hg_norm_w, hg_w_out=hg_w_out, peer_w_q=peer_w_q,
             peer_keys1=peer_keys1, peer_keys2=peer_keys2,
             peer_u=peer_u.astype(BF16), peer_v=peer_v.astype(BF16))
    n_a = s5_lambda_re.shape[0]
    s5_prm = [_s5_params(s5_lambda_re[j], s5_lambda_im[j], s5_log_dt[j], s5_b_re[j], s5_b_im[j],
                         s5_c_re[j], s5_c_im[j]) for j in range(n_a)]
    lb_all = jnp.cumsum(jax.nn.softmax(hg_lower_bounds.astype(F32), axis=0), axis=0)
    lb_all = lb_all - lb_all[0:1]
    nbp = x_prompt.shape[0]
    z_s5 = jnp.zeros((n_a, nbp, S5_GROUPS, S5_STATE), state_s5_re.dtype)
    z_hg = jnp.zeros((state_hgrn.shape[0], nbp, HG_HEADS, HG_DK, HG_DV), state_hgrn.dtype)
    y_p, re_p, im_p, hg_p = _trunk(x_prompt, z_s5, z_s5, z_hg, p, s5_prm, lb_all)
    y_s, re_s, im_s, hg_s = _trunk(x_sample, state_s5_re, state_s5_im, state_hgrn, p, s5_prm, lb_all)
    return (y_p, y_s, re_p, im_p, hg_p, re_s, im_s, hg_s)
```

```python
import functools
import math

import jax
import jax.numpy as jnp
from jax import lax
from jax.experimental import pallas as pl
from jax.experimental.pallas import tpu as pltpu

F32 = jnp.float32
BF16 = jnp.bfloat16

D_MODEL = 1024
DEPTH = 2
N_MIXERS = 2
S5_GROUP = 16
S5_GROUPS = D_MODEL // S5_GROUP
S5_STATE = 64
HG_HEADS = 8
HG_DK = 128
HG_DV = 128
HG_FDIM = HG_HEADS * HG_DK
HG_CHUNK = 64
PEER_HEADS = 8
PEER_NKEYS = 128
PEER_EXPERTS = PEER_NKEYS * PEER_NKEYS
PEER_TOPK = 16
PEER_DKEY = 256
PEER_HALF = PEER_DKEY // 2
RMS_EPS = 1e-6

LANES = 128
SUBLANES = 8
MXU_WIDTH = 256
VMEM_LIMIT_BYTES = 56 * 2**20

S5_TIME_CHUNK = 64
HG_CHUNKS_PER_STEP = 8
PEER_SELECT_TOKENS = 512
PEER_EXPERT_TOKENS = 1024
PEER_EXPERT_BLOCK = 8 * PEER_NKEYS
PEER_TOKEN_TILE = MXU_WIDTH

S5_CB = 4
S5_CB_GROUPS = S5_GROUPS // S5_CB
S5_CB_IN = S5_CB_GROUPS * S5_GROUP
S5_CB_STATE = S5_CB_GROUPS * S5_STATE


def _const_spec(shape):
    nd = len(shape)
    return pl.BlockSpec(shape, lambda *_: (0,) * nd, pipeline_mode=pl.Buffered(1))


def _rmsnorm(x, w):
    ms = jnp.mean(x * x, axis=-1, keepdims=True)
    return x * lax.rsqrt(ms + RMS_EPS) * w


def _gelu(x):
    return 0.5 * x * (1.0 + lax.erf(x * (1.0 / math.sqrt(2.0))))


def _sigmoid(x):
    return 1.0 / (1.0 + jnp.exp(-x))


def _dot(a, b):
    return jnp.dot(a, b, preferred_element_type=F32)


def _dot_nt(a, b):
    return lax.dot_general(a, b, (((1,), (1,)), ((), ())), preferred_element_type=F32)


def _dot_tn(a, b):
    return lax.dot_general(a, b, (((0,), (0,)), ((), ())), preferred_element_type=F32)


def _s5_discretize_kernel(lre_ref, lim_ref, ldt_ref, are_ref, aim_ref, zre_ref, zim_ref):
    lr = jnp.minimum(lre_ref[...], -1e-4)
    li = lim_ref[...]
    dt = jnp.exp(ldt_ref[...])
    mag = jnp.exp(lr * dt)
    ang = li * dt
    ab_re = mag * jnp.cos(ang)
    ab_im = mag * jnp.sin(ang)
    den = lr * lr + li * li
    nr = ab_re - 1.0
    are_ref[...] = ab_re
    aim_ref[...] = ab_im
    zre_ref[...] = (nr * lr + ab_im * li) / den
    zim_ref[...] = (ab_im * lr - nr * li) / den


def _s5_scale_b_kernel(zre_ref, zim_ref, bre_ref, bim_ref, ore_ref, oim_ref):
    zr = zre_ref[...]
    zi = zim_ref[...]
    br = bre_ref[...]
    bi = bim_ref[...]
    ore_ref[...] = zr * br - zi * bi
    oim_ref[...] = zr * bi + zi * br


def _s5_params(lam_re, lam_im, log_dt, b_re, b_im, c_re, c_im):
    g, p, hh = S5_GROUPS, S5_STATE, S5_GROUP
    gp = jax.ShapeDtypeStruct((g, p), F32)
    a_re, a_im, z_re, z_im = pl.pallas_call(
        _s5_discretize_kernel, out_shape=(gp, gp, gp, gp), name="s5_discretize",
    )(lam_re.astype(F32), lam_im.astype(F32), log_dt.astype(F32).reshape(g, 1))
    gph = jax.ShapeDtypeStruct((g * p, hh), F32)
    bb_re, bb_im = pl.pallas_call(
        _s5_scale_b_kernel, out_shape=(gph, gph), name="s5_scale_b",
    )(z_re.reshape(g * p, 1), z_im.reshape(g * p, 1),
      b_re.astype(F32).reshape(g * p, hh), b_im.astype(F32).reshape(g * p, hh))
    eye = jnp.eye(S5_CB_GROUPS, dtype=F32)

    def blockdiag_in(w):
        w4 = w.reshape(S5_CB, S5_CB_GROUPS, p, hh)
        return jnp.einsum('cgph,gk->cghkp', w4, eye).reshape(S5_CB, S5_CB_IN, S5_CB_STATE)

    def blockdiag_out(w):
        w4 = w.astype(F32).reshape(S5_CB, S5_CB_GROUPS, hh, p)
        return jnp.einsum('cghp,gk->cgpkh', w4, eye).reshape(S5_CB, S5_CB_STATE, S5_CB_IN)

    bbd = jnp.concatenate([blockdiag_in(bb_re), blockdiag_in(bb_im)], axis=-1).astype(BF16)
    ccd = jnp.concatenate([blockdiag_out(c_re), -blockdiag_out(c_im)], axis=1).astype(BF16)
    a = jnp.stack([a_re.reshape(S5_CB, S5_CB_STATE), a_im.reshape(S5_CB, S5_CB_STATE)], axis=1)
    return a, bbd, ccd


def _s5_pack_state(s_re, s_im):
    nb = s_re.shape[0]
    return jnp.stack([s_re.reshape(nb, S5_CB, S5_CB_STATE), s_im.reshape(nb, S5_CB, S5_CB_STATE)],
                     axis=2).reshape(nb, 2 * S5_GROUPS * S5_STATE)


def _s5_unpack_state(s):
    nb = s.shape[0]
    s4 = s.reshape(nb, S5_CB, 2, S5_CB_STATE)
    return (s4[:, :, 0].reshape(nb, S5_GROUPS, S5_STATE), s4[:, :, 1].reshape(nb, S5_GROUPS, S5_STATE))


def _s5_kernel(x_ref, s0_ref, nw_ref, a_ref, bbd_ref, ccd_ref, d_ref, wglu_ref,
               out_ref, sfin_ref, bu_ref, st_ref, hs_ref, ys_ref, *, nb, lc):
    rows = nb * lc
    nk = D_MODEL // LANES
    lane = lambda k: slice(k * LANES, (k + 1) * LANES)

    @pl.when(pl.program_id(0) == 0)
    def _():
        st_ref[...] = s0_ref[...]

    x = x_ref[...].reshape(rows, D_MODEL)
    h = _rmsnorm(x, nw_ref[...])
    if nb <= lc:
        for b in range(nb):
            for k in range(nk):
                hs_ref[k, pl.ds(b, lc, stride=nb), :] = h[b * lc:(b + 1) * lc, lane(k)]
        hb = jnp.concatenate([hs_ref[k] for k in range(nk)], axis=-1).astype(BF16)
    else:
        for k in range(nk):
            hs_ref[k] = h[:, lane(k)]
        hb = jnp.concatenate(
            [jnp.concatenate([hs_ref[k, pl.ds(t, nb, stride=lc), :] for t in range(lc)], axis=0)
             for k in range(nk)], axis=-1).astype(BF16)
    nt = S5_CB_STATE // LANES
    for cb in range(S5_CB):
        bu = _dot(hb[:, cb * S5_CB_IN:(cb + 1) * S5_CB_IN], bbd_ref[cb])
        for k in range(2 * nt):
            bu_ref[k] = bu[:, lane(k)]
        a_re = [jnp.broadcast_to(a_ref[cb, 0:1, lane(k)], (SUBLANES, LANES)) for k in range(nt)]
        a_im = [jnp.broadcast_to(a_ref[cb, 1:2, lane(k)], (SUBLANES, LANES)) for k in range(nt)]
        col = cb * 2 * S5_CB_STATE
        for j in range(nb // SUBLANES):
            brow = j * SUBLANES

            def step(t, carry, brow=brow):
                idx = pl.ds(pl.multiple_of(t * nb + brow, SUBLANES), SUBLANES)
                new_re, new_im = [], []
                for k in range(nt):
                    s_re, s_im = carry[k], carry[nt + k]
                    n_re = a_re[k] * s_re - a_im[k] * s_im + bu_ref[k, idx, :]
                    n_im = a_re[k] * s_im + a_im[k] * s_re + bu_ref[nt + k, idx, :]
                    bu_ref[k, idx, :] = n_re
                    bu_ref[nt + k, idx, :] = n_im
                    new_re.append(n_re)
                    new_im.append(n_im)
                return tuple(new_re) + tuple(new_im)

            carry = tuple(st_ref[brow:brow + SUBLANES, col + k * LANES:col + (k + 1) * LANES]
                          for k in range(2 * nt))
            if lc <= 8:
                for t in range(lc):
                    carry = step(t, carry)
            else:
                carry = lax.fori_loop(0, lc, step, carry, unroll=4)
            for k in range(2 * nt):
                st_ref[brow:brow + SUBLANES, col + k * LANES:col + (k + 1) * LANES] = carry[k]
        st_all = jnp.concatenate([bu_ref[k] for k in range(2 * nt)], axis=-1)
        y_cb = _dot(st_all.astype(BF16), ccd_ref[cb])
        for k in range(S5_CB_IN // LANES):
            ys_ref[cb * (S5_CB_IN // LANES) + k] = y_cb[:, lane(k)]
    if nb <= lc:
        y = jnp.concatenate(
            [jnp.concatenate([ys_ref[k, pl.ds(b, lc, stride=nb), :] for b in range(nb)], axis=0)
             for k in range(nk)], axis=-1)
    else:
        for t in range(lc):
            for k in range(nk):
                hs_ref[k, pl.ds(t, nb, stride=lc), :] = ys_ref[k, t * nb:(t + 1) * nb, :]
        y = jnp.concatenate([hs_ref[k] for k in range(nk)], axis=-1)
    y = _gelu(y + d_ref[...] * h)
    z = _dot(y.astype(BF16), wglu_ref[...])
    out = x + z[:, :D_MODEL] * _sigmoid(z[:, D_MODEL:])
    out_ref[...] = out.reshape(out_ref.shape)
    sfin_ref[...] = st_ref[...]


def _s5_layer(x, s0_re, s0_im, norm_w, a, bbd, ccd, d, w_glu, *, lc):
    nb, seq, _ = x.shape
    nstate = 2 * S5_GROUPS * S5_STATE
    rows = nb * lc
    if seq == lc:
        xin = x.reshape(rows, D_MODEL)
        x_spec = pl.BlockSpec((rows, D_MODEL), lambda c: (0, 0))
    else:
        xin = x
        x_spec = pl.BlockSpec((nb, lc, D_MODEL), lambda c: (0, c, 0))
    out, sfin = pl.pallas_call(
        functools.partial(_s5_kernel, nb=nb, lc=lc),
        grid=(seq // lc,),
        in_specs=[
            x_spec,
            _const_spec((nb, nstate)),
            _const_spec((1, D_MODEL)),
            _const_spec((S5_CB, 2, S5_CB_STATE)),
            _const_spec((S5_CB, S5_CB_IN, 2 * S5_CB_STATE)),
            _const_spec((S5_CB, 2 * S5_CB_STATE, S5_CB_IN)),
            _const_spec((1, D_MODEL)),
            _const_spec((D_MODEL, 2 * D_MODEL)),
        ],
        out_specs=(x_spec, pl.BlockSpec((nb, nstate), lambda c: (0, 0))),
        out_shape=(jax.ShapeDtypeStruct(xin.shape, F32), jax.ShapeDtypeStruct((nb, nstate), F32)),
        scratch_shapes=[
            pltpu.VMEM((2 * S5_CB_STATE // LANES, rows, LANES), F32),
            pltpu.VMEM((nb, nstate), F32),
            pltpu.VMEM((D_MODEL // LANES, rows, LANES), F32),
            pltpu.VMEM((D_MODEL // LANES, rows, LANES), F32),
        ],
        compiler_params=pltpu.CompilerParams(
            dimension_semantics=("arbitrary",), vmem_limit_bytes=VMEM_LIMIT_BYTES),
        name="s5_layer",
    )(xin, _s5_pack_state(s0_re, s0_im), norm_w.reshape(1, D_MODEL), a, bbd, ccd,
      d.reshape(1, D_MODEL), w_glu)
    new_re, new_im = _s5_unpack_state(sfin)
    return out.reshape(x.shape), new_re, new_im


def _hgrn_gates(proj, lb):
    q = proj[:, 0:HG_FDIM]
    q = q * _sigmoid(q)
    fz = proj[:, HG_FDIM:2 * HG_FDIM]
    v = proj[:, 2 * HG_FDIM:2 * HG_FDIM + D_MODEL]
    g = proj[:, 2 * HG_FDIM + D_MODEL:]
    log_sig = jnp.minimum(fz, 0.0) - jnp.log1p(jnp.exp(-jnp.abs(fz)))
    t1 = jnp.log(lb)
    t2 = jnp.log1p(-lb) + log_sig
    hi = jnp.maximum(t1, t2)
    lo = jnp.minimum(t1, t2)
    logf = hi + jnp.log1p(jnp.exp(lo - hi))
    k = (1.0 - lb) * _sigmoid(-fz)
    return q, k, v, g, logf


def _hgrn_tile(q, k, v, g, logf, st_refs, nw, *, chunk):
    rows = q.shape[0]
    nseg = rows // chunk
    ri = lax.broadcasted_iota(jnp.int32, (rows, rows), 0)
    ci = lax.broadcasted_iota(jnp.int32, (rows, rows), 1)
    if nseg == 1:
        same = ci <= ri
    else:
        same = (ci <= ri) & ((ri // chunk) == (ci // chunk))
    tri = jnp.where(same, 1.0, 0.0).astype(F32)
    b = jnp.dot(tri, logf, precision=lax.Precision.HIGHEST, preferred_element_type=F32)
    rowid = lax.broadcasted_iota(jnp.int32, (rows, 1), 0)
    b_mid = b[chunk // 2:chunk // 2 + 1, :]
    b_last = b[chunk - 1:chunk, :]
    for s in range(1, nseg):
        in_s = rowid >= s * chunk
        b_mid = jnp.where(in_s, b[s * chunk + chunk // 2:s * chunk + chunk // 2 + 1, :], b_mid)
        b_last = jnp.where(in_s, b[s * chunk + chunk - 1:s * chunk + chunk, :], b_last)
    qs = (q * jnp.exp(b - b_mid)).astype(BF16)
    ks = (k * jnp.exp(b_mid - b)).astype(BF16)
    qi = (q * jnp.exp(b)).astype(BF16)
    kd = k * jnp.exp(b_last - b)
    vb = v.astype(BF16)
    outs = []
    for hd in range(HG_HEADS):
        sl = slice(hd * HG_DK, (hd + 1) * HG_DK)
        scores = jnp.where(same, _dot_nt(qs[:, sl], ks[:, sl]), 0.0)
        o = _dot(scores.astype(BF16), vb[:, sl])
        v_t = v[:, sl].T
        for s in range(nseg):
            st = st_refs[s][hd]
            o_s = _dot_nt(qi[:, sl], st.astype(BF16))
            kd_s = kd[:, sl]
            if nseg > 1:
                in_s = (rowid >= s * chunk) & (rowid < (s + 1) * chunk)
                o_s = jnp.where(in_s, o_s, 0.0)
                kd_s = jnp.where(in_s, kd_s, 0.0)
            o = o + o_s
            dec = jnp.exp(b[s * chunk + chunk - 1:s * chunk + chunk, sl])
            st_refs[s][hd] = dec * st + _dot(v_t.astype(BF16), kd_s.astype(BF16))
        o = o * lax.rsqrt(jnp.mean(o * o, axis=-1, keepdims=True) + RMS_EPS)
        gh = g[:, sl]
        outs.append(o * nw * (gh * _sigmoid(gh)))
    return jnp.concatenate(outs, axis=-1)


def _hgrn_prompt_kernel(x_ref, s0_ref, nw_ref, win_ref, lb_ref, hnw_ref, wout_ref,
                        out_ref, sout_ref, st_ref, y_ref, *, lc, chunk):
    c = pl.program_id(1)

    @pl.when(c == 0)
    def _():
        for hd in range(HG_HEADS):
            st_ref[hd] = s0_ref[0, hd].T

    x = x_ref[0]
    hb = _rmsnorm(x, nw_ref[...]).astype(BF16)
    proj = _dot(hb, win_ref[...])
    q, k, v, g, logf = _hgrn_gates(proj, lb_ref[...])
    for sub in range(lc // chunk):
        r = slice(sub * chunk, (sub + 1) * chunk)
        y_ref[r, :] = _hgrn_tile(q[r], k[r], v[r], g[r], logf[r], [st_ref], hnw_ref[...], chunk=chunk)
    out_ref[0] = x + _dot(y_ref[...].astype(BF16), wout_ref[...])

    @pl.when(c == pl.num_programs(1) - 1)
    def _():
        for hd in range(HG_HEADS):
            sout_ref[0, hd] = st_ref[hd].T


def _hgrn_sample_kernel(x_ref, s0_ref, nw_ref, win_ref, lb_ref, hnw_ref, wout_ref,
                        out_ref, sout_ref, st_ref, proj_ref, y_ref, *, seq):
    p = pl.program_id(0)
    nseg = SUBLANES // seq

    @pl.when(p == 0)
    def _():
        hb = _rmsnorm(x_ref[...], nw_ref[...]).astype(BF16)
        proj_ref[...] = _dot(hb, win_ref[...])

    for s in range(nseg):
        for hd in range(HG_HEADS):
            st_ref[s, hd] = s0_ref[s, hd].T
    r = pl.ds(pl.multiple_of(p * SUBLANES, SUBLANES), SUBLANES)
    q, k, v, g, logf = _hgrn_gates(proj_ref[r, :], lb_ref[...])
    y_ref[r, :] = _hgrn_tile(q, k, v, g, logf, [st_ref.at[s] for s in range(nseg)], hnw_ref[...],
                             chunk=seq)
    for s in range(nseg):
        for hd in range(HG_HEADS):
            sout_ref[s, hd] = st_ref[s, hd].T

    @pl.when(p == pl.num_programs(0) - 1)
    def _():
        out_ref[...] = x_ref[...] + _dot(y_ref[...].astype(BF16), wout_ref[...])


def _hgrn_layer(x, s0, norm_w, w_in, lb, hg_norm_w, w_out):
    nb, seq, _ = x.shape
    weights = (norm_w.reshape(1, D_MODEL), w_in.astype(BF16), lb.reshape(1, HG_FDIM),
               hg_norm_w.reshape(1, HG_DV), w_out.astype(BF16))
    w_specs = [
        _const_spec((1, D_MODEL)),
        _const_spec((D_MODEL, 2 * HG_FDIM + 2 * D_MODEL)),
        _const_spec((1, HG_FDIM)),
        _const_spec((1, HG_DV)),
        _const_spec((D_MODEL, D_MODEL)),
    ]
    s_shape = jax.ShapeDtypeStruct(s0.shape, F32)
    if seq >= HG_CHUNK:
        chunk = HG_CHUNK
        lc = HG_CHUNKS_PER_STEP * chunk
        x_spec = pl.BlockSpec((1, lc, D_MODEL), lambda b, c: (b, c, 0))
        s_spec = pl.BlockSpec((1, HG_HEADS, HG_DK, HG_DV), lambda b, c: (b, 0, 0, 0))
        out, s_new = pl.pallas_call(
            functools.partial(_hgrn_prompt_kernel, lc=lc, chunk=chunk),
            grid=(nb, seq // lc),
            in_specs=[x_spec, s_spec] + w_specs,
            out_specs=(x_spec, s_spec),
            out_shape=(jax.ShapeDtypeStruct(x.shape, F32), s_shape),
            scratch_shapes=[
                pltpu.VMEM((HG_HEADS, HG_DV, HG_DK), F32),
                pltpu.VMEM((lc, D_MODEL), F32),
            ],
            compiler_params=pltpu.CompilerParams(
                dimension_semantics=("arbitrary", "arbitrary"), vmem_limit_bytes=VMEM_LIMIT_BYTES),
            name="hgrn_prompt",
        )(x, s0, *weights)
        return out, s_new
    assert SUBLANES % seq == 0 and nb % (SUBLANES // seq) == 0
    nseg = SUBLANES // seq
    rows = nb * seq
    x_spec = _const_spec((rows, D_MODEL))
    s_spec = pl.BlockSpec((nseg, HG_HEADS, HG_DK, HG_DV), lambda p: (p, 0, 0, 0))
    out, s_new = pl.pallas_call(
        functools.partial(_hgrn_sample_kernel, seq=seq),
        grid=(nb // nseg,),
        in_specs=[x_spec, s_spec] + w_specs,
        out_specs=(pl.BlockSpec((rows, D_MODEL), lambda p: (0, 0)), s_spec),
        out_shape=(jax.ShapeDtypeStruct((rows, D_MODEL), F32), s_shape),
        scratch_shapes=[
            pltpu.VMEM((nseg, HG_HEADS, HG_DV, HG_DK), F32),
            pltpu.VMEM((rows, 2 * HG_FDIM + 2 * D_MODEL), F32),
            pltpu.VMEM((rows, D_MODEL), F32),
        ],
        compiler_params=pltpu.CompilerParams(
            dimension_semantics=("arbitrary",), vmem_limit_bytes=VMEM_LIMIT_BYTES),
        name="hgrn_sample",
    )(x.reshape(rows, D_MODEL), s0, *weights)
    return out.reshape(x.shape), s_new


def _compare_exchange(v, i, j):
    hi = jnp.maximum(v[i], v[j])
    lo = jnp.minimum(v[i], v[j])
    v[i] = hi
    v[j] = lo


def _bitonic_merge_desc(v):
    n = len(v)
    j = n // 2
    while j >= 1:
        for i in range(n):
            if i & j == 0:
                _compare_exchange(v, i, i + j)
        j //= 2


_SORT16 = (
    (0, 13), (1, 12), (2, 15), (3, 14), (4, 8), (5, 6), (7, 11), (9, 10),
    (0, 5), (1, 7), (2, 9), (3, 4), (6, 13), (8, 14), (10, 15), (11, 12),
    (0, 1), (2, 3), (4, 5), (6, 8), (7, 9), (10, 11), (12, 13), (14, 15),
    (0, 2), (1, 3), (4, 10), (5, 11), (6, 7), (8, 9), (12, 14), (13, 15),
    (1, 2), (3, 12), (4, 6), (5, 7), (8, 10), (9, 11), (13, 14),
    (1, 4), (2, 6), (5, 8), (7, 10), (9, 13), (11, 14),
    (2, 4), (3, 6), (9, 12), (11, 13),
    (3, 5), (6, 8), (7, 9), (10, 12),
    (3, 4), (5, 6), (7, 8), (9, 10), (11, 12),
    (6, 7), (8, 9),
)


def _sort_desc(v):
    assert len(v) == PEER_TOPK
    for i, j in _SORT16:
        _compare_exchange(v, i, j)
    return v


def _merge_top(a, b):
    n = len(a)
    v = [jnp.maximum(a[r], b[n - 1 - r]) for r in range(n)]
    _bitonic_merge_desc(v)
    return v


def _top16_rows(s):
    v = [s[SUBLANES * j:SUBLANES * (j + 1), :] for j in range(PEER_NKEYS // SUBLANES)]
    v = _sort_desc(v)
    for shift in (1, 2, 4):
        v = _merge_top(v, [pltpu.roll(x, shift, axis=0) for x in v])
    return v


def _candidate_top(a, b):
    n = PEER_TOPK
    low = jnp.full(a[0].shape, -jnp.inf, F32)
    runs = [
        [a[i] + b[0] for i in range(1, n)],
        [a[1] + b[j] for j in range(1, 8)],
        [a[2] + b[j] for j in range(1, 5)],
        [a[3] + b[j] for j in range(1, 4)],
        [a[4] + b[j] for j in range(1, 3)],
        [a[i] + b[1] for i in range(5, 8)],
    ]
    assert all((i + 1) * (j + 1) > n for i in range(1, n) for j in range(1, n)
               if not ((i == 1 and j < 8) or (i == 2 and j < 5) or (i == 3 and j < 4)
                       or (i == 4 and j < 3) or (5 <= i < 8 and j == 1)))
    top = [a[0] + b[j] for j in range(n)]
    for run in runs:
        top = _merge_top(top, run + [low] * (n - len(run)))
    return top


def _staircase_cells():
    return [(i, j) for i in range(PEER_TOPK) for j in range(PEER_TOPK) if (i + 1) * (j + 1) <= PEER_TOPK]


def _staircase_count(a, b, thr):
    n = None
    for i, j in _staircase_cells():
        hit = jnp.where(a[i] + b[j] >= thr, 1.0, 0.0)
        n = hit if n is None else n + hit
    return n


def _peer_select_kernel(x_ref, nw_ref, wq_ref, k1_ref, k2_ref,
                        hb_ref, s1_ref, s2_ref, thr_ref, tie_ref, t1_ref, t2_ref):
    hb = _rmsnorm(x_ref[...], nw_ref[...]).astype(BF16)
    hb_ref[...] = hb
    qb = _dot(hb, wq_ref[...]).astype(BF16)
    k1 = k1_ref[...]
    k2 = k2_ref[...]
    log2e = 1.0 / math.log(2.0)
    tied = None
    for hd in range(PEER_HEADS):
        base = hd * PEER_DKEY
        s1 = _dot_nt(k1, qb[:, base:base + PEER_HALF]) * log2e
        s2 = _dot_nt(k2, qb[:, base + PEER_HALF:base + PEER_DKEY]) * log2e
        s1_ref[hd] = s1
        s2_ref[hd] = s2
        for s, t_ref in ((s1, t1_ref), (s2, t2_ref)):
            top = _top16_rows(s)
            for r in range(PEER_TOPK):
                t_ref[r, hd:hd + 1, :] = top[r][0:1, :]
            n_top = jnp.sum(jnp.where(s >= top[PEER_TOPK - 1][0:1, :], 1.0, 0.0), axis=0, keepdims=True)
            over = jnp.where(n_top > PEER_TOPK, 1.0, 0.0)
            tied = over if tied is None else jnp.maximum(tied, over)
    a = [t1_ref[r] for r in range(PEER_TOPK)]
    b = [t2_ref[r] for r in range(PEER_TOPK)]
    top = _candidate_top(a, b)
    z = jnp.exp2(top[0] - top[0])
    for r in range(1, PEER_TOPK):
        z = z + jnp.exp2(top[r] - top[0])
    shift = top[0] + jnp.log2(z) + 1.0
    for hd in range(PEER_HEADS):
        s1_ref[hd] = s1_ref[hd] - shift[hd:hd + 1, :]
    a_shifted = [v - shift for v in a]
    thr = _candidate_top(a_shifted, b)[PEER_TOPK - 1]
    thr_ref[...] = thr
    n_sel = _staircase_count(a, b, top[PEER_TOPK - 1])
    over = jnp.max(jnp.where(n_sel > PEER_TOPK, 1.0, 0.0), axis=0, keepdims=True)
    flag = jnp.max(jnp.maximum(tied, over), axis=1, keepdims=True)
    tie_ref[0] = jnp.broadcast_to(flag, tie_ref.shape[1:])


def _ordered_rank(s, top):
    n, w = s.shape
    ri = lax.broadcasted_iota(jnp.int32, (n, n), 0)
    ci = lax.broadcasted_iota(jnp.int32, (n, n), 1)
    lower = jnp.where(ci < ri, 1.0, 0.0).astype(BF16)
    larger = None
    equal = []
    for t in top:
        hit = jnp.where(t > s, 1.0, 0.0)
        larger = hit if larger is None else larger + hit
        equal.append(jnp.where(s == t, 1.0, 0.0))
    before = _dot(lower, jnp.concatenate(equal, axis=1).astype(BF16))
    same = None
    for r, eq in enumerate(equal):
        cnt = eq * before[:, r * w:(r + 1) * w]
        same = cnt if same is None else jnp.maximum(same, cnt)
    return jnp.minimum(larger + same, float(PEER_TOPK))


def _peer_select_ordered_kernel(tied_ref, x_ref, nw_ref, wq_ref, k1_ref, k2_ref,
                                c1_ref, r2_ref, t1_ref, t2_ref, r1_ref):
    tied = tied_ref[pl.program_id(0)] != 0

    @pl.when(jnp.logical_not(tied))
    def _():
        c1_ref[...] = jnp.zeros_like(c1_ref)
        r2_ref[...] = jnp.zeros_like(r2_ref)

    @pl.when(tied)
    def _():
        _ordered_selection(x_ref, nw_ref, wq_ref, k1_ref, k2_ref, c1_ref, r2_ref, t1_ref, t2_ref, r1_ref)


def _ordered_selection(x_ref, nw_ref, wq_ref, k1_ref, k2_ref, c1_ref, r2_ref, t1_ref, t2_ref, r1_ref):
    hb = _rmsnorm(x_ref[...], nw_ref[...]).astype(BF16)
    qb = _dot(hb, wq_ref[...]).astype(BF16)
    k1 = k1_ref[...]
    k2 = k2_ref[...]
    log2e = 1.0 / math.log(2.0)
    for hd in range(PEER_HEADS):
        base = hd * PEER_DKEY
        s1 = _dot_nt(k1, qb[:, base:base + PEER_HALF]) * log2e
        s2 = _dot_nt(k2, qb[:, base + PEER_HALF:base + PEER_DKEY]) * log2e
        for s, t_ref, rank_ref in ((s1, t1_ref, r1_ref), (s2, t2_ref, r2_ref)):
            top = [v[0:1, :] for v in _top16_rows(s)]
            rank_ref[hd] = _ordered_rank(s, top)
            for r in range(PEER_TOPK):
                t_ref[r, hd:hd + 1, :] = top[r]
    a = [t1_ref[r] for r in range(PEER_TOPK)]
    b = [t2_ref[r] for r in range(PEER_TOPK)]
    cells = _staircase_cells()
    sums = [a[i] + b[j] for i, j in cells]
    ahead = [None] * len(cells)

    def bump(k, hit):
        ahead[k] = hit if ahead[k] is None else ahead[k] + hit

    for p in range(len(cells)):
        for q in range(p + 1, len(cells)):
            bump(q, jnp.where(sums[p] >= sums[q], 1.0, 0.0))
            bump(p, jnp.where(sums[q] > sums[p], 1.0, 0.0))
    count = [None] * PEER_TOPK
    for (i, j), n in zip(cells, ahead):
        ch = jnp.where(n < PEER_TOPK, 1.0, 0.0)
        count[i] = ch if count[i] is None else count[i] + ch
    for hd in range(PEER_HEADS):
        row = slice(hd, hd + 1)
        rank1 = r1_ref[hd]
        c1 = jnp.zeros(rank1.shape, F32)
        for r in range(PEER_TOPK):
            c1 = jnp.where(rank1 == r, count[r][row, :], c1)
        c1_ref[hd] = c1


def _peer_expert_kernel(tied_ref, hb_ref, x_ref, s1_ref, s2_ref, thr_ref, c1_ref, r2_ref,
                        u_ref, vp_ref, v_ref, fw_ref, out_ref, acc_ref, *tile_refs, tb, eb, tw, final_norm):
    e = pl.program_id(1)
    tied = tied_ref[pl.program_id(0)] != 0
    ntile = tb // tw
    ht_refs, at_refs = tile_refs[:ntile], tile_refs[ntile:]

    @pl.when(e == 0)
    def _():
        acc_ref[...] = jnp.zeros_like(acc_ref)
        for at_ref in at_refs:
            at_ref[...] = jnp.zeros_like(at_ref)

    def step(ordered):
        u = u_ref[...]
        vp = vp_ref[...]
        for i in range(ntile):
            ht_refs[i][...] = _dot_nt(u, hb_ref[i * tw:(i + 1) * tw, :])
            acc_ref[:, i * tw:(i + 1) * tw] += _dot(vp, at_refs[i][...])
        for i in range(ntile):
            for lt in range(tw // LANES):
                lcol = slice(lt * LANES, (lt + 1) * LANES)
                col = slice(i * tw + lt * LANES, i * tw + (lt + 1) * LANES)
                for r in range(eb // PEER_NKEYS):
                    gate = None
                    for hd in range(PEER_HEADS):
                        t = s2_ref[hd, :, col] + s1_ref[hd, r:r + 1, col]
                        if ordered:
                            picked = r2_ref[hd, :, col] < c1_ref[hd, r:r + 1, col]
                        else:
                            picked = t >= thr_ref[hd:hd + 1, col]
                        g = jnp.where(picked, jnp.exp2(t), 0.0)
                        gate = g if gate is None else gate + g
                    rr = slice(r * PEER_NKEYS, (r + 1) * PEER_NKEYS)
                    h = ht_refs[i][rr, lcol]
                    at_refs[i][rr, lcol] = (
                        h * (1.0 + lax.erf(h * (1.0 / math.sqrt(2.0)))) * gate).astype(BF16)

    pl.when(jnp.logical_not(tied))(functools.partial(step, False))
    pl.when(tied)(functools.partial(step, True))

    @pl.when(e == pl.num_programs(1) - 1)
    def _():
        v = v_ref[...]
        for i in range(ntile):
            acc_ref[:, i * tw:(i + 1) * tw] += _dot(v, at_refs[i][...])
        y = x_ref[...] + acc_ref[...].T
        if final_norm:
            y = _rmsnorm(y, fw_ref[...])
        out_ref[...] = y


def _peer_layer(x, norm_w, w_q, keys1, keys2, u_tab, v_tab, final_w, *, layer, final_norm):
    t_tot = x.shape[0]
    tbs = min(PEER_SELECT_TOKENS, t_tot)
    tb = min(PEER_EXPERT_TOKENS, t_tot)
    eb = PEER_EXPERT_BLOCK
    tw = PEER_TOKEN_TILE
    nsel = t_tot // tbs
    select_in = (x, norm_w.reshape(1, D_MODEL), w_q.astype(BF16), keys1.astype(BF16), keys2.astype(BF16))
    select_specs = [
        pl.BlockSpec((tbs, D_MODEL), lambda i: (i, 0)),
        _const_spec((1, D_MODEL)),
        _const_spec((D_MODEL, PEER_HEADS * PEER_DKEY)),
        _const_spec((PEER_NKEYS, PEER_HALF)),
        _const_spec((PEER_NKEYS, PEER_HALF)),
    ]
    select_params = pltpu.CompilerParams(dimension_semantics=("arbitrary",), vmem_limit_bytes=VMEM_LIMIT_BYTES)
    key_spec = pl.BlockSpec((PEER_HEADS, PEER_NKEYS, tbs), lambda i: (0, 0, i))
    key_f32 = jax.ShapeDtypeStruct((PEER_HEADS, PEER_NKEYS, t_tot), F32)
    top_scratch = pltpu.VMEM((PEER_TOPK, PEER_HEADS, tbs), F32)
    hb, s1, s2, thr, tie = pl.pallas_call(
        _peer_select_kernel,
        grid=(nsel,),
        in_specs=select_specs,
        out_specs=(pl.BlockSpec((tbs, D_MODEL), lambda i: (i, 0)), key_spec, key_spec,
                   pl.BlockSpec((PEER_HEADS, tbs), lambda i: (0, i)),
                   pl.BlockSpec((1, SUBLANES, LANES), lambda i: (i, 0, 0))),
        out_shape=(jax.ShapeDtypeStruct((t_tot, D_MODEL), BF16), key_f32, key_f32,
                   jax.ShapeDtypeStruct((PEER_HEADS, t_tot), F32),
                   jax.ShapeDtypeStruct((nsel, SUBLANES, LANES), F32)),
        scratch_shapes=[top_scratch, top_scratch],
        compiler_params=select_params,
        name="peer_select",
    )(*select_in)

    tied_exp = jnp.max((tie[:, 0, 0] > 0.0).astype(jnp.int32).reshape(t_tot // tb, tb // tbs), axis=1)
    tied_sel = jnp.repeat(tied_exp, tb // tbs)
    c1, r2 = pl.pallas_call(
        _peer_select_ordered_kernel,
        grid_spec=pltpu.PrefetchScalarGridSpec(
            num_scalar_prefetch=1,
            grid=(nsel,),
            in_specs=[
                pl.BlockSpec((tbs, D_MODEL), lambda i, tied: (i, 0)),
                pl.BlockSpec((1, D_MODEL), lambda i, tied: (0, 0)),
                pl.BlockSpec((D_MODEL, PEER_HEADS * PEER_DKEY), lambda i, tied: (0, 0)),
                pl.BlockSpec((PEER_NKEYS, PEER_HALF), lambda i, tied: (0, 0)),
                pl.BlockSpec((PEER_NKEYS, PEER_HALF), lambda i, tied: (0, 0)),
            ],
            out_specs=(pl.BlockSpec((PEER_HEADS, PEER_NKEYS, tbs), lambda i, tied: (0, 0, i)),
                       pl.BlockSpec((PEER_HEADS, PEER_NKEYS, tbs), lambda i, tied: (0, 0, i))),
            scratch_shapes=[top_scratch, top_scratch,
                            pltpu.VMEM((PEER_HEADS, PEER_NKEYS, tbs), F32)],
        ),
        out_shape=(key_f32, key_f32),
        compiler_params=select_params,
        name="peer_select_ordered",
    )(tied_sel, *select_in)

    once = pl.Buffered(1)
    tok = lambda i, e, tied: (i, 0)
    row_spec = pl.BlockSpec((PEER_HEADS, eb // PEER_NKEYS, tb), lambda i, e, tied: (0, e, i))
    all_spec = pl.BlockSpec((PEER_HEADS, PEER_NKEYS, tb), lambda i, e, tied: (0, 0, i), pipeline_mode=once)
    return pl.pallas_call(
        functools.partial(_peer_expert_kernel, tb=tb, eb=eb, tw=tw, final_norm=final_norm),
        grid_spec=pltpu.PrefetchScalarGridSpec(
            num_scalar_prefetch=1,
            grid=(t_tot // tb, PEER_EXPERTS // eb),
            in_specs=[
                pl.BlockSpec((tb, D_MODEL), tok, pipeline_mode=once),
                pl.BlockSpec((tb, D_MODEL), tok, pipeline_mode=once),
                row_spec,
                all_spec,
                pl.BlockSpec((PEER_HEADS, tb), lambda i, e, tied: (0, i), pipeline_mode=once),
                row_spec,
                all_spec,
                pl.BlockSpec((None, eb, D_MODEL), lambda i, e, tied: (layer, e, 0)),
                pl.BlockSpec((None, D_MODEL, eb), lambda i, e, tied: (layer, 0, jnp.maximum(e - 1, 0))),
                pl.BlockSpec((None, D_MODEL, eb), lambda i, e, tied: (layer, 0, PEER_EXPERTS // eb - 1),
                             pipeline_mode=once),
                pl.BlockSpec((1, D_MODEL), lambda i, e, tied: (0, 0), pipeline_mode=once),
            ],
            out_specs=pl.BlockSpec((tb, D_MODEL), tok),
            scratch_shapes=(
                [pltpu.VMEM((D_MODEL, tb), F32)]
                + [pltpu.VMEM((eb, tw), F32) for _ in range(tb // tw)]
                + [pltpu.VMEM((eb, tw), BF16) for _ in range(tb // tw)]
            ),
        ),
        out_shape=jax.ShapeDtypeStruct((t_tot, D_MODEL), F32),
        compiler_params=pltpu.CompilerParams(
            dimension_semantics=("arbitrary", "arbitrary"), vmem_limit_bytes=VMEM_LIMIT_BYTES),
        name="peer_experts",
    )(tied_exp, hb, x, s1, s2, thr, c1, r2, u_tab, v_tab, v_tab, final_w.reshape(1, D_MODEL))


def _trunk(x, s5_re, s5_im, s_hg, p, s5_prm, lb_all):
    nb, seq, _ = x.shape
    new_re, new_im, new_hg = [], [], []
    for i in range(DEPTH):
        j = i // N_MIXERS
        if i % N_MIXERS == 0:
            a, bbd, ccd = s5_prm[j]
            x, sr, si = _s5_layer(x, s5_re[j], s5_im[j], p["norm_mix"][i], a, bbd, ccd,
                                  p["s5_d"][j], p["s5_w_glu"][j].astype(BF16), lc=min(seq, S5_TIME_CHUNK))
            new_re.append(sr)
            new_im.append(si)
        else:
            x, s_new = _hgrn_layer(x, s_hg[j], p["norm_mix"][i], p["hg_w_in"][j], lb_all[i],
                                   p["hg_norm_w"][j], p["hg_w_out"][j])
            new_hg.append(s_new)
        x = _peer_layer(x.reshape(nb * seq, D_MODEL), p["norm_ffn"][i], p["peer_w_q"][i],
                        p["peer_keys1"][i], p["peer_keys2"][i], p["peer_u"], p["peer_v"],
                        p["norm_final"], layer=i, final_norm=(i == DEPTH - 1)).reshape(nb, seq, D_MODEL)
    return x, jnp.stack(new_re), jnp.stack(new_im), jnp.stack(new_hg)


def kernel(x_prompt, x_sample, state_s5_re, state_s5_im, state_hgrn, norm_mix, norm_ffn, norm_final,
           s5_lambda_re, s5_lambda_im, s5_log_dt, s5_b_re, s5_b_im, s5_c_re, s5_c_im, s5_d, s5_w_glu,
           hg_w_in, hg_lower_bounds, hg_norm_w, hg_w_out, peer_w_q, peer_keys1, peer_keys2, peer_u, peer_v):
    p = dict(norm_mix=norm_mix, norm_ffn=norm_ffn, norm_final=norm_final, s5_d=s5_d, s5_w_glu=s5_w_glu,
             hg_w_in=hg_w_in, hg_norm_w=hg_norm_w, hg_w_out=hg_w_out, peer_w_q=peer_w_q,
             peer_keys1=peer_keys1, peer_keys2=peer_keys2,
             peer_u=peer_u.astype(BF16), peer_v=jnp.swapaxes(peer_v, 1, 2).astype(BF16))
    n_a = s5_lambda_re.shape[0]
    s5_prm = [_s5_params(s5_lambda_re[j], s5_lambda_im[j], s5_log_dt[j], s5_b_re[j], s5_b_im[j],
                         s5_c_re[j], s5_c_im[j]) for j in range(n_a)]
    lb_all = jnp.cumsum(jax.nn.softmax(hg_lower_bounds.astype(F32), axis=0), axis=0)
    lb_all = lb_all - lb_all[0:1]
    nbp = x_prompt.shape[0]
    z_s5 = jnp.zeros((n_a, nbp, S5_GROUPS, S5_STATE), state_s5_re.dtype)
    z_hg = jnp.zeros((state_hgrn.shape[0], nbp, HG_HEADS, HG_DK, HG_DV), state_hgrn.dtype)
    y_p, re_p, im_p, hg_p = _trunk(x_prompt, z_s5, z_s5, z_hg, p, s5_prm, lb_all)
    y_s, re_s, im_s, hg_s = _trunk(x_sample, state_s5_re, state_s5_im, state_hgrn, p, s5_prm, lb_all)
    return (y_p, y_s, re_p, im_p, hg_p, re_s, im_s, hg_s)
```
